```python
import math
import jax
import jax.numpy as jnp
from jax import lax
import numpy as np

D_MODEL = 1024
BATCH = 4
SEQ = 4096
DEPTH = 2
DEC_BATCH = 32
DEC_SEQ = 1
PAST_LEN = 16384
PAGE_SIZE = 128

ATT_HEADS = 4
ATT_DK = 64
ATT_DV = 2 * ATT_DK
ATT_W = ATT_HEADS * ATT_DV
NUM_BUCKETS = 32
MAX_DISTANCE = 128
Q_BLOCK = 128
SSM_D_INNER = D_MODEL
SSM_HEAD_DIM = 64
SSM_HEADS = SSM_D_INNER // SSM_HEAD_DIM
SSM_GROUPS = 2
SSM_STATE = 128
SSM_CONV_DIM = SSM_D_INNER + 2 * SSM_GROUPS * SSM_STATE
SSM_CHUNK = 128
DN_HEADS = 4
DN_DK = 128
DN_DV = 128
DN_W = DN_HEADS * DN_DV
DN_CHUNK = 64
CONV_K = 4
D_FF = 2816
N_EXPERTS = 8
TOP_K = 2
D_FF_EXPERT = 3584
MOE_BLOCK = 128
N_DENSE_LAYERS = (DEPTH + 1) // 2
N_MOE_LAYERS = DEPTH // 2
PLE_DIM = 256
EPS = 1e-6
F32 = jnp.float32

IN_SPLITS = (ATT_HEADS * 2 * ATT_DK, ATT_HEADS * 2 * ATT_DK, ATT_W,
             SSM_D_INNER, SSM_CONV_DIM, SSM_HEADS,
             DN_HEADS * DN_DK, DN_HEADS * DN_DK, DN_W, DN_W, DN_HEADS, DN_HEADS)
IN_DIM = (3 * ATT_HEADS * 2 * ATT_DK + SSM_D_INNER + SSM_CONV_DIM + SSM_HEADS
          + 2 * DN_HEADS * DN_DK + 2 * DN_W + 2 * DN_HEADS)

kernel_name = 'hybrid_diffattn_ssd_gdn_decode_step'


def rms_norm(x, w):
    xf = x.astype(F32)
    y = xf * lax.rsqrt(jnp.mean(xf * xf, axis=-1, keepdims=True) + EPS)
    return (y * w.astype(F32)).astype(x.dtype)


def l2_norm(x):
    xf = x.astype(F32)
    return xf * lax.rsqrt(jnp.sum(xf * xf, axis=-1, keepdims=True) + EPS)


def causal_conv(x, buf, w):
    xx = jnp.concatenate([buf.astype(x.dtype), x], axis=1)
    y = lax.conv_general_dilated(xx, w[:, None, :].astype(x.dtype), (1,), 'VALID',
                                 dimension_numbers=('NWC', 'WIO', 'NWC'),
                                 feature_group_count=x.shape[-1])
    return y, xx[:, xx.shape[1] - (CONV_K - 1):]


def t5_bucket(rel):
    n = jnp.maximum(-rel, 0)
    max_exact = NUM_BUCKETS // 2
    scaled = jnp.log(jnp.maximum(n, 1).astype(F32) / max_exact) / math.log(MAX_DISTANCE / max_exact)
    large = jnp.minimum(max_exact + (scaled * (NUM_BUCKETS - max_exact)).astype(jnp.int32), NUM_BUCKETS - 1)
    return jnp.where(n < max_exact, n, large)


def diff_attention(q, k, v, k_past, v_past, lam, rel_bias):
    bsz, L = q.shape[0], q.shape[1]
    P = k_past.shape[1]
    qb = math.gcd(L, Q_BLOCK)
    nb = L // qb
    q_blocks = jnp.moveaxis(q.reshape(bsz, nb, qb, ATT_HEADS, 2, ATT_DK), 1, 0)
    k_pos = jnp.arange(P + L)
    scale = ATT_DK ** -0.5

    def one_block(args):
        qblk, b = args
        q_pos = P + b * qb + jnp.arange(qb)
        s = jnp.concatenate([jnp.einsum('bqhcd,bkhcd->bhcqk', qblk, k_past),
                             jnp.einsum('bqhcd,bkhcd->bhcqk', qblk, k)], axis=-1).astype(F32) * scale
        rel = k_pos[None, :] - q_pos[:, None]
        bias = jnp.moveaxis(rel_bias[t5_bucket(rel)], -1, 0).astype(F32)
        s = jnp.where(rel <= 0, s + bias[None, :, None], -jnp.inf)
        pr = jax.nn.softmax(s, axis=-1)
        a = (pr[:, :, 0] - lam * pr[:, :, 1]).astype(v.dtype)
        return (jnp.einsum('bhqk,bkhv->bqhv', a[..., :P], v_past)
                + jnp.einsum('bhqk,bkhv->bqhv', a[..., P:], v))

    out = lax.map(one_block, (q_blocks, jnp.arange(nb)))
    return jnp.moveaxis(out, 0, 1).reshape(bsz, L, ATT_HEADS, ATT_DV)


def ssd_scan(x, step, a, bm, cm, h0):
    bsz, L, H, P = x.shape
    G, N = bm.shape[2], bm.shape[3]
    R = H // G
    cl = math.gcd(L, SSM_CHUNK)
    nc = L // cl
    xs = (x * step[..., None]).reshape(bsz, nc, cl, G, R, P)
    la = (step * a).reshape(bsz, nc, cl, G, R)
    bm = bm.reshape(bsz, nc, cl, G, N)
    cm = cm.reshape(bsz, nc, cl, G, N)
    acs = jnp.cumsum(la, axis=2)
    causal = jnp.tril(jnp.ones((cl, cl), bool))[:, :, None, None]
    seg = acs[:, :, :, None] - acs[:, :, None, :]
    decay = jnp.where(causal, jnp.exp(jnp.where(causal, seg, 0.0)), 0.0)
    cb = jnp.einsum('bclgn,bcsgn->bclsg', cm, bm)
    y_diag = jnp.einsum('bclsgr,bcsgrp->bclgrp', cb[..., None] * decay, xs)
    to_end = jnp.exp(acs[:, :, -1:] - acs)
    chunk_states = jnp.einsum('bclgn,bclgr,bclgrp->bcgrpn', bm, to_end, xs)
    chunk_decay = jnp.exp(acs[:, :, -1])

    def chunk_step(hc, inp):
        st, dc = inp
        return hc * dc[..., None, None] + st, hc

    h_last, h_start = lax.scan(chunk_step, h0.reshape(bsz, G, R, P, N),
                               (jnp.moveaxis(chunk_states, 1, 0), jnp.moveaxis(chunk_decay, 1, 0)))
    h_start = jnp.moveaxis(h_start, 0, 1)
    y_off = jnp.einsum('bclgn,bcgrpn,bclgr->bclgrp', cm, h_start, jnp.exp(acs))
    return (y_diag + y_off).reshape(bsz, L, H, P), h_last.reshape(bsz, H, P, N)


def gated_delta_rule(q, k, v, g, beta, s0):
    bsz, L, H, dk = q.shape
    dv = v.shape[-1]
    cl = math.gcd(L, DN_CHUNK)
    nc = L // cl

    def to_chunks(t):
        return jnp.moveaxis(t.reshape((bsz, nc, cl) + t.shape[2:]), 3, 2)

    q, k, v, g, beta = (to_chunks(t) for t in (q, k, v, g, beta))
    gcs = jnp.cumsum(g, axis=-1)
    incl = jnp.tril(jnp.ones((cl, cl), bool))
    strict = jnp.tril(jnp.ones((cl, cl), bool), -1)
    diff = gcs[..., :, None] - gcs[..., None, :]
    decay = jnp.where(incl, jnp.exp(jnp.where(incl, diff, 0.0)), 0.0)
    kb = k * beta[..., None]
    lower = jnp.where(strict, jnp.einsum('bnhid,bnhjd->bnhij', kb, k) * decay, 0.0)
    tmat = lower + jnp.eye(cl, dtype=lower.dtype)
    rhs = jnp.concatenate([v * beta[..., None], kb * jnp.exp(gcs)[..., None]], axis=-1)
    sol = lax.linalg.triangular_solve(tmat, rhs, left_side=True, lower=True, unit_diagonal=True)
    u, w = sol[..., :dv], sol[..., dv:]
    qk = jnp.einsum('bnhid,bnhjd->bnhij', q, k) * decay
    q_dec = q * jnp.exp(gcs)[..., None]
    k_dec = k * jnp.exp(gcs[..., -1:] - gcs)[..., None]
    g_end = jnp.exp(gcs[..., -1])

    def chunk_step(s, inp):
        u_c, w_c, qk_c, qd_c, kd_c, ge_c = inp
        v_new = u_c - jnp.einsum('bhcd,bhde->bhce', w_c, s)
        o_c = jnp.einsum('bhcd,bhde->bhce', qd_c, s) + jnp.einsum('bhij,bhje->bhie', qk_c, v_new)
        s = s * ge_c[..., None, None] + jnp.einsum('bhcd,bhce->bhde', kd_c, v_new)
        return s, o_c

    xs = tuple(jnp.moveaxis(t, 1, 0) for t in (u, w, qk, q_dec, k_dec, g_end))
    s_last, o = lax.scan(chunk_step, s0, xs)
    o = jnp.transpose(o, (1, 0, 3, 2, 4)).reshape(bsz, L, H, dv)
    return o, s_last


def moe_swiglu(x, w_router, b_router, w_g, w_u, w_d):
    n, d = x.shape
    logits = jnp.dot(x, w_router).astype(F32) + b_router.astype(F32)
    top_val, top_idx = lax.top_k(logits, TOP_K)
    gate = jax.nn.softmax(top_val, axis=-1)
    n_assign = n * TOP_K
    expert = top_idx.reshape(n_assign)
    token = jnp.repeat(jnp.arange(n, dtype=jnp.int32), TOP_K)
    weight = gate.reshape(n_assign)
    order = jnp.argsort(expert)
    expert_s, token_s, weight_s = expert[order], token[order], weight[order]
    counts = jnp.bincount(expert, length=N_EXPERTS)
    padded = (counts + MOE_BLOCK - 1) // MOE_BLOCK * MOE_BLOCK
    starts = jnp.cumsum(counts) - counts
    pad_ends = jnp.cumsum(padded)
    pad_starts = pad_ends - padded
    dest = pad_starts[expert_s] + jnp.arange(n_assign) - starts[expert_s]
    n_blocks = -(-(n_assign + N_EXPERTS * (MOE_BLOCK - 1)) // MOE_BLOCK)
    rows = n_blocks * MOE_BLOCK
    row_token = jnp.full((rows,), n, jnp.int32).at[dest].set(token_s)
    row_weight = jnp.zeros((rows,), F32).at[dest].set(weight_s)
    block_expert = jnp.minimum(jnp.searchsorted(pad_ends, jnp.arange(n_blocks) * MOE_BLOCK, side='right'),
                               N_EXPERTS - 1)
    x_ext = jnp.concatenate([x, jnp.zeros((1, d), x.dtype)], axis=0)

    def run_block(args):
        tok, wt, ex = args
        xb = x_ext[tok]
        hb = jax.nn.silu(xb @ w_g[ex]) * (xb @ w_u[ex])
        return (hb @ w_d[ex]) * wt[:, None].astype(x.dtype)

    y = lax.map(run_block, (row_token.reshape(n_blocks, MOE_BLOCK),
                            row_weight.reshape(n_blocks, MOE_BLOCK), block_expert))
    return jnp.zeros((n + 1, d), y.dtype).at[row_token].add(y.reshape(rows, d))[:n]


def hybrid_layer(h, p_i, i, lw, rel_bias, k_past, v_past, ssm_conv0, ssm_h0, dn_conv0, dn_s0):
    bsz, L, _ = h.shape
    act_dtype = h.dtype
    xn = rms_norm(h, lw['ln_mix'])
    offs = np.cumsum(IN_SPLITS)[:-1].tolist()
    (aq, ak, av, sz, sxbc, sdt, dq, dk, dv, dz, db, da) = jnp.split(xn @ lw['w_in'], offs, axis=-1)

    aq = rms_norm(aq.reshape(bsz, L, ATT_HEADS, 2, ATT_DK), lw['q_norm'])
    ak = rms_norm(ak.reshape(bsz, L, ATT_HEADS, 2, ATT_DK), lw['k_norm'])
    av = av.reshape(bsz, L, ATT_HEADS, ATT_DV)
    lam_init = 0.8 - 0.6 * math.exp(-0.3 * i)
    lam = (jnp.exp(jnp.sum(lw['lam_q1'].astype(F32) * lw['lam_k1'].astype(F32)))
           - jnp.exp(jnp.sum(lw['lam_q2'].astype(F32) * lw['lam_k2'].astype(F32))) + lam_init)
    att = diff_attention(aq, ak, av, k_past, v_past, lam, rel_bias)
    att = (rms_norm(att, lw['att_subln']) * (1.0 - lam_init)).reshape(bsz, L, ATT_W)

    xbc, ssm_conv1 = causal_conv(sxbc, ssm_conv0, lw['ssm_conv_w'])
    xbc = jax.nn.silu(xbc + lw['ssm_conv_b'])
    sx, sb, sc = jnp.split(xbc, [SSM_D_INNER, SSM_D_INNER + SSM_GROUPS * SSM_STATE], axis=-1)
    sx = sx.reshape(bsz, L, SSM_HEADS, SSM_HEAD_DIM).astype(F32)
    step = jax.nn.softplus(sdt.astype(F32) + lw['ssm_dt_bias'].astype(F32))
    a = -jnp.exp(lw['ssm_a_log'].astype(F32))
    y, ssm_h1 = ssd_scan(sx, step, a,
                         sb.reshape(bsz, L, SSM_GROUPS, SSM_STATE).astype(F32),
                         sc.reshape(bsz, L, SSM_GROUPS, SSM_STATE).astype(F32),
                         ssm_h0.astype(F32))
    y = (y + lw['ssm_d'].astype(F32)[:, None] * sx).reshape(bsz, L, SSM_D_INNER) * jax.nn.silu(sz.astype(F32))
    y = rms_norm(y.reshape(bsz, L, SSM_GROUPS, SSM_D_INNER // SSM_GROUPS),
                 lw['ssm_norm'].reshape(SSM_GROUPS, SSM_D_INNER // SSM_GROUPS))
    y = y.reshape(bsz, L, SSM_D_INNER).astype(act_dtype)

    qkv, dn_conv1 = causal_conv(jnp.concatenate([dq, dk, dv], axis=-1), dn_conv0, lw['dn_conv_w'])
    dq, dk, dv = jnp.split(jax.nn.silu(qkv), [DN_HEADS * DN_DK, 2 * DN_HEADS * DN_DK], axis=-1)
    dq = l2_norm(dq.reshape(bsz, L, DN_HEADS, DN_DK)) * (DN_DK ** -0.5)
    dk = l2_norm(dk.reshape(bsz, L, DN_HEADS, DN_DK))
    dv = dv.reshape(bsz, L, DN_HEADS, DN_DV).astype(F32)
    beta = jax.nn.sigmoid(db.astype(F32))
    g = -jnp.exp(lw['dn_a_log'].astype(F32)) * jax.nn.softplus(da.astype(F32) + lw['dn_dt_bias'].astype(F32))
    o, dn_s1 = gated_delta_rule(dq, dk, dv, g, beta, dn_s0.astype(F32))
    o = rms_norm(o, lw['dn_norm']) * jax.nn.silu(dz.astype(F32).reshape(bsz, L, DN_HEADS, DN_DV))
    o = o.reshape(bsz, L, DN_W).astype(act_dtype)

    ga, gb, gc = jnp.split(jax.nn.sigmoid(xn @ lw['w_gate']), 3, axis=-1)
    merged = ga * (att @ lw['w_up_att']) + gb * (y @ lw['w_up_ssm']) + gc * (o @ lw['w_up_dn'])
    h = h + merged @ lw['w_out']

    hn = rms_norm(h, lw['ln_ffn'])
    if i % 2 == 0:
        f = (jax.nn.silu(hn @ lw['ffn_w_gate']) * (hn @ lw['ffn_w_up'])) @ lw['ffn_w_down']
    else:
        f = moe_swiglu(hn.reshape(bsz * L, D_MODEL), lw['router_w'], lw['router_b'],
                       lw['exp_w_gate'], lw['exp_w_up'], lw['exp_w_down']).reshape(bsz, L, D_MODEL)
    h = h + f

    hn = rms_norm(h, lw['ln_ple'])
    h = h + jax.nn.sigmoid(hn @ lw['ple_w_gate']) * (p_i.astype(act_dtype) @ lw['ple_w_proj'])
    new_k = ak.reshape(bsz, L, ATT_HEADS, 2 * ATT_DK)
    return h, (new_k, av, ssm_conv1, ssm_h1, dn_conv1, dn_s1)


def setup_inputs(seed: int = 0) -> dict:
    key = jax.random.key(seed)
    keys = iter(list(jax.random.split(key, 64)))

    def nrm(shape, scale=1.0):
        return jax.random.normal(next(keys), shape, F32) * scale

    def gain(shape):
        return 1.0 + nrm(shape, 0.05)

    def dt_bias(shape):
        dt = jnp.exp(jax.random.uniform(next(keys), shape, F32, math.log(1e-3), math.log(1e-1)))
        return dt + jnp.log(-jnp.expm1(-dt))

    def a_log(shape):
        return jnp.log(jax.random.uniform(next(keys), shape, F32, 1.0, 16.0))

    D = D_MODEL
    n_pages = PAST_LEN // PAGE_SIZE
    n_phys = (DEC_BATCH * n_pages * 5) // 4
    page_table = jax.random.permutation(next(keys), n_phys)[: DEC_BATCH * n_pages]
    page_table = page_table.reshape(DEC_BATCH, n_pages).astype(jnp.int32)
    return {
        'x_prompt': nrm((BATCH, SEQ, D)),
        'x_sample': nrm((DEC_BATCH, DEC_SEQ, D)),
        'cache_k': nrm((DEPTH, n_phys, PAGE_SIZE, ATT_HEADS, 2 * ATT_DK)),
        'cache_v': nrm((DEPTH, n_phys, PAGE_SIZE, ATT_HEADS, ATT_DV)),
        'state_ssm_conv': nrm((DEPTH, DEC_BATCH, CONV_K - 1, SSM_CONV_DIM)),
        'state_ssm': nrm((DEPTH, DEC_BATCH, SSM_HEADS, SSM_HEAD_DIM, SSM_STATE), 0.1),
        'state_dn_conv': nrm((DEPTH, DEC_BATCH, CONV_K - 1, 3 * DN_HEADS * DN_DK)),
        'state_dn': nrm((DEPTH, DEC_BATCH, DN_HEADS, DN_DK, DN_DV), 0.1),
        'page_table': page_table,
        'p_prompt': nrm((DEPTH, BATCH, SEQ, PLE_DIM)),
        'p_sample': nrm((DEPTH, DEC_BATCH, DEC_SEQ, PLE_DIM)),
        'ln_mix': gain((DEPTH, D)),
        'w_in': nrm((DEPTH, D, IN_DIM), D ** -0.5),
        'w_gate': nrm((DEPTH, D, 3 * D), D ** -0.5),
        'q_norm': gain((DEPTH, ATT_DK)),
        'k_norm': gain((DEPTH, ATT_DK)),
        'lam_q1': nrm((DEPTH, ATT_DK), 0.1),
        'lam_k1': nrm((DEPTH, ATT_DK), 0.1),
        'lam_q2': nrm((DEPTH, ATT_DK), 0.1),
        'lam_k2': nrm((DEPTH, ATT_DK), 0.1),
        'att_subln': gain((DEPTH, ATT_DV)),
        'rel_bias': nrm((NUM_BUCKETS, ATT_HEADS), 0.5),
        'ssm_conv_w': nrm((DEPTH, CONV_K, SSM_CONV_DIM), CONV_K ** -0.5),
        'ssm_conv_b': nrm((DEPTH, SSM_CONV_DIM), 0.02),
        'ssm_dt_bias': dt_bias((DEPTH, SSM_HEADS)),
        'ssm_a_log': a_log((DEPTH, SSM_HEADS)),
        'ssm_d': gain((DEPTH, SSM_HEADS)),
        'ssm_norm': gain((DEPTH, SSM_D_INNER)),
        'dn_conv_w': nrm((DEPTH, CONV_K, 3 * DN_HEADS * DN_DK), CONV_K ** -0.5),
        'dn_dt_bias': dt_bias((DEPTH, DN_HEADS)),
        'dn_a_log': a_log((DEPTH, DN_HEADS)),
        'dn_norm': gain((DEPTH, DN_DV)),
        'w_up_att': nrm((DEPTH, ATT_W, D), ATT_W ** -0.5),
        'w_up_ssm': nrm((DEPTH, SSM_D_INNER, D), SSM_D_INNER ** -0.5),
        'w_up_dn': nrm((DEPTH, DN_W, D), DN_W ** -0.5),
        'w_out': nrm((DEPTH, D, D), D ** -0.5),
        'ln_ffn': gain((DEPTH, D)),
        'ffn_w_gate': nrm((N_DENSE_LAYERS, D, D_FF), D ** -0.5),
        'ffn_w_up': nrm((N_DENSE_LAYERS, D, D_FF), D ** -0.5),
        'ffn_w_down': nrm((N_DENSE_LAYERS, D_FF, D), D_FF ** -0.5),
        'router_w': nrm((N_MOE_LAYERS, D, N_EXPERTS), D ** -0.5),
        'router_b': nrm((N_MOE_LAYERS, N_EXPERTS), 0.01),
        'exp_w_gate': nrm((N_MOE_LAYERS, N_EXPERTS, D, D_FF_EXPERT), D ** -0.5),
        'exp_w_up': nrm((N_MOE_LAYERS, N_EXPERTS, D, D_FF_EXPERT), D ** -0.5),
        'exp_w_down': nrm((N_MOE_LAYERS, N_EXPERTS, D_FF_EXPERT, D), D_FF_EXPERT ** -0.5),
        'ln_ple': gain((DEPTH, D)),
        'ple_w_gate': nrm((DEPTH, D, D), D ** -0.5),
        'ple_w_proj': nrm((DEPTH, PLE_DIM, D), PLE_DIM ** -0.5),
    }


def reference(x_prompt, x_sample, cache_k, cache_v, state_ssm_conv, state_ssm, state_dn_conv, state_dn,
              page_table, p_prompt, p_sample, ln_mix, w_in, w_gate, q_norm, k_norm, lam_q1, lam_k1,
              lam_q2, lam_k2, att_subln, rel_bias, ssm_conv_w, ssm_conv_b, ssm_dt_bias, ssm_a_log, ssm_d,
              ssm_norm, dn_conv_w, dn_dt_bias, dn_a_log, dn_norm, w_up_att, w_up_ssm, w_up_dn, w_out,
              ln_ffn, ffn_w_gate, ffn_w_up, ffn_w_down, router_w, router_b, exp_w_gate, exp_w_up,
              exp_w_down, ln_ple, ple_w_gate, ple_w_proj):
    bp = x_prompt.shape[0]
    bs = x_sample.shape[0]
    past_len = page_table.shape[1] * cache_k.shape[2]
    hp, hs = x_prompt, x_sample
    st_prompt, st_sample = [], []
    for i in range(DEPTH):
        lw = {'ln_mix': ln_mix[i], 'w_in': w_in[i], 'w_gate': w_gate[i], 'q_norm': q_norm[i],
              'k_norm': k_norm[i], 'lam_q1': lam_q1[i], 'lam_k1': lam_k1[i], 'lam_q2': lam_q2[i],
              'lam_k2': lam_k2[i], 'att_subln': att_subln[i], 'ssm_conv_w': ssm_conv_w[i],
              'ssm_conv_b': ssm_conv_b[i], 'ssm_dt_bias': ssm_dt_bias[i], 'ssm_a_log': ssm_a_log[i],
              'ssm_d': ssm_d[i], 'ssm_norm': ssm_norm[i], 'dn_conv_w': dn_conv_w[i],
              'dn_dt_bias': dn_dt_bias[i], 'dn_a_log': dn_a_log[i], 'dn_norm': dn_norm[i],
              'w_up_att': w_up_att[i], 'w_up_ssm': w_up_ssm[i], 'w_up_dn': w_up_dn[i], 'w_out': w_out[i],
              'ln_ffn': ln_ffn[i], 'ln_ple': ln_ple[i], 'ple_w_gate': ple_w_gate[i], 'ple_w_proj': ple_w_proj[i]}
        if i % 2 == 0:
            lw.update(ffn_w_gate=ffn_w_gate[i // 2], ffn_w_up=ffn_w_up[i // 2], ffn_w_down=ffn_w_down[i // 2])
        else:
            lw.update(router_w=router_w[i // 2], router_b=router_b[i // 2], exp_w_gate=exp_w_gate[i // 2],
                      exp_w_up=exp_w_up[i // 2], exp_w_down=exp_w_down[i // 2])
        hp, st = hybrid_layer(
            hp, p_prompt[i], i, lw, rel_bias,
            jnp.zeros((bp, 0, ATT_HEADS, 2, ATT_DK), x_prompt.dtype),
            jnp.zeros((bp, 0, ATT_HEADS, ATT_DV), x_prompt.dtype),
            jnp.zeros((bp, CONV_K - 1, SSM_CONV_DIM), x_prompt.dtype),
            jnp.zeros((bp, SSM_HEADS, SSM_HEAD_DIM, SSM_STATE), F32),
            jnp.zeros((bp, CONV_K - 1, 3 * DN_HEADS * DN_DK), x_prompt.dtype),
            jnp.zeros((bp, DN_HEADS, DN_DK, DN_DV), F32))
        st_prompt.append(st)
        k_past = cache_k[i][page_table].reshape(bs, past_len, ATT_HEADS, 2, ATT_DK)
        v_past = cache_v[i][page_table].reshape(bs, past_len, ATT_HEADS, ATT_DV)
        hs, st = hybrid_layer(hs, p_sample[i], i, lw, rel_bias, k_past, v_past,
                              state_ssm_conv[i], state_ssm[i], state_dn_conv[i], state_dn[i])
        st_sample.append(st)

    def stk(states, j):
        return jnp.stack([s[j] for s in states])

    new_k_prompt, new_v_prompt = stk(st_prompt, 0), stk(st_prompt, 1)
    new_ssm_conv_prompt, new_ssm_prompt = stk(st_prompt, 2), stk(st_prompt, 3)
    new_dn_conv_prompt, new_dn_prompt = stk(st_prompt, 4), stk(st_prompt, 5)
    new_k_sample, new_v_sample = stk(st_sample, 0), stk(st_sample, 1)
    new_ssm_conv_sample, new_ssm_sample = stk(st_sample, 2), stk(st_sample, 3)
    new_dn_conv_sample, new_dn_sample = stk(st_sample, 4), stk(st_sample, 5)
    return (hp, hs, new_k_prompt, new_v_prompt, new_ssm_conv_prompt, new_ssm_prompt,
            new_dn_conv_prompt, new_dn_prompt, new_k_sample, new_v_sample, new_ssm_conv_sample,
            new_ssm_sample, new_dn_conv_sample, new_dn_sample)
```

```python
import functools
import math

import jax
import jax.numpy as jnp
from jax import lax
from jax.experimental import pallas as pl
from jax.experimental.pallas import tpu as pltpu

F32 = jnp.float32
BF16 = jnp.bfloat16
S = jax.ShapeDtypeStruct
BS = pl.BlockSpec

D_MODEL = 1024
ATT_HEADS = 4
ATT_DK = 64
ATT_DV = 128
ATT_W = ATT_HEADS * ATT_DV
NUM_BUCKETS = 32
MAX_DISTANCE = 128
SSM_HEADS = 16
SSM_HEAD_DIM = 64
SSM_GROUPS = 2
SSM_STATE = 128
SSM_D_INNER = 1024
SSM_CONV_DIM = 1536
DN_HEADS = 4
DN_DK = 128
DN_DV = 128
DN_W = 512
DN_CONV_DIM = 1536
CONV_K = 4
N_EXPERTS = 8
EPS = 1e-6
NEG = -1e30

C_XBC, C_AQ, C_SZ, C_AK, C_AV, C_DQ, C_DK, C_DV, C_DZ = 0, 1536, 2048, 3072, 3584, 4096, 4608, 5120, 5632
PROJ_W = 6144
PS_DT, PS_BETA, PS_DECAY, PS_W = 0, 16, 20, 128

VMEM_LIMIT = 56 * 1024 * 1024
T_ATT = 512
TM = 1024
CL_SSD = 128
CL_GDN = 64
PAGES_PER_STEP = 8
TM_MOE = 512
TC_MOE = 256


def _cparams(sem):
    return pltpu.CompilerParams(dimension_semantics=sem, vmem_limit_bytes=VMEM_LIMIT)


def _dot(a, b):
    return jnp.dot(a, b, preferred_element_type=F32)


def _dot_nt(a, b):
    return lax.dot_general(a, b, (((1,), (1,)), ((), ())), preferred_element_type=F32)


def _dot_tn(a, b):
    return lax.dot_general(a, b, (((0,), (0,)), ((), ())), preferred_element_type=F32)


def _dot_hi(a, b):
    return jnp.dot(a, b, preferred_element_type=F32, precision=lax.Precision.HIGHEST)


def _split(a):
    hi = a.astype(BF16)
    lo = (a - hi.astype(F32)).astype(BF16)
    return hi, lo


def _dot3(a, b):
    ah, al = _split(a)
    bh, bl = _split(b)
    return _dot(ah, bh) + (_dot(ah, bl) + _dot(al, bh))


def _sigmoid(x):
    return 1.0 / (1.0 + jnp.exp(-x))


def _silu(x):
    return x * _sigmoid(x)


def _softplus(x):
    return jnp.maximum(x, 0.0) + jnp.log1p(jnp.exp(-jnp.abs(x)))


def _rmsnorm_body(x_ref, w_ref, o_ref):
    x = x_ref[...]
    y = x * lax.rsqrt(jnp.mean(x * x, axis=-1, keepdims=True) + EPS) * w_ref[...]
    o_ref[...] = y.astype(o_ref.dtype)


def _rmsnorm(x, w, tm):
    n, d = x.shape
    return pl.pallas_call(
        _rmsnorm_body, out_shape=S((n, d), BF16), grid=(n // tm,),
        in_specs=[BS((tm, d), lambda i: (i, 0)), BS((1, d), lambda i: (0, 0))],
        out_specs=BS((tm, d), lambda i: (i, 0)),
        compiler_params=_cparams(("parallel",)), name="rmsnorm")(x, w.reshape(1, d))


def _fused_matmul(acts, pairs, combine, extras, out_dtype, tm, tn, name):
    n = acts[0].shape[0]
    m = pairs[0][1].shape[1]
    na, npair, nex = len(acts), len(pairs), len(extras)
    act_idx = [a for a, _ in pairs]

    def body(*refs):
        a_vals = [r[...] for r in refs[:na]]
        dots = [_dot(a_vals[act_idx[j]], refs[na + j][...]) for j in range(npair)]
        ex = [r[...] for r in refs[na + npair:na + npair + nex]]
        o_ref = refs[na + npair + nex]
        o_ref[...] = combine(dots, ex).astype(o_ref.dtype)

    in_specs = ([BS((tm, a.shape[1]), lambda i, j: (i, 0)) for a in acts]
                + [BS((w.shape[0], tn), lambda i, j: (0, j)) for _, w in pairs]
                + [BS((tm, tn), lambda i, j: (i, j)) for _ in extras])
    return pl.pallas_call(
        body, out_shape=S((n, m), out_dtype), grid=(n // tm, m // tn),
        in_specs=in_specs, out_specs=BS((tm, tn), lambda i, j: (i, j)),
        compiler_params=_cparams(("parallel", "arbitrary")), name=name,
    )(*acts, *[w for _, w in pairs], *extras)


def _qkv_body(aq_ref, ak_ref, av_ref, g_ref, qw_ref, kw_ref, q_ref, kf_ref, kb_ref, vb_ref):
    g = g_ref[...]

    def gnorm(x, w):
        hi, lo = _split(x * x)
        ss = _dot(hi, g) + _dot(lo, g)
        return x * lax.rsqrt(ss * (1.0 / ATT_DK) + EPS) * w

    q = gnorm(aq_ref[...], qw_ref[...]) * (ATT_DK ** -0.5)
    k = gnorm(ak_ref[...], kw_ref[...])
    q_ref[...] = q.astype(BF16)
    kf_ref[...] = k
    kb_ref[...] = k.astype(BF16)
    vb_ref[...] = av_ref[...].astype(BF16)


def _qkv_prep(proj, q_norm, k_norm, tm):
    n = proj.shape[0]
    w = ATT_W
    gi = jnp.arange(w) // ATT_DK
    gmat = (gi[:, None] == gi[None, :]).astype(BF16)
    qw = jnp.tile(q_norm, w // ATT_DK).reshape(1, w)
    kw = jnp.tile(k_norm, w // ATT_DK).reshape(1, w)
    row = lambda c: BS((tm, w), lambda i: (i, c))
    const = lambda shp: BS(shp, lambda i: (0, 0))
    return pl.pallas_call(
        _qkv_body,
        out_shape=(S((n, w), BF16), S((n, w), F32), S((n, w), BF16), S((n, w), BF16)),
        grid=(n // tm,),
        in_specs=[row(C_AQ // w), row(C_AK // w), row(C_AV // w), const((w, w)), const((1, w)), const((1, w))],
        out_specs=(row(0), row(0), row(0), row(0)),
        compiler_params=_cparams(("parallel",)), name="qkv_prep")(proj, proj, proj, gmat, qw, kw)


def _t5_bucket(n):
    max_exact = NUM_BUCKETS // 2
    scaled = jnp.log(jnp.maximum(n, 1).astype(F32) / max_exact) / math.log(MAX_DISTANCE / max_exact)
    large = jnp.minimum(max_exact + (scaled * (NUM_BUCKETS - max_exact)).astype(jnp.int32), NUM_BUCKETS - 1)
    return jnp.where(n < max_exact, n, large)


def _bias_by_distance(rel_bias, n):
    return jnp.moveaxis(rel_bias[_t5_bucket(n)], -1, 0).astype(F32)


def _attn_body(lam_ref, cfar_ref, q_ref, k_ref, v_ref, tab_ref, sw_ref, o_ref, m_ref, l_ref, acc_ref, *, t, post_scale):
    h = pl.program_id(1)
    qi = pl.program_id(2)
    q = q_ref[...]
    lane = lax.broadcasted_iota(jnp.int32, q.shape, 1)
    zero = jnp.zeros_like(q)
    qs = (jnp.where(lane < ATT_DK, q, zero), jnp.where(lane >= ATT_DK, q, zero))
    m_ref[...] = jnp.full(m_ref.shape, -jnp.inf, F32)
    l_ref[...] = jnp.zeros(l_ref.shape, F32)
    acc_ref[...] = jnp.zeros(acc_ref.shape, F32)

    def tile(ki, bias):
        off = pl.multiple_of(ki * t, t)
        k = k_ref[pl.ds(off, t), :]
        v = v_ref[pl.ds(off, t), :]
        for c in range(2):
            s = _dot_nt(qs[c], k) + bias
            m_old = m_ref[c]
            m_new = jnp.maximum(m_old, jnp.max(s, axis=1, keepdims=True))
            p = jnp.exp(s - m_new)
            alpha = jnp.exp(m_old - m_new)
            l_ref[c] = alpha * l_ref[c] + jnp.sum(p, axis=1, keepdims=True)
            acc_ref[c] = alpha * acc_ref[c] + _dot(p.astype(BF16), v)
            m_ref[c] = m_new

    cfar = cfar_ref[h]

    def far(ki, carry):
        tile(ki, cfar)
        return carry

    lax.fori_loop(0, jnp.maximum(qi - 1, 0), far, 0)

    @pl.when(qi >= 1)
    def _():
        tile(qi - 1, tab_ref[1])

    tile(qi, tab_ref[0])
    o = acc_ref[0] * (1.0 / l_ref[0]) - lam_ref[0] * (acc_ref[1] * (1.0 / l_ref[1]))
    o = o * lax.rsqrt(jnp.mean(o * o, axis=-1, keepdims=True) + EPS) * sw_ref[...] * post_scale
    o_ref[...] = o.astype(o_ref.dtype)


def _attn_prompt(q, k, v, lam, rel_bias, subln, post_scale, bsz, seq, t):
    nq = seq // t
    r = jnp.arange(t)
    dist = r[:, None] - r[None, :]
    tab0 = jnp.where(dist >= 0, _bias_by_distance(rel_bias, jnp.maximum(dist, 0)), NEG)
    tab1 = _bias_by_distance(rel_bias, dist + t)
    tabs = jnp.stack([tab0, tab1], axis=1)
    cfar = rel_bias[NUM_BUCKETS - 1].astype(F32)
    smem = BS(memory_space=pltpu.SMEM)
    return pl.pallas_call(
        functools.partial(_attn_body, t=t, post_scale=post_scale),
        out_shape=S((bsz * seq, ATT_W), BF16), grid=(bsz, ATT_HEADS, nq),
        in_specs=[smem, smem,
                  BS((t, ATT_DV), lambda b, h, qi: (b * nq + qi, h)),
                  BS((seq, ATT_DV), lambda b, h, qi: (b, h)),
                  BS((seq, ATT_DV), lambda b, h, qi: (b, h)),
                  BS((None, 2, t, t), lambda b, h, qi: (h, 0, 0, 0)),
                  BS((1, ATT_DV), lambda b, h, qi: (0, 0))],
        out_specs=BS((t, ATT_DV), lambda b, h, qi: (b * nq + qi, h)),
        scratch_shapes=[pltpu.VMEM((2, t, 1), F32), pltpu.VMEM((2, t, 1), F32), pltpu.VMEM((2, t, ATT_DV), F32)],
        compiler_params=_cparams(("parallel", "parallel", "arbitrary")), name="attn_prompt",
    )(lam.reshape(1), cfar, q, k, v, tabs, subln.reshape(1, ATT_DV))


def _decode_body(pt_ref, lam_ref, qm_ref, kn_ref, vn_ref, btab_ref, sw_ref, *rest, pp, n_steps, post_scale):
    k_refs = rest[:pp]
    v_refs = rest[pp:2 * pp]
    o_ref = rest[2 * pp]
    m_ref, l_ref, acc_ref = rest[2 * pp + 1:]
    s_idx = pl.program_id(1)

    @pl.when(s_idx == 0)
    def _():
        m_ref[...] = jnp.full(m_ref.shape, -jnp.inf, F32)
        l_ref[...] = jnp.zeros(l_ref.shape, F32)
        acc_ref[...] = jnp.zeros(acc_ref.shape, F32)

    qm = qm_ref[...]
    n_pages = n_steps * pp
    scores = []
    for j in range(pp):
        page = s_idx * pp + j
        bias = jnp.where(page == n_pages - 1, btab_ref[0], btab_ref[1])
        scores.append(_dot_nt(qm, k_refs[j][...].astype(BF16)) + bias)
    s_all = jnp.concatenate(scores, axis=1)
    m_old = m_ref[...]
    m_new = jnp.maximum(m_old, jnp.max(s_all, axis=1, keepdims=True))
    alpha = jnp.exp(m_old - m_new)
    p_all = jnp.exp(s_all - m_new)
    l_ref[...] = alpha * l_ref[...] + jnp.sum(p_all, axis=1, keepdims=True)
    ps = p_all.shape[1] // pp
    pv = _dot(p_all[:, :ps].astype(BF16), v_refs[0][...].astype(BF16))
    for j in range(1, pp):
        pv = pv + _dot(p_all[:, j * ps:(j + 1) * ps].astype(BF16), v_refs[j][...].astype(BF16))
    acc_ref[...] = alpha * acc_ref[...] + pv
    m_ref[...] = m_new

    @pl.when(s_idx == n_steps - 1)
    def _():
        s_self = jnp.sum(qm.astype(F32) * kn_ref[...].astype(F32), axis=1, keepdims=True) + btab_ref[2][:, 0:1]
        m_o = m_ref[...]
        m_f = jnp.maximum(m_o, s_self)
        a_o = jnp.exp(m_o - m_f)
        p_s = jnp.exp(s_self - m_f)
        l_f = a_o * l_ref[...] + p_s
        acc = a_o * acc_ref[...] + p_s * vn_ref[...].astype(F32)
        o8 = acc * (1.0 / l_f)
        lam = lam_ref[0]
        outs = []
        for h in range(ATT_HEADS):
            sl = slice(h * ATT_DV, (h + 1) * ATT_DV)
            o = o8[2 * h:2 * h + 1, sl] - lam * o8[2 * h + 1:2 * h + 2, sl]
            o = o * lax.rsqrt(jnp.mean(o * o, axis=-1, keepdims=True) + EPS) * sw_ref[...] * post_scale
            outs.append(o)
        o_ref[...] = jnp.concatenate(outs, axis=1).astype(o_ref.dtype)


def _attn_decode(layer, q, k_new, v_new, cache_k, cache_v, page_table, lam, rel_bias, subln, post_scale, pp):
    bsz = q.shape[0]
    n_phys, page = cache_k.shape[1], cache_k.shape[2]
    n_pages = page_table.shape[1]
    n_steps = n_pages // pp
    ck = cache_k.reshape(cache_k.shape[0], n_phys, page, ATT_W)
    cv = cache_v.reshape(cache_v.shape[0], n_phys, page, ATT_W)
    rows = jnp.arange(2 * ATT_HEADS)
    colmask = (jnp.arange(ATT_W)[None, :] // ATT_DK == rows[:, None])
    qm = jnp.where(colmask[None], q[:, None, :], jnp.zeros((), q.dtype))
    d_last = page - jnp.arange(page)
    b_last = jnp.repeat(_bias_by_distance(rel_bias, d_last), 2, axis=0)
    b_far = jnp.broadcast_to(jnp.repeat(rel_bias[NUM_BUCKETS - 1].astype(F32), 2)[:, None], (2 * ATT_HEADS, page))
    b_self = jnp.broadcast_to(jnp.repeat(rel_bias[0].astype(F32), 2)[:, None], (2 * ATT_HEADS, page))
    btab = jnp.stack([b_last, b_far, b_self])

    def page_spec(j):
        return BS((None, None, page, ATT_W), lambda b, s, pt: (layer, pt[b, s * pp + j], 0, 0))

    const = lambda shp: BS(shp, lambda b, s, pt: tuple(0 for _ in shp))
    grid_spec = pltpu.PrefetchScalarGridSpec(
        num_scalar_prefetch=1, grid=(bsz, n_steps),
        in_specs=[BS(memory_space=pltpu.SMEM),
                  BS((None, 2 * ATT_HEADS, ATT_W), lambda b, s, pt: (b, 0, 0)),
                  BS((None, 1, ATT_W), lambda b, s, pt: (b, 0, 0)),
                  BS((None, 1, ATT_W), lambda b, s, pt: (b, 0, 0)),
                  const((3, 2 * ATT_HEADS, page)), const((1, ATT_DV))]
        + [page_spec(j) for j in range(pp)] + [page_spec(j) for j in range(pp)],
        out_specs=BS((None, 1, ATT_W), lambda b, s, pt: (b, 0, 0)),
        scratch_shapes=[pltpu.VMEM((2 * ATT_HEADS, 1), F32), pltpu.VMEM((2 * ATT_HEADS, 1), F32),
                        pltpu.VMEM((2 * ATT_HEADS, ATT_W), F32)])
    out = pl.pallas_call(
        functools.partial(_decode_body, pp=pp, n_steps=n_steps, post_scale=post_scale),
        out_shape=S((bsz, 1, ATT_W), BF16), grid_spec=grid_spec,
        compiler_params=_cparams(("parallel", "arbitrary")), name="attn_decode",
    )(page_table, lam.reshape(1), qm, k_new.reshape(bsz, 1, ATT_W), v_new.reshape(bsz, 1, ATT_W), btab,
      subln.reshape(1, ATT_DV), *([ck] * pp), *([cv] * pp))
    return out.reshape(bsz, ATT_W)


def _causal_conv(x, prev8, cw_ref, c0, c1):
    acc = x * cw_ref[CONV_K - 1:CONV_K, c0:c1]
    row8 = lax.broadcasted_iota(jnp.int32, prev8.shape, 0)
    for s in range(1, CONV_K):
        r = pltpu.roll(x, s, axis=0)
        pr = pltpu.roll(prev8, s, axis=0)
        head = jnp.where(row8 < s, pr, r[:8])
        sh = jnp.concatenate([head, r[8:]], axis=0) if x.shape[0] > 8 else head
        acc = acc + sh * cw_ref[CONV_K - 1 - s:CONV_K - s, c0:c1]
    return acc


def _tri(cl):
    ii = lax.broadcasted_iota(jnp.int32, (cl, cl), 0)
    jj = lax.broadcasted_iota(jnp.int32, (cl, cl), 1)
    return ii, jj


def _ssd_body(z_ref, xbc_ref, ps_ref, cs0_ref, h0_ref, cw_ref, cb_ref, brow_ref, bcol_ref, arow_ref, acol_ref,
              dexp_ref, nw_ref, y_ref, hl_ref, st_ref, halo_ref, yacc_ref, *, cl, nc, valid_len):
    c = pl.program_id(1)

    @pl.when(c == 0)
    def _():
        st_ref[...] = h0_ref[...]
        halo_ref[...] = cs0_ref[...]

    x_raw = xbc_ref[...]
    conv = _causal_conv(x_raw, halo_ref[...], cw_ref, 0, SSM_CONV_DIM)
    halo_ref[...] = x_raw[cl - 8:, :]
    xbc = _silu(conv + cb_ref[...])
    x = xbc[:, :SSM_D_INNER]
    ps = ps_ref[...]
    step_c = _softplus(ps + brow_ref[...])
    step_r = _softplus(ps.T + bcol_ref[...])
    if valid_len is not None:
        t_c = lax.broadcasted_iota(jnp.int32, step_c.shape, 0) + c * cl
        t_r = lax.broadcasted_iota(jnp.int32, step_r.shape, 1) + c * cl
        step_c = jnp.where(t_c < valid_len, step_c, 0.0)
        step_r = jnp.where(t_r < valid_len, step_r, 0.0)
    la_c = (step_c * -jnp.exp(arow_ref[...]))[:, PS_DT:PS_DT + SSM_HEADS]
    la_r = (step_r * -jnp.exp(acol_ref[...]))[PS_DT:PS_DT + SSM_HEADS, :]
    ii, jj = _tri(cl)
    incl = ii >= jj
    acs_c = _dot_hi(incl.astype(F32), la_c)
    acs_r = _dot_hi(la_r, (ii <= jj).astype(F32))
    hpg = SSM_HEADS // SSM_GROUPS
    for g in range(SSM_GROUPS):
        b_g = xbc[:, SSM_D_INNER + g * SSM_STATE:SSM_D_INNER + (g + 1) * SSM_STATE].astype(BF16)
        c0 = SSM_D_INNER + SSM_GROUPS * SSM_STATE + g * SSM_STATE
        c_g = xbc[:, c0:c0 + SSM_STATE].astype(BF16)
        cb = _dot_nt(c_g, b_g)
        for r in range(hpg):
            h = g * hpg + r
            col = acs_c[:, h:h + 1]
            row = acs_r[h:h + 1, :]
            dec = jnp.where(incl, jnp.exp(jnp.where(incl, col - row, 0.0)), 0.0)
            xs = x[:, h * SSM_HEAD_DIM:(h + 1) * SSM_HEAD_DIM] * step_c[:, PS_DT + h:PS_DT + h + 1]
            y_diag = _dot((cb * dec).astype(BF16), xs.astype(BF16))
            hst = st_ref[h]
            y_off = _dot_nt(c_g, hst.astype(BF16)) * jnp.exp(col)
            last = acs_c[cl - 1:cl, h:h + 1]
            to_end = jnp.exp(last - col)
            st_ref[h] = hst * jnp.exp(last) + _dot_tn((xs * to_end).astype(BF16), b_g)
            yacc_ref[:, h * SSM_HEAD_DIM:(h + 1) * SSM_HEAD_DIM] = y_diag + y_off
    y = (yacc_ref[...] + dexp_ref[...] * x) * _silu(z_ref[...])
    gw = SSM_D_INNER // SSM_GROUPS
    parts = []
    for g in range(SSM_GROUPS):
        yg = y[:, g * gw:(g + 1) * gw]
        parts.append(yg * lax.rsqrt(jnp.mean(yg * yg, axis=-1, keepdims=True) + EPS))
    y_ref[...] = (jnp.concatenate(parts, axis=1) * nw_ref[...]).astype(y_ref.dtype)

    @pl.when(c == nc - 1)
    def _():
        hl_ref[...] = st_ref[...]


def _ssd(proj, ps, conv0, h0, conv_w, conv_b, vec, ssm_d, ssm_norm, bsz, seq, cl, valid_len):
    nc = seq // cl
    n = bsz * seq
    brow, bcol, arow, acol = vec
    cs0 = jnp.pad(conv0, ((0, 0), (8 - (CONV_K - 1), 0), (0, 0)))
    dexp = jnp.repeat(ssm_d, SSM_HEAD_DIM).reshape(1, SSM_D_INNER)
    const = lambda shp: BS(shp, lambda b, c: tuple(0 for _ in shp))
    y, h_last = pl.pallas_call(
        functools.partial(_ssd_body, cl=cl, nc=nc, valid_len=valid_len),
        out_shape=(S((n, SSM_D_INNER), BF16), S(h0.shape, F32)), grid=(bsz, nc),
        in_specs=[BS((cl, SSM_D_INNER), lambda b, c: (b * nc + c, C_SZ // SSM_D_INNER)),
                  BS((cl, SSM_CONV_DIM), lambda b, c: (b * nc + c, C_XBC // SSM_CONV_DIM)),
                  BS((cl, PS_W), lambda b, c: (b * nc + c, 0)),
                  BS((None, 8, SSM_CONV_DIM), lambda b, c: (b, 0, 0)),
                  BS((None, SSM_HEADS, SSM_HEAD_DIM, SSM_STATE), lambda b, c: (b, 0, 0, 0)),
                  const((CONV_K, SSM_CONV_DIM)), const((1, SSM_CONV_DIM)),
                  const((1, PS_W)), const((PS_W, 1)), const((1, PS_W)), const((PS_W, 1)),
                  const((1, SSM_D_INNER)), const((1, SSM_D_INNER))],
        out_specs=(BS((cl, SSM_D_INNER), lambda b, c: (b * nc + c, 0)),
                   BS((None, SSM_HEADS, SSM_HEAD_DIM, SSM_STATE), lambda b, c: (b, 0, 0, 0))),
        scratch_shapes=[pltpu.VMEM((SSM_HEADS, SSM_HEAD_DIM, SSM_STATE), F32), pltpu.VMEM((8, SSM_CONV_DIM), F32),
                        pltpu.VMEM((cl, SSM_D_INNER), F32)],
        compiler_params=_cparams(("parallel", "arbitrary")), name="ssd",
    )(proj, proj, ps, cs0, h0, conv_w, conv_b.reshape(1, -1), brow, bcol, arow, acol, dexp,
      ssm_norm.reshape(1, SSM_D_INNER))
    return y, h_last


def _unit_lower_inverse(a, ii, jj, cl):
    eye = (ii == jj).astype(F32)
    inv = eye - jnp.where(((ii >> 1) == (jj >> 1)) & (ii > jj), a, 0.0)
    s = 2
    while s < cl:
        blk = ((ii // (2 * s)) == (jj // (2 * s))) & ((ii % (2 * s)) >= s) & ((jj % (2 * s)) < s)
        inv = inv - _dot3(_dot3(inv, jnp.where(blk, a, 0.0)), inv)
        s *= 2
    return inv


def _gdn_body(q_ref, k_ref, v_ref, z_ref, ps_ref, cs0_ref, s0_ref, cw_ref, brow_ref, bcol_ref, arow_ref, acol_ref,
              nw_ref, o_ref, sl_ref, st_ref, halo_ref, *, cl, nc, valid_len):
    c = pl.program_id(1)

    @pl.when(c == 0)
    def _():
        st_ref[...] = s0_ref[...]
        halo_ref[...] = cs0_ref[...]

    w = DN_W
    halo = halo_ref[...]
    raws = (q_ref[...], k_ref[...], v_ref[...])
    q, k, v = (_silu(_causal_conv(raws[p], halo[:, p * w:(p + 1) * w], cw_ref, p * w, (p + 1) * w)) for p in range(3))
    for p in range(3):
        halo_ref[:, p * w:(p + 1) * w] = raws[p][cl - 8:, :]
    z = z_ref[...]
    ps = ps_ref[...]
    g_c = -jnp.exp(arow_ref[...]) * _softplus(ps + brow_ref[...])
    ps_t = ps.T
    g_r = -jnp.exp(acol_ref[...]) * _softplus(ps_t + bcol_ref[...])
    beta = _sigmoid(ps)
    if valid_len is not None:
        t_c = lax.broadcasted_iota(jnp.int32, g_c.shape, 0) + c * cl
        t_r = lax.broadcasted_iota(jnp.int32, g_r.shape, 1) + c * cl
        g_c = jnp.where(t_c < valid_len, g_c, 0.0)
        beta = jnp.where(t_c < valid_len, beta, 0.0)
        g_r = jnp.where(t_r < valid_len, g_r, 0.0)
    ii, jj = _tri(cl)
    incl = ii >= jj
    strict = ii > jj
    gcs_c = _dot_hi(incl.astype(F32), g_c[:, 16:24])
    gcs_r = _dot_hi(g_r[16:24, :], (ii <= jj).astype(F32))
    off = PS_DECAY - 16
    for h in range(DN_HEADS):
        sl = slice(h * DN_DK, (h + 1) * DN_DK)
        qh = q[:, sl]
        kh = k[:, sl]
        qh = qh * lax.rsqrt(jnp.sum(qh * qh, axis=-1, keepdims=True) + EPS) * (DN_DK ** -0.5)
        kh = kh * lax.rsqrt(jnp.sum(kh * kh, axis=-1, keepdims=True) + EPS)
        vh = v[:, sl]
        bh = beta[:, PS_BETA + h:PS_BETA + h + 1]
        col = gcs_c[:, off + h:off + h + 1]
        row = gcs_r[off + h:off + h + 1, :]
        dec = jnp.where(incl, jnp.exp(jnp.where(incl, col - row, 0.0)), 0.0)
        kb = kh * bh
        kh16 = kh.astype(BF16)
        lower = jnp.where(strict, _dot_nt(kb.astype(BF16), kh16) * dec, 0.0)
        tinv = _unit_lower_inverse(lower, ii, jj, cl)
        e_col = jnp.exp(col)
        u = _dot3(tinv, vh * bh)
        wm = _dot3(tinv, kb * e_col)
        qk = _dot_nt(qh.astype(BF16), kh16) * dec
        last = gcs_c[cl - 1:cl, off + h:off + h + 1]
        s_old = st_ref[h]
        s16 = s_old.astype(BF16)
        v_new = u - _dot(wm.astype(BF16), s16)
        o = _dot((qh * e_col).astype(BF16), s16) + _dot(qk.astype(BF16), v_new.astype(BF16))
        k_dec = kh * jnp.exp(last - col)
        st_ref[h] = s_old * jnp.exp(last) + _dot_tn(k_dec.astype(BF16), v_new.astype(BF16))
        o = o * lax.rsqrt(jnp.mean(o * o, axis=-1, keepdims=True) + EPS) * nw_ref[...] * _silu(z[:, sl])
        o_ref[:, sl] = o.astype(o_ref.dtype)

    @pl.when(c == nc - 1)
    def _():
        sl_ref[...] = st_ref[...]


def _gdn(proj, ps, conv0, s0, conv_w, vec, dn_norm, bsz, seq, cl, valid_len):
    nc = seq // cl
    n = bsz * seq
    brow, bcol, arow, acol = vec
    cs0 = jnp.pad(conv0, ((0, 0), (8 - (CONV_K - 1), 0), (0, 0)))
    w = DN_W
    const = lambda shp: BS(shp, lambda b, c: tuple(0 for _ in shp))
    blk = lambda col: BS((cl, w), lambda b, c: (b * nc + c, col // w))
    o, s_last = pl.pallas_call(
        functools.partial(_gdn_body, cl=cl, nc=nc, valid_len=valid_len),
        out_shape=(S((n, w), BF16), S(s0.shape, F32)), grid=(bsz, nc),
        in_specs=[blk(C_DQ), blk(C_DK), blk(C_DV), blk(C_DZ),
                  BS((cl, PS_W), lambda b, c: (b * nc + c, 0)),
                  BS((None, 8, DN_CONV_DIM), lambda b, c: (b, 0, 0)),
                  BS((None, DN_HEADS, DN_DK, DN_DV), lambda b, c: (b, 0, 0, 0)),
                  const((CONV_K, DN_CONV_DIM)),
                  const((1, PS_W)), const((PS_W, 1)), const((1, PS_W)), const((PS_W, 1)), const((1, DN_DV))],
        out_specs=(BS((cl, w), lambda b, c: (b * nc + c, 0)),
                   BS((None, DN_HEADS, DN_DK, DN_DV), lambda b, c: (b, 0, 0, 0))),
        scratch_shapes=[pltpu.VMEM((DN_HEADS, DN_DK, DN_DV), F32), pltpu.VMEM((8, DN_CONV_DIM), F32)],
        compiler_params=_cparams(("parallel", "arbitrary")), name="gdn",
    )(proj, proj, proj, proj, ps, cs0, s0, conv_w, brow, bcol, arow, acol, dn_norm.reshape(1, DN_DV))
    return o, s_last


def _router_body(h_ref, lw_ref, rw_ref, rb_ref, meta_ref):
    x = h_ref[...]
    xn = x * lax.rsqrt(jnp.mean(x * x, axis=-1, keepdims=True) + EPS) * lw_ref[...]
    logits = _dot_hi(xn, rw_ref[...]) + rb_ref[...]
    lane = lax.broadcasted_iota(jnp.int32, logits.shape, 1)
    big = jnp.int32(logits.shape[1])
    m1 = jnp.max(logits, axis=1, keepdims=True)
    i1 = jnp.min(jnp.where(logits == m1, lane, big), axis=1, keepdims=True)
    rest = jnp.where(lane == i1, NEG, logits)
    m2 = jnp.max(rest, axis=1, keepdims=True)
    i2 = jnp.min(jnp.where(rest == m2, lane, big), axis=1, keepdims=True)
    e2 = jnp.exp(m2 - m1)
    g1 = 1.0 / (1.0 + e2)
    g2 = e2 * g1
    meta = jnp.where(lane == 0, i1.astype(F32), jnp.where(lane == 1, i2.astype(F32),
                     jnp.where(lane == 2, g1, jnp.where(lane == 3, g2, 0.0))))
    meta_ref[...] = meta


def _router(h, ln_w, router_w, router_b, tm):
    n, d = h.shape
    rw = jnp.pad(router_w, ((0, 0), (0, 128 - N_EXPERTS)))
    rb = jnp.pad(router_b, (0, 128 - N_EXPERTS), constant_values=NEG).reshape(1, 128)
    const = lambda shp: BS(shp, lambda i: (0, 0))
    return pl.pallas_call(
        _router_body, out_shape=S((n, 128), F32), grid=(n // tm,),
        in_specs=[BS((tm, d), lambda i: (i, 0)), const((1, d)), const((d, 128)), const((1, 128))],
        out_specs=BS((tm, 128), lambda i: (i, 0)),
        compiler_params=_cparams(("parallel",)), name="router")(h, ln_w.reshape(1, d), rw, rb)


def _experts_body(be_ref, tok_ref, nused_ref, h_hbm, lw_ref, rwt_ref, wg_ref, wu_ref, wd_ref, y_ref,
                  xg_ref, xb_ref, acc_ref, sem, *, tm, nf):
    i = pl.program_id(0)
    f = pl.program_id(1)
    used = i < nused_ref[0]

    @pl.when(used & (f == 0))
    def _():
        def row_copy(r):
            return pltpu.make_async_copy(h_hbm.at[pl.ds(tok_ref[i * tm + r], 1), :], xg_ref.at[pl.ds(r, 1), :], sem)

        def start(r, carry):
            row_copy(r).start()
            return carry

        def wait(r, carry):
            row_copy(r).wait()
            return carry

        lax.fori_loop(0, tm, start, 0)
        lax.fori_loop(0, tm, wait, 0)
        x = xg_ref[...]
        xn = x * lax.rsqrt(jnp.mean(x * x, axis=-1, keepdims=True) + EPS) * lw_ref[...]
        xb_ref[...] = xn.astype(BF16)
        acc_ref[...] = jnp.zeros(acc_ref.shape, F32)

    @pl.when(used)
    def _():
        xb = xb_ref[...]
        hid = _silu(_dot(xb, wg_ref[...])) * _dot(xb, wu_ref[...])
        acc_ref[...] += _dot(hid.astype(BF16), wd_ref[...])

    @pl.when(used & (f == nf - 1))
    def _():
        y_ref[...] = acc_ref[...] * rwt_ref[...]

    @pl.when(jnp.logical_not(used) & (f == nf - 1))
    def _():
        y_ref[...] = jnp.zeros(y_ref.shape, F32)


def _experts(h, ln_w, block_expert, row_token, n_used, row_weight, wg, wu, wd, tm, tf):
    n, d = h.shape
    rows = row_token.shape[0]
    n_blocks = rows // tm
    ff = wg.shape[2]
    nf = ff // tf
    grid_spec = pltpu.PrefetchScalarGridSpec(
        num_scalar_prefetch=3, grid=(n_blocks, nf),
        in_specs=[BS(memory_space=pl.ANY),
                  BS((1, d), lambda i, f, be, tok, nu: (0, 0)),
                  BS((tm, 1), lambda i, f, be, tok, nu: (i, 0)),
                  BS((None, d, tf), lambda i, f, be, tok, nu: (be[i], 0, f)),
                  BS((None, d, tf), lambda i, f, be, tok, nu: (be[i], 0, f)),
                  BS((None, tf, d), lambda i, f, be, tok, nu: (be[i], f, 0))],
        out_specs=BS((tm, d), lambda i, f, be, tok, nu: (i, 0)),
        scratch_shapes=[pltpu.VMEM((tm, d), F32), pltpu.VMEM((tm, d), BF16), pltpu.VMEM((tm, d), F32),
                        pltpu.SemaphoreType.DMA(())])
    return pl.pallas_call(
        functools.partial(_experts_body, tm=tm, nf=nf),
        out_shape=S((rows, d), F32), grid_spec=grid_spec,
        compiler_params=_cparams(("arbitrary", "arbitrary")), name="experts",
    )(block_expert, row_token, n_used, h, ln_w.reshape(1, d), row_weight.reshape(rows, 1), wg, wu, wd)


def _combine_body(pos_ref, y_hbm, h_ref, o_ref, ya_ref, yb_ref, sem, *, tc):
    i = pl.program_id(0)

    def copies(r):
        t = i * tc + r
        return (pltpu.make_async_copy(y_hbm.at[pl.ds(pos_ref[2 * t], 1), :], ya_ref.at[pl.ds(r, 1), :], sem),
                pltpu.make_async_copy(y_hbm.at[pl.ds(pos_ref[2 * t + 1], 1), :], yb_ref.at[pl.ds(r, 1), :], sem))

    def start(r, carry):
        a, b = copies(r)
        a.start()
        b.start()
        return carry

    def wait(r, carry):
        a, b = copies(r)
        a.wait()
        b.wait()
        return carry

    lax.fori_loop(0, tc, start, 0)
    lax.fori_loop(0, tc, wait, 0)
    o_ref[...] = h_ref[...] + (ya_ref[...] + yb_ref[...])


def _combine(h, y_sorted, pos, tc):
    n, d = h.shape
    grid_spec = pltpu.PrefetchScalarGridSpec(
        num_scalar_prefetch=1, grid=(n // tc,),
        in_specs=[BS(memory_space=pl.ANY), BS((tc, d), lambda i, pos: (i, 0))],
        out_specs=BS((tc, d), lambda i, pos: (i, 0)),
        scratch_shapes=[pltpu.VMEM((tc, d), F32), pltpu.VMEM((tc, d), F32), pltpu.SemaphoreType.DMA(())])
    return pl.pallas_call(
        functools.partial(_combine_body, tc=tc), out_shape=S((n, d), F32), grid_spec=grid_spec,
        compiler_params=_cparams(("arbitrary",)), name="moe_combine")(pos, y_sorted, h)


def _moe(h, ln_w, router_w, router_b, wg, wu, wd, tm_r, tm, tf, tc):
    n, d = h.shape
    meta = _router(h, ln_w, router_w, router_b, tm_r)
    idx = meta[:, 0:2].astype(jnp.int32)
    gate = meta[:, 2:4]
    member = jnp.sum(jax.nn.one_hot(idx, N_EXPERTS, dtype=jnp.int32), axis=1)
    before = jnp.cumsum(member, axis=0) - member
    counts = jnp.sum(member, axis=0)
    padded = (counts + tm - 1) // tm * tm
    pad_ends = jnp.cumsum(padded)
    pad_starts = pad_ends - padded
    pos = (pad_starts[idx] + jnp.take_along_axis(before, idx, axis=1)).astype(jnp.int32)
    n_blocks = -(-(2 * n + N_EXPERTS * (tm - 1)) // tm)
    rows = n_blocks * tm
    tok = jnp.broadcast_to(jnp.arange(n, dtype=jnp.int32)[:, None], (n, 2))
    row_token = jnp.zeros((rows,), jnp.int32).at[pos.reshape(-1)].set(tok.reshape(-1))
    row_weight = jnp.zeros((rows,), F32).at[pos.reshape(-1)].set(gate.reshape(-1))
    n_used = (pad_ends[-1] // tm).astype(jnp.int32).reshape(1)
    blk_start = jnp.minimum(jnp.arange(n_blocks, dtype=jnp.int32), n_used[0] - 1) * tm
    block_expert = jnp.minimum(jnp.searchsorted(pad_ends, blk_start, side="right"), N_EXPERTS - 1).astype(jnp.int32)
    y_sorted = _experts(h, ln_w, block_expert, row_token, n_used, row_weight, wg, wu, wd, tm, tf)
    return _combine(h, y_sorted, pos.reshape(-1), tc)


def _layer(i, h, p, lw, cfg, att_fn, ssm_conv0, ssm_h0, dn_conv0, dn_s0):
    bsz, seq, tm, cl_ssd, cl_gdn, pad_to, valid_len = (cfg[k] for k in
                                                      ("bsz", "seq", "tm", "cl_ssd", "cl_gdn", "pad_to", "valid_len"))
    n = bsz * seq
    xn = _rmsnorm(h, lw["ln_mix"], tm)
    ident = lambda d, e: d[0]
    proj = _fused_matmul([xn], [(0, lw["w_main"])], ident, [], F32, tm, 512, "in_proj")
    ps = _fused_matmul([xn], [(0, lw["w_small"])], ident, [], F32, tm, PS_W, "in_proj_small")

    q16, k32, k16, v16 = _qkv_prep(proj, lw["q_norm"], lw["k_norm"], tm)
    att = att_fn(q16, k16, v16)

    if pad_to is None:
        proj_r, ps_r, seq_r = proj, ps, seq
    else:
        seq_r = pad_to
        proj_r = jnp.pad(proj.reshape(bsz, seq, -1), ((0, 0), (0, pad_to - seq), (0, 0))).reshape(bsz * pad_to, -1)
        ps_r = jnp.pad(ps.reshape(bsz, seq, -1), ((0, 0), (0, pad_to - seq), (0, 0))).reshape(bsz * pad_to, -1)
    y, ssm_h1 = _ssd(proj_r, ps_r, ssm_conv0, ssm_h0, lw["ssm_conv_w"], lw["ssm_conv_b"], lw["vec"], lw["ssm_d"],
                     lw["ssm_norm"], bsz, seq_r, cl_ssd, valid_len)
    o, dn_s1 = _gdn(proj_r, ps_r, dn_conv0, dn_s0, lw["dn_conv_w"], lw["vec"], lw["dn_norm"], bsz, seq_r, cl_gdn,
                    valid_len)
    if pad_to is not None:
        y = y.reshape(bsz, pad_to, -1)[:, :seq].reshape(n, -1)
        o = o.reshape(bsz, pad_to, -1)[:, :seq].reshape(n, -1)

    def merge(d, e):
        return _sigmoid(d[0]) * d[3] + _sigmoid(d[1]) * d[4] + _sigmoid(d[2]) * d[5]

    merged = _fused_matmul([xn, att, y, o],
                           [(0, lw["w_gate_a"]), (0, lw["w_gate_b"]), (0, lw["w_gate_c"]),
                            (1, lw["w_up_att"]), (2, lw["w_up_ssm"]), (3, lw["w_up_dn"])],
                           merge, [], BF16, tm, 512, "merge")
    resid = lambda d, e: e[0] + d[0]
    h = _fused_matmul([merged], [(0, lw["w_out"])], resid, [h], F32, tm, 512, "out_proj")

    if i % 2 == 0:
        hn = _rmsnorm(h, lw["ln_ffn"], tm)
        ff = _fused_matmul([hn], [(0, lw["ffn_w_gate"]), (0, lw["ffn_w_up"])], lambda d, e: _silu(d[0]) * d[1], [],
                           BF16, min(tm, 512), 1408, "ffn_up")
        h = _fused_matmul([ff], [(0, lw["ffn_w_down"])], resid, [h], F32, min(tm, 512), 512, "ffn_down")
    else:
        h = _moe(h, lw["ln_ffn"], lw["router_w"], lw["router_b"], lw["exp_w_gate"], lw["exp_w_up"], lw["exp_w_down"],
                 min(tm, 512), cfg["tm_moe"], 512, cfg["tc"])

    hn = _rmsnorm(h, lw["ln_ple"], tm)
    h = _fused_matmul([hn, p.astype(BF16)], [(0, lw["ple_w_gate"]), (1, lw["ple_w_proj"])],
                      lambda d, e: e[0] + _sigmoid(d[0]) * d[1], [h], F32, tm, 512, "ple")

    xbc_raw = proj[:, C_XBC:C_XBC + SSM_CONV_DIM].reshape(bsz, seq, -1)
    dn_raw = proj[:, C_DQ:C_DQ + DN_CONV_DIM].reshape(bsz, seq, -1)
    keep = CONV_K - 1
    if seq >= keep:
        ssm_conv1, dn_conv1 = xbc_raw[:, seq - keep:], dn_raw[:, seq - keep:]
    else:
        ssm_conv1 = jnp.concatenate([ssm_conv0, xbc_raw], axis=1)[:, -keep:]
        dn_conv1 = jnp.concatenate([dn_conv0, dn_raw], axis=1)[:, -keep:]
    new_k = k32.reshape(bsz, seq, ATT_HEADS, 2 * ATT_DK)
    new_v = proj[:, C_AV:C_AV + ATT_W].reshape(bsz, seq, ATT_HEADS, ATT_DV)
    return h, (new_k, new_v, ssm_conv1, ssm_h1, dn_conv1, dn_s1)


def _prep_layer_weights(i, W):
    w_in = W["w_in"][i]
    o = 0
    cols = {}
    for name, width in (("aq", 512), ("ak", 512), ("av", 512), ("sz", 1024), ("xbc", 1536), ("dt", 16),
                        ("dq", 512), ("dk", 512), ("dv", 512), ("dz", 512), ("beta", 4), ("decay", 4)):
        cols[name] = w_in[:, o:o + width]
        o += width
    w_main = jnp.concatenate([cols[k] for k in ("xbc", "aq", "sz", "ak", "av", "dq", "dk", "dv", "dz")], axis=1)
    w_small = jnp.concatenate([cols["dt"], cols["beta"], cols["decay"],
                               jnp.zeros((D_MODEL, PS_W - 24), w_in.dtype)], axis=1)
    bias_row = jnp.zeros((PS_W,), F32).at[PS_DT:PS_DT + SSM_HEADS].set(W["ssm_dt_bias"][i])
    bias_row = bias_row.at[PS_DECAY:PS_DECAY + DN_HEADS].set(W["dn_dt_bias"][i])
    alog_row = jnp.zeros((PS_W,), F32).at[PS_DT:PS_DT + SSM_HEADS].set(W["ssm_a_log"][i])
    alog_row = alog_row.at[PS_DECAY:PS_DECAY + DN_HEADS].set(W["dn_a_log"][i])
    vec = (bias_row.reshape(1, PS_W), bias_row.reshape(PS_W, 1), alog_row.reshape(1, PS_W), alog_row.reshape(PS_W, 1))
    wg = W["w_gate"][i]
    lw = {
        "ln_mix": W["ln_mix"][i], "w_main": w_main.astype(BF16), "w_small": w_small.astype(BF16),
        "q_norm": W["q_norm"][i], "k_norm": W["k_norm"][i], "att_subln": W["att_subln"][i],
        "ssm_conv_w": W["ssm_conv_w"][i], "ssm_conv_b": W["ssm_conv_b"][i], "vec": vec,
        "ssm_d": W["ssm_d"][i], "ssm_norm": W["ssm_norm"][i],
        "dn_conv_w": W["dn_conv_w"][i], "dn_norm": W["dn_norm"][i],
        "w_gate_a": wg[:, :D_MODEL].astype(BF16), "w_gate_b": wg[:, D_MODEL:2 * D_MODEL].astype(BF16),
        "w_gate_c": wg[:, 2 * D_MODEL:].astype(BF16),
        "w_up_att": W["w_up_att"][i].astype(BF16), "w_up_ssm": W["w_up_ssm"][i].astype(BF16),
        "w_up_dn": W["w_up_dn"][i].astype(BF16), "w_out": W["w_out"][i].astype(BF16),
        "ln_ffn": W["ln_ffn"][i], "ln_ple": W["ln_ple"][i],
        "ple_w_gate": W["ple_w_gate"][i].astype(BF16), "ple_w_proj": W["ple_w_proj"][i].astype(BF16),
    }
    if i % 2 == 0:
        lw.update(ffn_w_gate=W["ffn_w_gate"][i // 2].astype(BF16), ffn_w_up=W["ffn_w_up"][i // 2].astype(BF16),
                  ffn_w_down=W["ffn_w_down"][i // 2].astype(BF16))
    else:
        lw.update(router_w=W["router_w"][i // 2], router_b=W["router_b"][i // 2],
                  exp_w_gate=W["exp_w_gate"][i // 2].astype(BF16), exp_w_up=W["exp_w_up"][i // 2].astype(BF16),
                  exp_w_down=W["exp_w_down"][i // 2].astype(BF16))
    lam_init = 0.8 - 0.6 * math.exp(-0.3 * i)
    lam = (jnp.exp(jnp.sum(W["lam_q1"][i] * W["lam_k1"][i])) - jnp.exp(jnp.sum(W["lam_q2"][i] * W["lam_k2"][i]))
           + lam_init).astype(F32)
    return lw, lam, lam_init


def _pick(n, pref):
    t = min(n, pref)
    while n % t:
        t //= 2
    return t


def kernel(x_prompt, x_sample, cache_k, cache_v, state_ssm_conv, state_ssm, state_dn_conv, state_dn, page_table, p_prompt, p_sample, ln_mix, w_in, w_gate, q_norm, k_norm, lam_q1, lam_k1, lam_q2, lam_k2, att_subln, rel_bias, ssm_conv_w, ssm_conv_b, ssm_dt_bias, ssm_a_log, ssm_d, ssm_norm, dn_conv_w, dn_dt_bias, dn_a_log, dn_norm, w_up_att, w_up_ssm, w_up_dn, w_out, ln_ffn, ffn_w_gate, ffn_w_up, ffn_w_down, router_w, router_b, exp_w_gate, exp_w_up, exp_w_down, ln_ple, ple_w_gate, ple_w_proj):
    W = dict(ln_mix=ln_mix, w_in=w_in, w_gate=w_gate, q_norm=q_norm, k_norm=k_norm, lam_q1=lam_q1, lam_k1=lam_k1,
             lam_q2=lam_q2, lam_k2=lam_k2, att_subln=att_subln, ssm_conv_w=ssm_conv_w, ssm_conv_b=ssm_conv_b,
             ssm_dt_bias=ssm_dt_bias, ssm_a_log=ssm_a_log, ssm_d=ssm_d, ssm_norm=ssm_norm, dn_conv_w=dn_conv_w,
             dn_dt_bias=dn_dt_bias, dn_a_log=dn_a_log, dn_norm=dn_norm, w_up_att=w_up_att, w_up_ssm=w_up_ssm,
             w_up_dn=w_up_dn, w_out=w_out, ln_ffn=ln_ffn, ffn_w_gate=ffn_w_gate, ffn_w_up=ffn_w_up,
             ffn_w_down=ffn_w_down, router_w=router_w, router_b=router_b, exp_w_gate=exp_w_gate, exp_w_up=exp_w_up,
             exp_w_down=exp_w_down, ln_ple=ln_ple, ple_w_gate=ple_w_gate, ple_w_proj=ple_w_proj)
    depth = ln_mix.shape[0]
    bp, lp, d = x_prompt.shape
    bs, ls, _ = x_sample.shape
    n_pages = page_table.shape[1]
    t_att = _pick(lp, T_ATT)
    cfg_p = dict(bsz=bp, seq=lp, tm=_pick(bp * lp, TM), cl_ssd=_pick(lp, CL_SSD), cl_gdn=_pick(lp, CL_GDN), pad_to=None,
                 valid_len=None, tm_moe=_pick(bp * lp, TM_MOE), tc=_pick(bp * lp, TC_MOE))
    cfg_s = dict(bsz=bs, seq=ls, tm=bs * ls, cl_ssd=64, cl_gdn=64, pad_to=64, valid_len=ls,
                 tm_moe=64, tc=bs * ls)
    hp = x_prompt.reshape(bp * lp, d)
    hs = x_sample.reshape(bs * ls, d)
    st_p, st_s = [], []
    for i in range(depth):
        lw, lam, lam_init = _prep_layer_weights(i, W)
        post = 1.0 - lam_init
        att_p = lambda q, k, v: _attn_prompt(q, k, v, lam, rel_bias, lw["att_subln"], post, bp, lp, t_att)
        hp, st = _layer(i, hp, p_prompt[i].reshape(bp * lp, -1), lw, cfg_p, att_p,
                        jnp.zeros((bp, CONV_K - 1, SSM_CONV_DIM), F32),
                        jnp.zeros((bp, SSM_HEADS, SSM_HEAD_DIM, SSM_STATE), F32),
                        jnp.zeros((bp, CONV_K - 1, DN_CONV_DIM), F32),
                        jnp.zeros((bp, DN_HEADS, DN_DK, DN_DV), F32))
        st_p.append(st)
        att_s = lambda q, k, v: _attn_decode(i, q, k, v, cache_k, cache_v, page_table, lam, rel_bias,
                                             lw["att_subln"], post, _pick(n_pages, PAGES_PER_STEP))
        hs, st = _layer(i, hs, p_sample[i].reshape(bs * ls, -1), lw, cfg_s, att_s,
                        state_ssm_conv[i], state_ssm[i], state_dn_conv[i], state_dn[i])
        st_s.append(st)
    stk = lambda sts, j: jnp.stack([s[j] for s in sts])
    return (hp.reshape(bp, lp, d), hs.reshape(bs, ls, d),
            stk(st_p, 0), stk(st_p, 1), stk(st_p, 2), stk(st_p, 3), stk(st_p, 4), stk(st_p, 5),
            stk(st_s, 0), stk(st_s, 1), stk(st_s, 2), stk(st_s, 3), stk(st_s, 4), stk(st_s, 5))
```

```python
import functools
import math

import jax
import jax.numpy as jnp
from jax import lax
from jax.experimental import pallas as pl
from jax.experimental.pallas import tpu as pltpu

F32 = jnp.float32
BF16 = jnp.bfloat16
S = jax.ShapeDtypeStruct
BS = pl.BlockSpec

D_MODEL = 1024
ATT_HEADS = 4
ATT_DK = 64
ATT_DV = 128
ATT_W = ATT_HEADS * ATT_DV
NUM_BUCKETS = 32
MAX_DISTANCE = 128
SSM_HEADS = 16
SSM_HEAD_DIM = 64
SSM_GROUPS = 2
SSM_STATE = 128
SSM_D_INNER = 1024
SSM_CONV_DIM = 1536
DN_HEADS = 4
DN_DK = 128
DN_DV = 128
DN_W = 512
DN_CONV_DIM = 1536
CONV_K = 4
N_EXPERTS = 8
EPS = 1e-6
NEG = -1e30
LOG2E = 1.4426950408889634

C_XBC, C_AQ, C_SZ, C_AK, C_AV, C_DQ, C_DK, C_DV, C_DZ = 0, 1536, 2048, 3072, 3584, 4096, 4608, 5120, 5632
PROJ_W = 6144
PS_DT, PS_BETA, PS_DECAY, PS_W = 0, 16, 20, 128

VMEM_LIMIT = 56 * 1024 * 1024
T_ATT = 512
TM = 1024
CL_SSD = 128
CL_GDN = 64
CPS_GDN = 2
PAGES_PER_STEP = 8
TM_MOE = 512
TC_MOE = 256


def _cparams(sem):
    return pltpu.CompilerParams(dimension_semantics=sem, vmem_limit_bytes=VMEM_LIMIT)


def _dot(a, b):
    return jnp.dot(a, b, preferred_element_type=F32)


def _dot_nt(a, b):
    return lax.dot_general(a, b, (((1,), (1,)), ((), ())), preferred_element_type=F32)


def _dot_tn(a, b):
    return lax.dot_general(a, b, (((0,), (0,)), ((), ())), preferred_element_type=F32)


def _dot_hi(a, b):
    return jnp.dot(a, b, preferred_element_type=F32, precision=lax.Precision.HIGHEST)


def _split(a):
    hi = a.astype(BF16)
    lo = (a - hi.astype(F32)).astype(BF16)
    return hi, lo


def _dot3(a, b):
    ah, al = _split(a)
    bh, bl = _split(b)
    return _dot(ah, bh) + (_dot(ah, bl) + _dot(al, bh))


def _sigmoid(x):
    return 1.0 / (1.0 + jnp.exp(-x))


def _silu(x):
    return x * _sigmoid(x)


def _softplus(x):
    return jnp.maximum(x, 0.0) + jnp.log1p(jnp.exp(-jnp.abs(x)))


def _rmsnorm_body(x_ref, w_ref, o_ref):
    x = x_ref[...]
    y = x * lax.rsqrt(jnp.mean(x * x, axis=-1, keepdims=True) + EPS) * w_ref[...]
    o_ref[...] = y.astype(o_ref.dtype)


def _rmsnorm(x, w, tm):
    n, d = x.shape
    return pl.pallas_call(
        _rmsnorm_body, out_shape=S((n, d), BF16), grid=(n // tm,),
        in_specs=[BS((tm, d), lambda i: (i, 0)), BS((1, d), lambda i: (0, 0))],
        out_specs=BS((tm, d), lambda i: (i, 0)),
        compiler_params=_cparams(("parallel",)), name="rmsnorm")(x, w.reshape(1, d))


def _fused_matmul(acts, pairs, combine, extras, out_dtype, tm, tn, name):
    n = acts[0].shape[0]
    m = pairs[0][1].shape[1]
    na, npair, nex = len(acts), len(pairs), len(extras)
    act_idx = [a for a, _ in pairs]

    def body(*refs):
        a_vals = [r[...] for r in refs[:na]]
        dots = [_dot(a_vals[act_idx[j]], refs[na + j][...]) for j in range(npair)]
        ex = [r[...] for r in refs[na + npair:na + npair + nex]]
        o_ref = refs[na + npair + nex]
        o_ref[...] = combine(dots, ex).astype(o_ref.dtype)

    in_specs = ([BS((tm, a.shape[1]), lambda i, j: (i, 0)) for a in acts]
                + [BS((w.shape[0], tn), lambda i, j: (0, j)) for _, w in pairs]
                + [BS((tm, tn), lambda i, j: (i, j)) for _ in extras])
    return pl.pallas_call(
        body, out_shape=S((n, m), out_dtype), grid=(n // tm, m // tn),
        in_specs=in_specs, out_specs=BS((tm, tn), lambda i, j: (i, j)),
        compiler_params=_cparams(("parallel", "arbitrary")), name=name,
    )(*acts, *[w for _, w in pairs], *extras)


def _qkv_body(aq_ref, ak_ref, av_ref, g_ref, qw_ref, kw_ref, q_ref, kf_ref, kb_ref, vb_ref):
    g = g_ref[...]

    def gnorm(x, w):
        hi, lo = _split(x * x)
        ss = _dot(hi, g) + _dot(lo, g)
        return x * lax.rsqrt(ss * (1.0 / ATT_DK) + EPS) * w

    q = gnorm(aq_ref[...], qw_ref[...]) * (ATT_DK ** -0.5 * LOG2E)
    k = gnorm(ak_ref[...], kw_ref[...])
    q_ref[...] = q.astype(BF16)
    kf_ref[...] = k
    kb_ref[...] = k.astype(BF16)
    vb_ref[...] = av_ref[...].astype(BF16)


def _qkv_prep(proj, q_norm, k_norm, tm):
    n = proj.shape[0]
    w = ATT_W
    gi = jnp.arange(w) // ATT_DK
    gmat = (gi[:, None] == gi[None, :]).astype(BF16)
    qw = jnp.tile(q_norm, w // ATT_DK).reshape(1, w)
    kw = jnp.tile(k_norm, w // ATT_DK).reshape(1, w)
    row = lambda c: BS((tm, w), lambda i: (i, c))
    const = lambda shp: BS(shp, lambda i: (0, 0))
    return pl.pallas_call(
        _qkv_body,
        out_shape=(S((n, w), BF16), S((n, w), F32), S((n, w), BF16), S((n, w), BF16)),
        grid=(n // tm,),
        in_specs=[row(C_AQ // w), row(C_AK // w), row(C_AV // w), const((w, w)), const((1, w)), const((1, w))],
        out_specs=(row(0), row(0), row(0), row(0)),
        compiler_params=_cparams(("parallel",)), name="qkv_prep")(proj, proj, proj, gmat, qw, kw)


def _t5_bucket(n):
    max_exact = NUM_BUCKETS // 2
    scaled = jnp.log(jnp.maximum(n, 1).astype(F32) / max_exact) / math.log(MAX_DISTANCE / max_exact)
    large = jnp.minimum(max_exact + (scaled * (NUM_BUCKETS - max_exact)).astype(jnp.int32), NUM_BUCKETS - 1)
    return jnp.where(n < max_exact, n, large)


def _bias_by_distance(rel_bias, n):
    return jnp.moveaxis(rel_bias[_t5_bucket(n)], -1, 0).astype(F32)


def _bias_table_body(rb_ref, o_ref, *, t, rows):
    h = pl.program_id(0)
    max_exact = NUM_BUCKETS // 2

    def block(i, carry):
        r0 = pl.multiple_of(i * rows, rows)
        r = lax.broadcasted_iota(jnp.int32, (rows, t), 0) + r0
        c = lax.broadcasted_iota(jnp.int32, (rows, t), 1)
        for idx in range(2):
            n = r - c + idx * t
            nn = jnp.maximum(n, 0)
            scaled = jnp.log(jnp.maximum(nn, 1).astype(F32) / max_exact) / math.log(MAX_DISTANCE / max_exact)
            large = jnp.minimum(max_exact + (scaled * (NUM_BUCKETS - max_exact)).astype(jnp.int32), NUM_BUCKETS - 1)
            bucket = jnp.where(nn < max_exact, nn, large)
            val = jnp.zeros((rows, t), F32)
            for b in range(NUM_BUCKETS):
                val = jnp.where(bucket == b, rb_ref[b, h] * LOG2E, val)
            if idx == 0:
                val = jnp.where(n >= 0, val, NEG)
            o_ref[idx, pl.ds(r0, rows), :] = val
        return carry

    lax.fori_loop(0, t // rows, block, 0)


def _bias_tables(rel_bias, t):
    return pl.pallas_call(
        functools.partial(_bias_table_body, t=t, rows=8),
        out_shape=S((ATT_HEADS, 2, t, t), F32), grid=(ATT_HEADS,),
        in_specs=[BS(memory_space=pltpu.SMEM)],
        out_specs=BS((None, 2, t, t), lambda h: (h, 0, 0, 0)),
        compiler_params=_cparams(("parallel",)), name="bias_tables")(rel_bias.astype(F32))


def _attn_body(lam_ref, cfar_ref, q_ref, k_ref, v_ref, tab_ref, sw_ref, o_ref,
               q2_ref, s_ref, p_ref, m_ref, l_ref, al_ref, acc_ref, *, t, rb, post_scale):
    h = pl.program_id(1)
    qi = pl.program_id(2)
    q = q_ref[...]
    lane = lax.broadcasted_iota(jnp.int32, q.shape, 1)
    zero = jnp.zeros_like(q)
    q2_ref[0:t, :] = jnp.where(lane < ATT_DK, q, zero)
    q2_ref[t:2 * t, :] = jnp.where(lane >= ATT_DK, q, zero)
    m_ref[...] = jnp.full(m_ref.shape, -jnp.inf, F32)
    l_ref[...] = jnp.zeros(l_ref.shape, F32)
    acc_ref[...] = jnp.zeros(acc_ref.shape, F32)
    cfar = cfar_ref[h] * LOG2E

    def tile(ki, tab_idx):
        off = pl.multiple_of(ki * t, t)
        s_ref[...] = _dot_nt(q2_ref[...], k_ref[pl.ds(off, t), :])

        def rows_body(i, carry):
            r0 = pl.multiple_of(i * rb, rb)
            rows = pl.ds(r0, rb)
            s = s_ref[rows, :]
            m_old = m_ref[rows, :]
            if tab_idx is None:
                m_new = jnp.maximum(m_old, jnp.max(s, axis=1, keepdims=True) + cfar)
                p = jnp.exp2(s - (m_new - cfar))
            else:
                tr0 = pl.multiple_of(lax.rem(r0, t), rb)
                s = s + tab_ref[tab_idx, pl.ds(tr0, rb), :]
                m_new = jnp.maximum(m_old, jnp.max(s, axis=1, keepdims=True))
                p = jnp.exp2(s - m_new)
            alpha = jnp.exp2(m_old - m_new)
            l_ref[rows, :] = alpha * l_ref[rows, :] + jnp.sum(p, axis=1, keepdims=True)
            p_ref[rows, :] = p.astype(BF16)
            al_ref[rows, :] = alpha
            m_ref[rows, :] = m_new
            return carry

        lax.fori_loop(0, 2 * t // rb, rows_body, 0, unroll=2)
        acc_ref[...] = al_ref[...] * acc_ref[...] + _dot(p_ref[...], v_ref[pl.ds(off, t), :])

    def far(ki, carry):
        tile(ki, None)
        return carry

    lax.fori_loop(0, jnp.maximum(qi - 1, 0), far, 0)

    @pl.when(qi >= 1)
    def _():
        tile(qi - 1, 1)

    tile(qi, 0)
    o1 = acc_ref[0:t, :] * (1.0 / l_ref[0:t, :])
    o2 = acc_ref[t:2 * t, :] * (1.0 / l_ref[t:2 * t, :])
    o = o1 - lam_ref[0] * o2
    o = o * lax.rsqrt(jnp.mean(o * o, axis=-1, keepdims=True) + EPS) * sw_ref[...] * post_scale
    o_ref[...] = o.astype(o_ref.dtype)


def _attn_prompt(q, k, v, lam, rel_bias, tabs, subln, post_scale, bsz, seq, t):
    nq = seq // t
    cfar = rel_bias[NUM_BUCKETS - 1].astype(F32)
    smem = BS(memory_space=pltpu.SMEM)
    return pl.pallas_call(
        functools.partial(_attn_body, t=t, rb=min(t, 32), post_scale=post_scale),
        out_shape=S((bsz * seq, ATT_W), BF16), grid=(bsz, ATT_HEADS, nq),
        in_specs=[smem, smem,
                  BS((t, ATT_DV), lambda b, h, qi: (b * nq + qi, h)),
                  BS((seq, ATT_DV), lambda b, h, qi: (b, h)),
                  BS((seq, ATT_DV), lambda b, h, qi: (b, h)),
                  BS((None, 2, t, t), lambda b, h, qi: (h, 0, 0, 0)),
                  BS((1, ATT_DV), lambda b, h, qi: (0, 0))],
        out_specs=BS((t, ATT_DV), lambda b, h, qi: (b * nq + qi, h)),
        scratch_shapes=[pltpu.VMEM((2 * t, ATT_DV), BF16), pltpu.VMEM((2 * t, t), F32), pltpu.VMEM((2 * t, t), BF16),
                        pltpu.VMEM((2 * t, 1), F32), pltpu.VMEM((2 * t, 1), F32), pltpu.VMEM((2 * t, 1), F32),
                        pltpu.VMEM((2 * t, ATT_DV), F32)],
        compiler_params=_cparams(("parallel", "parallel", "arbitrary")), name="attn_prompt",
    )(lam.reshape(1), cfar, q, k, v, tabs, subln.reshape(1, ATT_DV))


def _decode_body(pt_ref, lam_ref, q8_ref, kn_ref, vn_ref, btab_ref, sw_ref, *rest, pp, n_steps, post_scale):
    k_refs = rest[:pp]
    v_refs = rest[pp:2 * pp]
    o_ref = rest[2 * pp]
    m_ref, l_ref, acc_ref = rest[2 * pp + 1:]
    s_idx = pl.program_id(1)

    @pl.when(s_idx == 0)
    def _():
        m_ref[...] = jnp.full(m_ref.shape, -jnp.inf, F32)
        l_ref[...] = jnp.zeros(l_ref.shape, F32)
        acc_ref[...] = jnp.zeros(acc_ref.shape, F32)

    q8 = q8_ref[...]
    n_pages = n_steps * pp
    scores = []
    for j in range(pp):
        page = s_idx * pp + j
        bias = jnp.where(page == n_pages - 1, btab_ref[0], btab_ref[1])
        scores.append(_dot_nt(q8, k_refs[j][...].astype(BF16)) + bias)
    s_all = jnp.concatenate(scores, axis=1)
    m_old = m_ref[...]
    m_new = jnp.maximum(m_old, jnp.max(s_all, axis=1, keepdims=True))
    alpha = jnp.exp2(m_old - m_new)
    p_all = jnp.exp2(s_all - m_new)
    l_ref[...] = alpha * l_ref[...] + jnp.sum(p_all, axis=1, keepdims=True)
    ps = p_all.shape[1] // pp
    p16 = p_all.astype(BF16)
    pv = _dot(p16[:, :ps], v_refs[0][...].astype(BF16))
    for j in range(1, pp):
        pv = pv + _dot(p16[:, j * ps:(j + 1) * ps], v_refs[j][...].astype(BF16))
    acc_ref[...] = alpha * acc_ref[...] + pv
    m_ref[...] = m_new

    @pl.when(s_idx == n_steps - 1)
    def _():
        s_self = (jnp.sum(q8.astype(F32) * kn_ref[...].astype(F32), axis=1, keepdims=True)
                  + btab_ref[2][:, 0:1])
        m_o = m_ref[...]
        m_f = jnp.maximum(m_o, s_self)
        a_o = jnp.exp2(m_o - m_f)
        p_s = jnp.exp2(s_self - m_f)
        l_f = a_o * l_ref[...] + p_s
        acc = a_o * acc_ref[...] + p_s * vn_ref[...].astype(F32)
        o8 = acc * (1.0 / l_f)
        lam = lam_ref[0]
        outs = []
        for h in range(ATT_HEADS):
            o = o8[2 * h:2 * h + 1, :] - lam * o8[2 * h + 1:2 * h + 2, :]
            o = o * lax.rsqrt(jnp.mean(o * o, axis=-1, keepdims=True) + EPS) * sw_ref[...] * post_scale
            outs.append(o)
        o_ref[...] = jnp.concatenate(outs, axis=1).astype(o_ref.dtype)


def _attn_decode(layer, q, k_new, v_new, cache_k, cache_v, page_table, lam, rel_bias, subln, post_scale, pp):
    bsz = q.shape[0]
    depth, n_phys, page = cache_k.shape[0], cache_k.shape[1], cache_k.shape[2]
    n_pages = page_table.shape[1]
    n_steps = n_pages // pp
    prow = page * ATT_HEADS
    ck = cache_k.reshape(depth, n_phys, prow, ATT_DV)
    cv = cache_v.reshape(depth, n_phys, prow, ATT_DV)
    rows = jnp.arange(2 * ATT_HEADS)
    qh = jnp.repeat(q.reshape(bsz, ATT_HEADS, ATT_DV), 2, axis=1)
    q8 = jnp.where((jnp.arange(ATT_DV)[None, :] // ATT_DK == rows[:, None] % 2)[None], qh, jnp.zeros((), q.dtype))
    kn8 = jnp.repeat(k_new.reshape(bsz, ATT_HEADS, ATT_DV), 2, axis=1)
    vn8 = jnp.repeat(v_new.reshape(bsz, ATT_HEADS, ATT_DV), 2, axis=1)
    same = (jnp.arange(prow)[None, :] % ATT_HEADS) == (rows[:, None] // 2)
    d_last = page - jnp.arange(prow) // ATT_HEADS
    b_last = jnp.repeat(_bias_by_distance(rel_bias, d_last), 2, axis=0)
    b_far = jnp.broadcast_to(jnp.repeat(rel_bias[NUM_BUCKETS - 1].astype(F32), 2)[:, None], (2 * ATT_HEADS, prow))
    b_self = jnp.broadcast_to(jnp.repeat(rel_bias[0].astype(F32), 2)[:, None], (2 * ATT_HEADS, prow))
    btab = jnp.stack([jnp.where(same, b_last * LOG2E, NEG), jnp.where(same, b_far * LOG2E, NEG), b_self * LOG2E])

    def page_spec(j):
        return BS((None, None, prow, ATT_DV), lambda b, s, pt: (layer, pt[b, s * pp + j], 0, 0))

    const = lambda shp: BS(shp, lambda b, s, pt: tuple(0 for _ in shp))
    per_b = BS((None, 2 * ATT_HEADS, ATT_DV), lambda b, s, pt: (b, 0, 0))
    grid_spec = pltpu.PrefetchScalarGridSpec(
        num_scalar_prefetch=1, grid=(bsz, n_steps),
        in_specs=[BS(memory_space=pltpu.SMEM), per_b, per_b, per_b,
                  const((3, 2 * ATT_HEADS, prow)), const((1, ATT_DV))]
        + [page_spec(j) for j in range(pp)] + [page_spec(j) for j in range(pp)],
        out_specs=BS((None, 1, ATT_W), lambda b, s, pt: (b, 0, 0)),
        scratch_shapes=[pltpu.VMEM((2 * ATT_HEADS, 1), F32), pltpu.VMEM((2 * ATT_HEADS, 1), F32),
                        pltpu.VMEM((2 * ATT_HEADS, ATT_DV), F32)])
    out = pl.pallas_call(
        functools.partial(_decode_body, pp=pp, n_steps=n_steps, post_scale=post_scale),
        out_shape=S((bsz, 1, ATT_W), BF16), grid_spec=grid_spec,
        compiler_params=_cparams(("parallel", "arbitrary")), name="attn_decode",
    )(page_table, lam.reshape(1), q8, kn8, vn8, btab, subln.reshape(1, ATT_DV), *([ck] * pp), *([cv] * pp))
    return out.reshape(bsz, ATT_W)


def _causal_conv(x, prev8, cw_ref, c0, c1):
    acc = x * cw_ref[CONV_K - 1:CONV_K, c0:c1]
    row8 = lax.broadcasted_iota(jnp.int32, prev8.shape, 0)
    for s in range(1, CONV_K):
        r = pltpu.roll(x, s, axis=0)
        pr = pltpu.roll(prev8, s, axis=0)
        head = jnp.where(row8 < s, pr, r[:8])
        sh = jnp.concatenate([head, r[8:]], axis=0) if x.shape[0] > 8 else head
        acc = acc + sh * cw_ref[CONV_K - 1 - s:CONV_K - s, c0:c1]
    return acc


def _tri(cl):
    ii = lax.broadcasted_iota(jnp.int32, (cl, cl), 0)
    jj = lax.broadcasted_iota(jnp.int32, (cl, cl), 1)
    return ii, jj


def _ssd_body(z_ref, xbc_ref, ps_ref, cs0_ref, h0_ref, cw_ref, cb_ref, brow_ref, bcol_ref, arow_ref, acol_ref,
              dexp_ref, nw_ref, y_ref, hl_ref, st_ref, halo_ref, yacc_ref, *, cl, nc, valid_len):
    c = pl.program_id(1)

    @pl.when(c == 0)
    def _():
        st_ref[...] = h0_ref[...]
        halo_ref[...] = cs0_ref[...]

    states = [st_ref[h] for h in range(SSM_HEADS)]
    x_raw = xbc_ref[...]
    conv = _causal_conv(x_raw, halo_ref[...], cw_ref, 0, SSM_CONV_DIM)
    xbc = _silu(conv + cb_ref[...])
    x = xbc[:, :SSM_D_INNER]
    ps = ps_ref[...]
    step_c = _softplus(ps + brow_ref[...])
    step_r = _softplus(ps.T + bcol_ref[...])
    if valid_len is not None:
        t_c = lax.broadcasted_iota(jnp.int32, step_c.shape, 0) + c * cl
        t_r = lax.broadcasted_iota(jnp.int32, step_r.shape, 1) + c * cl
        step_c = jnp.where(t_c < valid_len, step_c, 0.0)
        step_r = jnp.where(t_r < valid_len, step_r, 0.0)
    la_c = (step_c * -jnp.exp(arow_ref[...]))[:, PS_DT:PS_DT + SSM_HEADS]
    la_r = (step_r * -jnp.exp(acol_ref[...]))[PS_DT:PS_DT + SSM_HEADS, :]
    ii, jj = _tri(cl)
    incl = ii >= jj
    acs_c = _dot_hi(incl.astype(F32), la_c)
    acs_r = _dot_hi(la_r, (ii <= jj).astype(F32))
    hpg = SSM_HEADS // SSM_GROUPS
    for g in range(SSM_GROUPS):
        b_g = xbc[:, SSM_D_INNER + g * SSM_STATE:SSM_D_INNER + (g + 1) * SSM_STATE].astype(BF16)
        c0 = SSM_D_INNER + SSM_GROUPS * SSM_STATE + g * SSM_STATE
        c_g = xbc[:, c0:c0 + SSM_STATE].astype(BF16)
        cb = _dot_nt(c_g, b_g)
        for r in range(hpg):
            h = g * hpg + r
            col = acs_c[:, h:h + 1]
            row = acs_r[h:h + 1, :]
            dec = jnp.where(incl, jnp.exp(jnp.where(incl, col - row, 0.0)), 0.0)
            xs = x[:, h * SSM_HEAD_DIM:(h + 1) * SSM_HEAD_DIM] * step_c[:, PS_DT + h:PS_DT + h + 1]
            y_diag = _dot((cb * dec).astype(BF16), xs.astype(BF16))
            hst = states[h]
            y_off = _dot_nt(c_g, hst.astype(BF16)) * jnp.exp(col)
            last = acs_c[cl - 1:cl, h:h + 1]
            to_end = jnp.exp(last - col)
            states[h] = hst * jnp.exp(last) + _dot_tn((xs * to_end).astype(BF16), b_g)
            yacc_ref[:, h * SSM_HEAD_DIM:(h + 1) * SSM_HEAD_DIM] = y_diag + y_off
    for h in range(SSM_HEADS):
        st_ref[h] = states[h]
    halo_ref[...] = x_raw[cl - 8:, :]
    y = (yacc_ref[...] + dexp_ref[...] * x) * _silu(z_ref[...])
    gw = SSM_D_INNER // SSM_GROUPS
    parts = []
    for g in range(SSM_GROUPS):
        yg = y[:, g * gw:(g + 1) * gw]
        parts.append(yg * lax.rsqrt(jnp.mean(yg * yg, axis=-1, keepdims=True) + EPS))
    y_ref[...] = (jnp.concatenate(parts, axis=1) * nw_ref[...]).astype(y_ref.dtype)

    @pl.when(c == nc - 1)
    def _():
        hl_ref[...] = st_ref[...]


def _ssd(proj, ps, conv0, h0, conv_w, conv_b, vec, ssm_d, ssm_norm, bsz, seq, cl, valid_len):
    nc = seq // cl
    n = bsz * seq
    brow, bcol, arow, acol = vec
    cs0 = jnp.pad(conv0, ((0, 0), (8 - (CONV_K - 1), 0), (0, 0)))
    dexp = jnp.repeat(ssm_d, SSM_HEAD_DIM).reshape(1, SSM_D_INNER)
    const = lambda shp: BS(shp, lambda b, c: tuple(0 for _ in shp))
    y, h_last = pl.pallas_call(
        functools.partial(_ssd_body, cl=cl, nc=nc, valid_len=valid_len),
        out_shape=(S((n, SSM_D_INNER), BF16), S(h0.shape, F32)), grid=(bsz, nc),
        in_specs=[BS((cl, SSM_D_INNER), lambda b, c: (b * nc + c, C_SZ // SSM_D_INNER)),
                  BS((cl, SSM_CONV_DIM), lambda b, c: (b * nc + c, C_XBC // SSM_CONV_DIM)),
                  BS((cl, PS_W), lambda b, c: (b * nc + c, 0)),
                  BS((None, 8, SSM_CONV_DIM), lambda b, c: (b, 0, 0)),
                  BS((None, SSM_HEADS, SSM_HEAD_DIM, SSM_STATE), lambda b, c: (b, 0, 0, 0)),
                  const((CONV_K, SSM_CONV_DIM)), const((1, SSM_CONV_DIM)),
                  const((1, PS_W)), const((PS_W, 1)), const((1, PS_W)), const((PS_W, 1)),
                  const((1, SSM_D_INNER)), const((1, SSM_D_INNER))],
        out_specs=(BS((cl, SSM_D_INNER), lambda b, c: (b * nc + c, 0)),
                   BS((None, SSM_HEADS, SSM_HEAD_DIM, SSM_STATE), lambda b, c: (b, 0, 0, 0))),
        scratch_shapes=[pltpu.VMEM((SSM_HEADS, SSM_HEAD_DIM, SSM_STATE), F32), pltpu.VMEM((8, SSM_CONV_DIM), F32),
                        pltpu.VMEM((cl, SSM_D_INNER), F32)],
        compiler_params=_cparams(("parallel", "arbitrary")), name="ssd",
    )(proj, proj, ps, cs0, h0, conv_w, conv_b.reshape(1, -1), brow, bcol, arow, acol, dexp,
      ssm_norm.reshape(1, SSM_D_INNER))
    return y, h_last


def _unit_lower_inverses(mats, ii, jj, cl):
    eye = (ii == jj).astype(F32)
    pair = ((ii >> 1) == (jj >> 1)) & (ii > jj)
    invs = [eye - jnp.where(pair, a, 0.0) for a in mats]
    s = 2
    while s < cl:
        blk = ((ii // (2 * s)) == (jj // (2 * s))) & ((ii % (2 * s)) >= s) & ((jj % (2 * s)) < s)
        xs = [_dot3(inv, jnp.where(blk, a, 0.0)) for inv, a in zip(invs, mats)]
        invs = [inv - _dot3(x, inv) for inv, x in zip(invs, xs)]
        s *= 2
    return invs


def _gdn_body(q_ref, k_ref, v_ref, z_ref, ps_ref, cs0_ref, s0_ref, cw_ref, brow_ref, bcol_ref, arow_ref, acol_ref,
              nw_ref, o_ref, sl_ref, st_ref, halo_ref, *, cl, cps, nc, valid_len):
    c = pl.program_id(1)

    @pl.when(c == 0)
    def _():
        st_ref[...] = s0_ref[...]
        halo_ref[...] = cs0_ref[...]

    w = DN_W
    rows = cl * cps
    halo = halo_ref[...]
    states = [st_ref[h] for h in range(DN_HEADS)]
    raws = (q_ref[...], k_ref[...], v_ref[...])
    q, k, v = (_silu(_causal_conv(raws[p], halo[:, p * w:(p + 1) * w], cw_ref, p * w, (p + 1) * w)) for p in range(3))
    z = z_ref[...]
    ps = ps_ref[...]
    g_c = -jnp.exp(arow_ref[...]) * _softplus(ps + brow_ref[...])
    g_r = -jnp.exp(acol_ref[...]) * _softplus(ps.T + bcol_ref[...])
    beta = _sigmoid(ps)
    if valid_len is not None:
        t_c = lax.broadcasted_iota(jnp.int32, g_c.shape, 0) + c * rows
        t_r = lax.broadcasted_iota(jnp.int32, g_r.shape, 1) + c * rows
        g_c = jnp.where(t_c < valid_len, g_c, 0.0)
        beta = jnp.where(t_c < valid_len, beta, 0.0)
        g_r = jnp.where(t_r < valid_len, g_r, 0.0)
    ii, jj = _tri(cl)
    incl = ii >= jj
    strict = ii > jj
    lower_ones = incl.astype(F32)
    upper_ones = (ii <= jj).astype(F32)
    off = PS_DECAY - 16
    nw = nw_ref[...]
    pre = []
    for ci in range(cps):
        rs = slice(ci * cl, (ci + 1) * cl)
        gcs_c = _dot_hi(lower_ones, g_c[rs, 16:24])
        gcs_r = _dot_hi(g_r[16:24, rs], upper_ones)
        for h in range(DN_HEADS):
            sl = slice(h * DN_DK, (h + 1) * DN_DK)
            qh = q[rs, sl]
            kh = k[rs, sl]
            qh = qh * lax.rsqrt(jnp.sum(qh * qh, axis=-1, keepdims=True) + EPS) * (DN_DK ** -0.5)
            kh = kh * lax.rsqrt(jnp.sum(kh * kh, axis=-1, keepdims=True) + EPS)
            bh = beta[rs, PS_BETA + h:PS_BETA + h + 1]
            col = gcs_c[:, off + h:off + h + 1]
            row = gcs_r[off + h:off + h + 1, :]
            last = gcs_c[cl - 1:cl, off + h:off + h + 1]
            e_col = jnp.exp(col)
            kb = kh * bh
            pre.append(dict(
                dec=jnp.where(incl, jnp.exp(jnp.where(incl, col - row, 0.0)), 0.0),
                q16=qh.astype(BF16), k16=kh.astype(BF16), kb16=kb.astype(BF16),
                vb=v[rs, sl] * bh, kbe=kb * e_col,
                qd16=(qh * e_col).astype(BF16),
                kd16=(kh * jnp.exp(last - col)).astype(BF16),
                g_end=jnp.exp(last),
                gate=nw * _silu(z[rs, sl])))
    lowers = [jnp.where(strict, _dot_nt(t["kb16"], t["k16"]) * t["dec"], 0.0) for t in pre]
    qks = [(_dot_nt(t["q16"], t["k16"]) * t["dec"]).astype(BF16) for t in pre]
    tinvs = _unit_lower_inverses(lowers, ii, jj, cl)
    us = [_dot3(ti, t["vb"]) for ti, t in zip(tinvs, pre)]
    ws = [_dot3(ti, t["kbe"]).astype(BF16) for ti, t in zip(tinvs, pre)]
    outs = []
    for ci in range(cps):
        items = range(ci * DN_HEADS, (ci + 1) * DN_HEADS)
        s16 = [st.astype(BF16) for st in states]
        v16 = [(us[i] - _dot(ws[i], s16[h])).astype(BF16) for h, i in enumerate(items)]
        os_ = [_dot(pre[i]["qd16"], s16[h]) + _dot(qks[i], v16[h]) for h, i in enumerate(items)]
        states = [states[h] * pre[i]["g_end"] + _dot_tn(pre[i]["kd16"], v16[h]) for h, i in enumerate(items)]
        outs += [o * lax.rsqrt(jnp.mean(o * o, axis=-1, keepdims=True) + EPS) * pre[i]["gate"]
                 for o, i in zip(os_, items)]
    for ci in range(cps):
        for h in range(DN_HEADS):
            o_ref[ci * cl:(ci + 1) * cl, h * DN_DV:(h + 1) * DN_DV] = outs[ci * DN_HEADS + h].astype(o_ref.dtype)
    for h in range(DN_HEADS):
        st_ref[h] = states[h]
    for p in range(3):
        halo_ref[:, p * w:(p + 1) * w] = raws[p][rows - 8:, :]

    @pl.when(c == nc - 1)
    def _():
        sl_ref[...] = st_ref[...]


def _gdn(proj, ps, conv0, s0, conv_w, vec, dn_norm, bsz, seq, cl, cps, valid_len):
    rows = cl * cps
    nc = seq // rows
    n = bsz * seq
    brow, bcol, arow, acol = vec
    cs0 = jnp.pad(conv0, ((0, 0), (8 - (CONV_K - 1), 0), (0, 0)))
    w = DN_W
    const = lambda shp: BS(shp, lambda b, c: tuple(0 for _ in shp))
    blk = lambda col: BS((rows, w), lambda b, c: (b * nc + c, col // w))
    o, s_last = pl.pallas_call(
        functools.partial(_gdn_body, cl=cl, cps=cps, nc=nc, valid_len=valid_len),
        out_shape=(S((n, w), BF16), S(s0.shape, F32)), grid=(bsz, nc),
        in_specs=[blk(C_DQ), blk(C_DK), blk(C_DV), blk(C_DZ),
                  BS((rows, PS_W), lambda b, c: (b * nc + c, 0)),
                  BS((None, 8, DN_CONV_DIM), lambda b, c: (b, 0, 0)),
                  BS((None, DN_HEADS, DN_DK, DN_DV), lambda b, c: (b, 0, 0, 0)),
                  const((CONV_K, DN_CONV_DIM)),
                  const((1, PS_W)), const((PS_W, 1)), const((1, PS_W)), const((PS_W, 1)), const((1, DN_DV))],
        out_specs=(BS((rows, w), lambda b, c: (b * nc + c, 0)),
                   BS((None, DN_HEADS, DN_DK, DN_DV), lambda b, c: (b, 0, 0, 0))),
        scratch_shapes=[pltpu.VMEM((DN_HEADS, DN_DK, DN_DV), F32), pltpu.VMEM((8, DN_CONV_DIM), F32)],
        compiler_params=_cparams(("parallel", "arbitrary")), name="gdn",
    )(proj, proj, proj, proj, ps, cs0, s0, conv_w, brow, bcol, arow, acol, dn_norm.reshape(1, DN_DV))
    return o, s_last


def _router_body(h_ref, lw_ref, rw_ref, rb_ref, meta_ref):
    x = h_ref[...]
    xn = x * lax.rsqrt(jnp.mean(x * x, axis=-1, keepdims=True) + EPS) * lw_ref[...]
    logits = _dot_hi(xn, rw_ref[...]) + rb_ref[...]
    lane = lax.broadcasted_iota(jnp.int32, logits.shape, 1)
    big = jnp.int32(logits.shape[1])
    m1 = jnp.max(logits, axis=1, keepdims=True)
    i1 = jnp.min(jnp.where(logits == m1, lane, big), axis=1, keepdims=True)
    rest = jnp.where(lane == i1, NEG, logits)
    m2 = jnp.max(rest, axis=1, keepdims=True)
    i2 = jnp.min(jnp.where(rest == m2, lane, big), axis=1, keepdims=True)
    e2 = jnp.exp(m2 - m1)
    g1 = 1.0 / (1.0 + e2)
    g2 = e2 * g1
    meta = jnp.where(lane == 0, i1.astype(F32), jnp.where(lane == 1, i2.astype(F32),
                     jnp.where(lane == 2, g1, jnp.where(lane == 3, g2, 0.0))))
    meta_ref[...] = meta


def _router(h, ln_w, router_w, router_b, tm):
    n, d = h.shape
    rw = jnp.pad(router_w, ((0, 0), (0, 128 - N_EXPERTS)))
    rb = jnp.pad(router_b, (0, 128 - N_EXPERTS), constant_values=NEG).reshape(1, 128)
    const = lambda shp: BS(shp, lambda i: (0, 0))
    return pl.pallas_call(
        _router_body, out_shape=S((n, 128), F32), grid=(n // tm,),
        in_specs=[BS((tm, d), lambda i: (i, 0)), const((1, d)), const((d, 128)), const((1, 128))],
        out_specs=BS((tm, 128), lambda i: (i, 0)),
        compiler_params=_cparams(("parallel",)), name="router")(h, ln_w.reshape(1, d), rw, rb)


def _experts_body(be_ref, tok_ref, nused_ref, h_hbm, lw_ref, rwt_ref, wg_ref, wu_ref, wd_ref, y_ref,
                  xg_ref, xb_ref, acc_ref, sem, *, tm, nf):
    i = pl.program_id(0)
    f = pl.program_id(1)
    used = i < nused_ref[0]

    @pl.when(used & (f == 0))
    def _():
        def row_copy(r):
            return pltpu.make_async_copy(h_hbm.at[pl.ds(tok_ref[i * tm + r], 1), :], xg_ref.at[pl.ds(r, 1), :], sem)

        def start(r, carry):
            row_copy(r).start()
            return carry

        def wait(r, carry):
            row_copy(r).wait()
            return carry

        lax.fori_loop(0, tm, start, 0)
        lax.fori_loop(0, tm, wait, 0)
        x = xg_ref[...]
        xn = x * lax.rsqrt(jnp.mean(x * x, axis=-1, keepdims=True) + EPS) * lw_ref[...]
        xb_ref[...] = xn.astype(BF16)
        acc_ref[...] = jnp.zeros(acc_ref.shape, F32)

    @pl.when(used)
    def _():
        xb = xb_ref[...]
        hid = _silu(_dot(xb, wg_ref[...])) * _dot(xb, wu_ref[...])
        acc_ref[...] += _dot(hid.astype(BF16), wd_ref[...])

    @pl.when(used & (f == nf - 1))
    def _():
        y_ref[...] = acc_ref[...] * rwt_ref[...]

    @pl.when(jnp.logical_not(used) & (f == nf - 1))
    def _():
        y_ref[...] = jnp.zeros(y_ref.shape, F32)


def _experts(h, ln_w, block_expert, row_token, n_used, row_weight, wg, wu, wd, tm, tf):
    n, d = h.shape
    rows = row_token.shape[0]
    n_blocks = rows // tm
    ff = wg.shape[2]
    nf = ff // tf
    grid_spec = pltpu.PrefetchScalarGridSpec(
        num_scalar_prefetch=3, grid=(n_blocks, nf),
        in_specs=[BS(memory_space=pl.ANY),
                  BS((1, d), lambda i, f, be, tok, nu: (0, 0)),
                  BS((tm, 1), lambda i, f, be, tok, nu: (i, 0)),
                  BS((None, d, tf), lambda i, f, be, tok, nu: (be[i], 0, f)),
                  BS((None, d, tf), lambda i, f, be, tok, nu: (be[i], 0, f)),
                  BS((None, tf, d), lambda i, f, be, tok, nu: (be[i], f, 0))],
        out_specs=BS((tm, d), lambda i, f, be, tok, nu: (i, 0)),
        scratch_shapes=[pltpu.VMEM((tm, d), F32), pltpu.VMEM((tm, d), BF16), pltpu.VMEM((tm, d), F32),
                        pltpu.SemaphoreType.DMA(())])
    return pl.pallas_call(
        functools.partial(_experts_body, tm=tm, nf=nf),
        out_shape=S((rows, d), F32), grid_spec=grid_spec,
        compiler_params=_cparams(("arbitrary", "arbitrary")), name="experts",
    )(block_expert, row_token, n_used, h, ln_w.reshape(1, d), row_weight.reshape(rows, 1), wg, wu, wd)


def _combine_body(pos_ref, y_hbm, h_ref, o_ref, ya_ref, yb_ref, sem, *, tc):
    i = pl.program_id(0)

    def copies(r):
        t = i * tc + r
        return (pltpu.make_async_copy(y_hbm.at[pl.ds(pos_ref[2 * t], 1), :], ya_ref.at[pl.ds(r, 1), :], sem),
                pltpu.make_async_copy(y_hbm.at[pl.ds(pos_ref[2 * t + 1], 1), :], yb_ref.at[pl.ds(r, 1), :], sem))

    def start(r, carry):
        a, b = copies(r)
        a.start()
        b.start()
        return carry

    def wait(r, carry):
        a, b = copies(r)
        a.wait()
        b.wait()
        return carry

    lax.fori_loop(0, tc, start, 0)
    lax.fori_loop(0, tc, wait, 0)
    o_ref[...] = h_ref[...] + (ya_ref[...] + yb_ref[...])


def _combine(h, y_sorted, pos, tc):
    n, d = h.shape
    grid_spec = pltpu.PrefetchScalarGridSpec(
        num_scalar_prefetch=1, grid=(n // tc,),
        in_specs=[BS(memory_space=pl.ANY), BS((tc, d), lambda i, pos: (i, 0))],
        out_specs=BS((tc, d), lambda i, pos: (i, 0)),
        scratch_shapes=[pltpu.VMEM((tc, d), F32), pltpu.VMEM((tc, d), F32), pltpu.SemaphoreType.DMA(())])
    return pl.pallas_call(
        functools.partial(_combine_body, tc=tc), out_shape=S((n, d), F32), grid_spec=grid_spec,
        compiler_params=_cparams(("arbitrary",)), name="moe_combine")(pos, y_sorted, h)


def _moe(h, ln_w, router_w, router_b, wg, wu, wd, tm_r, tm, tf, tc):
    n, d = h.shape
    meta = _router(h, ln_w, router_w, router_b, tm_r)
    idx = meta[:, 0:2].astype(jnp.int32)
    gate = meta[:, 2:4]
    member = jnp.sum(jax.nn.one_hot(idx, N_EXPERTS, dtype=jnp.int32), axis=1)
    before = jnp.cumsum(member, axis=0) - member
    counts = jnp.sum(member, axis=0)
    padded = (counts + tm - 1) // tm * tm
    pad_ends = jnp.cumsum(padded)
    pad_starts = pad_ends - padded
    pos = (pad_starts[idx] + jnp.take_along_axis(before, idx, axis=1)).astype(jnp.int32)
    n_blocks = -(-(2 * n + N_EXPERTS * (tm - 1)) // tm)
    rows = n_blocks * tm
    tok = jnp.broadcast_to(jnp.arange(n, dtype=jnp.int32)[:, None], (n, 2))
    row_token = jnp.zeros((rows,), jnp.int32).at[pos.reshape(-1)].set(tok.reshape(-1))
    row_weight = jnp.zeros((rows,), F32).at[pos.reshape(-1)].set(gate.reshape(-1))
    n_used = (pad_ends[-1] // tm).astype(jnp.int32).reshape(1)
    blk_start = jnp.minimum(jnp.arange(n_blocks, dtype=jnp.int32), n_used[0] - 1) * tm
    block_expert = jnp.minimum(jnp.searchsorted(pad_ends, blk_start, side="right"), N_EXPERTS - 1).astype(jnp.int32)
    y_sorted = _experts(h, ln_w, block_expert, row_token, n_used, row_weight, wg, wu, wd, tm, tf)
    return _combine(h, y_sorted, pos.reshape(-1), tc)


def _layer(i, h, p, lw, cfg, att_fn, ssm_conv0, ssm_h0, dn_conv0, dn_s0):
    bsz, seq, tm, cl_ssd, cl_gdn, pad_to, valid_len = (cfg[k] for k in
                                                      ("bsz", "seq", "tm", "cl_ssd", "cl_gdn", "pad_to", "valid_len"))
    n = bsz * seq
    xn = _rmsnorm(h, lw["ln_mix"], tm)
    ident = lambda d, e: d[0]
    proj = _fused_matmul([xn], [(0, lw["w_main"])], ident, [], F32, tm, 512, "in_proj")
    ps = _fused_matmul([xn], [(0, lw["w_small"])], ident, [], F32, tm, PS_W, "in_proj_small")

    q16, k32, k16, v16 = _qkv_prep(proj, lw["q_norm"], lw["k_norm"], tm)
    att = att_fn(q16, k16, v16)

    if pad_to is None:
        proj_r, ps_r, seq_r = proj, ps, seq
    else:
        seq_r = pad_to
        proj_r = jnp.pad(proj.reshape(bsz, seq, -1), ((0, 0), (0, pad_to - seq), (0, 0))).reshape(bsz * pad_to, -1)
        ps_r = jnp.pad(ps.reshape(bsz, seq, -1), ((0, 0), (0, pad_to - seq), (0, 0))).reshape(bsz * pad_to, -1)
    y, ssm_h1 = _ssd(proj_r, ps_r, ssm_conv0, ssm_h0, lw["ssm_conv_w"], lw["ssm_conv_b"], lw["vec"], lw["ssm_d"],
                     lw["ssm_norm"], bsz, seq_r, cl_ssd, valid_len)
    o, dn_s1 = _gdn(proj_r, ps_r, dn_conv0, dn_s0, lw["dn_conv_w"], lw["vec"], lw["dn_norm"], bsz, seq_r, cl_gdn,
                    cfg["cps_gdn"], valid_len)
    if pad_to is not None:
        y = y.reshape(bsz, pad_to, -1)[:, :seq].reshape(n, -1)
        o = o.reshape(bsz, pad_to, -1)[:, :seq].reshape(n, -1)

    def merge(d, e):
        return _sigmoid(d[0]) * d[3] + _sigmoid(d[1]) * d[4] + _sigmoid(d[2]) * d[5]

    merged = _fused_matmul([xn, att, y, o],
                           [(0, lw["w_gate_a"]), (0, lw["w_gate_b"]), (0, lw["w_gate_c"]),
                            (1, lw["w_up_att"]), (2, lw["w_up_ssm"]), (3, lw["w_up_dn"])],
                           merge, [], BF16, tm, 512, "merge")
    resid = lambda d, e: e[0] + d[0]
    h = _fused_matmul([merged], [(0, lw["w_out"])], resid, [h], F32, tm, 512, "out_proj")

    if i % 2 == 0:
        hn = _rmsnorm(h, lw["ln_ffn"], tm)
        ff = _fused_matmul([hn], [(0, lw["ffn_w_gate"]), (0, lw["ffn_w_up"])], lambda d, e: _silu(d[0]) * d[1], [],
                           BF16, min(tm, 512), 1408, "ffn_up")
        h = _fused_matmul([ff], [(0, lw["ffn_w_down"])], resid, [h], F32, min(tm, 512), 512, "ffn_down")
    else:
        h = _moe(h, lw["ln_ffn"], lw["router_w"], lw["router_b"], lw["exp_w_gate"], lw["exp_w_up"], lw["exp_w_down"],
                 min(tm, 512), cfg["tm_moe"], 512, cfg["tc"])

    hn = _rmsnorm(h, lw["ln_ple"], tm)
    h = _fused_matmul([hn, p.astype(BF16)], [(0, lw["ple_w_gate"]), (1, lw["ple_w_proj"])],
                      lambda d, e: e[0] + _sigmoid(d[0]) * d[1], [h], F32, tm, 512, "ple")

    xbc_raw = proj[:, C_XBC:C_XBC + SSM_CONV_DIM].reshape(bsz, seq, -1)
    dn_raw = proj[:, C_DQ:C_DQ + DN_CONV_DIM].reshape(bsz, seq, -1)
    keep = CONV_K - 1
    if seq >= keep:
        ssm_conv1, dn_conv1 = xbc_raw[:, seq - keep:], dn_raw[:, seq - keep:]
    else:
        ssm_conv1 = jnp.concatenate([ssm_conv0, xbc_raw], axis=1)[:, -keep:]
        dn_conv1 = jnp.concatenate([dn_conv0, dn_raw], axis=1)[:, -keep:]
    new_k = k32.reshape(bsz, seq, ATT_HEADS, 2 * ATT_DK)
    new_v = proj[:, C_AV:C_AV + ATT_W].reshape(bsz, seq, ATT_HEADS, ATT_DV)
    return h, (new_k, new_v, ssm_conv1, ssm_h1, dn_conv1, dn_s1)


def _prep_layer_weights(i, W):
    w_in = W["w_in"][i]
    o = 0
    cols = {}
    for name, width in (("aq", 512), ("ak", 512), ("av", 512), ("sz", 1024), ("xbc", 1536), ("dt", 16),
                        ("dq", 512), ("dk", 512), ("dv", 512), ("dz", 512), ("beta", 4), ("decay", 4)):
        cols[name] = w_in[:, o:o + width]
        o += width
    w_main = jnp.concatenate([cols[k] for k in ("xbc", "aq", "sz", "ak", "av", "dq", "dk", "dv", "dz")], axis=1)
    w_small = jnp.concatenate([cols["dt"], cols["beta"], cols["decay"],
                               jnp.zeros((D_MODEL, PS_W - 24), w_in.dtype)], axis=1)
    bias_row = jnp.zeros((PS_W,), F32).at[PS_DT:PS_DT + SSM_HEADS].set(W["ssm_dt_bias"][i])
    bias_row = bias_row.at[PS_DECAY:PS_DECAY + DN_HEADS].set(W["dn_dt_bias"][i])
    alog_row = jnp.zeros((PS_W,), F32).at[PS_DT:PS_DT + SSM_HEADS].set(W["ssm_a_log"][i])
    alog_row = alog_row.at[PS_DECAY:PS_DECAY + DN_HEADS].set(W["dn_a_log"][i])
    vec = (bias_row.reshape(1, PS_W), bias_row.reshape(PS_W, 1), alog_row.reshape(1, PS_W), alog_row.reshape(PS_W, 1))
    wg = W["w_gate"][i]
    lw = {
        "ln_mix": W["ln_mix"][i], "w_main": w_main.astype(BF16), "w_small": w_small.astype(BF16),
        "q_norm": W["q_norm"][i], "k_norm": W["k_norm"][i], "att_subln": W["att_subln"][i],
        "ssm_conv_w": W["ssm_conv_w"][i], "ssm_conv_b": W["ssm_conv_b"][i], "vec": vec,
        "ssm_d": W["ssm_d"][i], "ssm_norm": W["ssm_norm"][i],
        "dn_conv_w": W["dn_conv_w"][i], "dn_norm": W["dn_norm"][i],
        "w_gate_a": wg[:, :D_MODEL].astype(BF16), "w_gate_b": wg[:, D_MODEL:2 * D_MODEL].astype(BF16),
        "w_gate_c": wg[:, 2 * D_MODEL:].astype(BF16),
        "w_up_att": W["w_up_att"][i].astype(BF16), "w_up_ssm": W["w_up_ssm"][i].astype(BF16),
        "w_up_dn": W["w_up_dn"][i].astype(BF16), "w_out": W["w_out"][i].astype(BF16),
        "ln_ffn": W["ln_ffn"][i], "ln_ple": W["ln_ple"][i],
        "ple_w_gate": W["ple_w_gate"][i].astype(BF16), "ple_w_proj": W["ple_w_proj"][i].astype(BF16),
    }
    if i % 2 == 0:
        lw.update(ffn_w_gate=W["ffn_w_gate"][i // 2].astype(BF16), ffn_w_up=W["ffn_w_up"][i // 2].astype(BF16),
                  ffn_w_down=W["ffn_w_down"][i // 2].astype(BF16))
    else:
        lw.update(router_w=W["router_w"][i // 2], router_b=W["router_b"][i // 2],
                  exp_w_gate=W["exp_w_gate"][i // 2].astype(BF16), exp_w_up=W["exp_w_up"][i // 2].astype(BF16),
                  exp_w_down=W["exp_w_down"][i // 2].astype(BF16))
    lam_init = 0.8 - 0.6 * math.exp(-0.3 * i)
    lam = (jnp.exp(jnp.sum(W["lam_q1"][i] * W["lam_k1"][i])) - jnp.exp(jnp.sum(W["lam_q2"][i] * W["lam_k2"][i]))
           + lam_init).astype(F32)
    return lw, lam, lam_init


def _pick(n, pref):
    t = min(n, pref)
    while n % t:
        t //= 2
    return t


def kernel(x_prompt, x_sample, cache_k, cache_v, state_ssm_conv, state_ssm, state_dn_conv, state_dn, page_table, p_prompt, p_sample, ln_mix, w_in, w_gate, q_norm, k_norm, lam_q1, lam_k1, lam_q2, lam_k2, att_subln, rel_bias, ssm_conv_w, ssm_conv_b, ssm_dt_bias, ssm_a_log, ssm_d, ssm_norm, dn_conv_w, dn_dt_bias, dn_a_log, dn_norm, w_up_att, w_up_ssm, w_up_dn, w_out, ln_ffn, ffn_w_gate, ffn_w_up, ffn_w_down, router_w, router_b, exp_w_gate, exp_w_up, exp_w_down, ln_ple, ple_w_gate, ple_w_proj):
    W = dict(ln_mix=ln_mix, w_in=w_in, w_gate=w_gate, q_norm=q_norm, k_norm=k_norm, lam_q1=lam_q1, lam_k1=lam_k1,
             lam_q2=lam_q2, lam_k2=lam_k2, att_subln=att_subln, ssm_conv_w=ssm_conv_w, ssm_conv_b=ssm_conv_b,
             ssm_dt_bias=ssm_dt_bias, ssm_a_log=ssm_a_log, ssm_d=ssm_d, ssm_norm=ssm_norm, dn_conv_w=dn_conv_w,
             dn_dt_bias=dn_dt_bias, dn_a_log=dn_a_log, dn_norm=dn_norm, w_up_att=w_up_att, w_up_ssm=w_up_ssm,
             w_up_dn=w_up_dn, w_out=w_out, ln_ffn=ln_ffn, ffn_w_gate=ffn_w_gate, ffn_w_up=ffn_w_up,
             ffn_w_down=ffn_w_down, router_w=router_w, router_b=router_b, exp_w_gate=exp_w_gate, exp_w_up=exp_w_up,
             exp_w_down=exp_w_down, ln_ple=ln_ple, ple_w_gate=ple_w_gate, ple_w_proj=ple_w_proj)
    depth = ln_mix.shape[0]
    bp, lp, d = x_prompt.shape
    bs, ls, _ = x_sample.shape
    n_pages = page_table.shape[1]
    t_att = _pick(lp, T_ATT)
    cfg_p = dict(bsz=bp, seq=lp, tm=_pick(bp * lp, TM), cl_ssd=_pick(lp, CL_SSD), cl_gdn=_pick(lp, CL_GDN), pad_to=None,
                 valid_len=None, tm_moe=_pick(bp * lp, TM_MOE), tc=_pick(bp * lp, TC_MOE))
    cfg_p["cps_gdn"] = _pick(lp // cfg_p["cl_gdn"], CPS_GDN)
    cfg_s = dict(bsz=bs, seq=ls, tm=bs * ls, cl_ssd=64, cl_gdn=64, cps_gdn=1, pad_to=64, valid_len=ls,
                 tm_moe=64, tc=bs * ls)
    hp = x_prompt.reshape(bp * lp, d)
    hs = x_sample.reshape(bs * ls, d)
    st_p, st_s = [], []
    tabs = _bias_tables(rel_bias, t_att)
    for i in range(depth):
        lw, lam, lam_init = _prep_layer_weights(i, W)
        post = 1.0 - lam_init
        att_p = lambda q, k, v: _attn_prompt(q, k, v, lam, rel_bias, tabs, lw["att_subln"], post, bp, lp, t_att)
        hp, st = _layer(i, hp, p_prompt[i].reshape(bp * lp, -1), lw, cfg_p, att_p,
                        jnp.zeros((bp, CONV_K - 1, SSM_CONV_DIM), F32),
                        jnp.zeros((bp, SSM_HEADS, SSM_HEAD_DIM, SSM_STATE), F32),
                        jnp.zeros((bp, CONV_K - 1, DN_CONV_DIM), F32),
                        jnp.zeros((bp, DN_HEADS, DN_DK, DN_DV), F32))
        st_p.append(st)
        att_s = lambda q, k, v: _attn_decode(i, q, k, v, cache_k, cache_v, page_table, lam, rel_bias,
                                             lw["att_subln"], post, _pick(n_pages, PAGES_PER_STEP))
        hs, st = _layer(i, hs, p_sample[i].reshape(bs * ls, -1), lw, cfg_s, att_s,
                        state_ssm_conv[i], state_ssm[i], state_dn_conv[i], state_dn[i])
        st_s.append(st)
    stk = lambda sts, j: jnp.stack([s[j] for s in sts])
    return (hp.reshape(bp, lp, d), hs.reshape(bs, ls, d),
            stk(st_p, 0), stk(st_p, 1), stk(st_p, 2), stk(st_p, 3), stk(st_p, 4), stk(st_p, 5),
            stk(st_s, 0), stk(st_s, 1), stk(st_s, 2), stk(st_s, 3), stk(st_s, 4), stk(st_s, 5))
```

```python
import functools
import math

import jax
import jax.numpy as jnp
from jax import lax
from jax.experimental import pallas as pl
from jax.experimental.pallas import tpu as pltpu

F32 = jnp.float32
BF16 = jnp.bfloat16
S = jax.ShapeDtypeStruct
BS = pl.BlockSpec

D_MODEL = 1024
ATT_HEADS = 4
ATT_DK = 64
ATT_DV = 128
ATT_W = ATT_HEADS * ATT_DV
NUM_BUCKETS = 32
MAX_DISTANCE = 128
SSM_HEADS = 16
SSM_HEAD_DIM = 64
SSM_GROUPS = 2
SSM_STATE = 128
SSM_D_INNER = 1024
SSM_CONV_DIM = 1536
DN_HEADS = 4
DN_DK = 128
DN_DV = 128
DN_W = 512
DN_CONV_DIM = 1536
CONV_K = 4
N_EXPERTS = 8
EPS = 1e-6
NEG = -1e30
LOG2E = 1.4426950408889634

C_XBC, C_AQ, C_SZ, C_AK, C_AV, C_DQ, C_DK, C_DV, C_DZ = 0, 1536, 2048, 3072, 3584, 4096, 4608, 5120, 5632
PROJ_W = 6144
PS_DT, PS_BETA, PS_DECAY, PS_W = 0, 16, 20, 128

VMEM_LIMIT = 56 * 1024 * 1024
T_ATT = 512
TM = 1024
CL_SSD = 128
CL_GDN = 64
CPS_GDN = 2
PAGES_PER_STEP = 8
TM_MOE = 512
TF_MOE = 896
TC_MOE = 256


def _cparams(sem):
    return pltpu.CompilerParams(dimension_semantics=sem, vmem_limit_bytes=VMEM_LIMIT)


def _dot(a, b):
    return jnp.dot(a, b, preferred_element_type=F32)


def _dot_nt(a, b):
    return lax.dot_general(a, b, (((1,), (1,)), ((), ())), preferred_element_type=F32)


def _dot_tn(a, b):
    return lax.dot_general(a, b, (((0,), (0,)), ((), ())), preferred_element_type=F32)


def _dot_hi(a, b):
    return jnp.dot(a, b, preferred_element_type=F32, precision=lax.Precision.HIGHEST)


def _split(a):
    hi = a.astype(BF16)
    lo = (a - hi.astype(F32)).astype(BF16)
    return hi, lo


def _dot3(a, b):
    ah, al = _split(a)
    bh, bl = _split(b)
    return _dot(ah, bh) + (_dot(ah, bl) + _dot(al, bh))


def _sigmoid(x):
    return 1.0 / (1.0 + jnp.exp(-x))


def _silu(x):
    return x * _sigmoid(x)


def _softplus(x):
    return jnp.maximum(x, 0.0) + jnp.log1p(jnp.exp(-jnp.abs(x)))


def _rmsnorm_body(x_ref, w_ref, o_ref):
    x = x_ref[...]
    y = x * lax.rsqrt(jnp.mean(x * x, axis=-1, keepdims=True) + EPS) * w_ref[...]
    o_ref[...] = y.astype(o_ref.dtype)


def _rmsnorm(x, w, tm):
    n, d = x.shape
    return pl.pallas_call(
        _rmsnorm_body, out_shape=S((n, d), BF16), grid=(n // tm,),
        in_specs=[BS((tm, d), lambda i: (i, 0)), BS((1, d), lambda i: (0, 0))],
        out_specs=BS((tm, d), lambda i: (i, 0)),
        compiler_params=_cparams(("parallel",)), name="rmsnorm")(x, w.reshape(1, d))


def _fused_matmul(acts, pairs, combine, extras, out_dtype, tm, tn, name):
    n = acts[0].shape[0]
    m = pairs[0][1].shape[1]
    na, npair, nex = len(acts), len(pairs), len(extras)
    act_idx = [a for a, _ in pairs]

    def body(*refs):
        a_vals = [r[...] for r in refs[:na]]
        dots = [_dot(a_vals[act_idx[j]], refs[na + j][...]) for j in range(npair)]
        ex = [r[...] for r in refs[na + npair:na + npair + nex]]
        o_ref = refs[na + npair + nex]
        o_ref[...] = combine(dots, ex).astype(o_ref.dtype)

    in_specs = ([BS((tm, a.shape[1]), lambda i, j: (i, 0)) for a in acts]
                + [BS((w.shape[0], tn), lambda i, j: (0, j)) for _, w in pairs]
                + [BS((tm, tn), lambda i, j: (i, j)) for _ in extras])
    return pl.pallas_call(
        body, out_shape=S((n, m), out_dtype), grid=(n // tm, m // tn),
        in_specs=in_specs, out_specs=BS((tm, tn), lambda i, j: (i, j)),
        compiler_params=_cparams(("parallel", "arbitrary")), name=name,
    )(*acts, *[w for _, w in pairs], *extras)


def _qkv_body(aq_ref, ak_ref, av_ref, g_ref, qw_ref, kw_ref, q_ref, kf_ref, kb_ref, vb_ref):
    g = g_ref[...]

    def gnorm(x, w):
        hi, lo = _split(x * x)
        ss = _dot(hi, g) + _dot(lo, g)
        return x * lax.rsqrt(ss * (1.0 / ATT_DK) + EPS) * w

    q = gnorm(aq_ref[...], qw_ref[...]) * (ATT_DK ** -0.5 * LOG2E)
    k = gnorm(ak_ref[...], kw_ref[...])
    q_ref[...] = q.astype(BF16)
    kf_ref[...] = k
    kb_ref[...] = k.astype(BF16)
    vb_ref[...] = av_ref[...].astype(BF16)


def _qkv_prep(proj, q_norm, k_norm, tm):
    n = proj.shape[0]
    w = ATT_W
    gi = jnp.arange(w) // ATT_DK
    gmat = (gi[:, None] == gi[None, :]).astype(BF16)
    qw = jnp.tile(q_norm, w // ATT_DK).reshape(1, w)
    kw = jnp.tile(k_norm, w // ATT_DK).reshape(1, w)
    row = lambda c: BS((tm, w), lambda i: (i, c))
    const = lambda shp: BS(shp, lambda i: (0, 0))
    return pl.pallas_call(
        _qkv_body,
        out_shape=(S((n, w), BF16), S((n, w), F32), S((n, w), BF16), S((n, w), BF16)),
        grid=(n // tm,),
        in_specs=[row(C_AQ // w), row(C_AK // w), row(C_AV // w), const((w, w)), const((1, w)), const((1, w))],
        out_specs=(row(0), row(0), row(0), row(0)),
        compiler_params=_cparams(("parallel",)), name="qkv_prep")(proj, proj, proj, gmat, qw, kw)


def _t5_bucket(n):
    max_exact = NUM_BUCKETS // 2
    scaled = jnp.log(jnp.maximum(n, 1).astype(F32) / max_exact) / math.log(MAX_DISTANCE / max_exact)
    large = jnp.minimum(max_exact + (scaled * (NUM_BUCKETS - max_exact)).astype(jnp.int32), NUM_BUCKETS - 1)
    return jnp.where(n < max_exact, n, large)


def _bias_by_distance(rel_bias, n):
    return jnp.moveaxis(rel_bias[_t5_bucket(n)], -1, 0).astype(F32)


def _bias_table_body(rb_ref, o_ref, *, t, rows):
    h = pl.program_id(0)
    max_exact = NUM_BUCKETS // 2

    def block(i, carry):
        r0 = pl.multiple_of(i * rows, rows)
        c = lax.broadcasted_iota(jnp.int32, (rows, t), 0) + r0
        r = lax.broadcasted_iota(jnp.int32, (rows, t), 1)
        for idx in range(2):
            n = r - c + idx * t
            nn = jnp.maximum(n, 0)
            scaled = jnp.log(jnp.maximum(nn, 1).astype(F32) / max_exact) / math.log(MAX_DISTANCE / max_exact)
            large = jnp.minimum(max_exact + (scaled * (NUM_BUCKETS - max_exact)).astype(jnp.int32), NUM_BUCKETS - 1)
            bucket = jnp.where(nn < max_exact, nn, large)
            val = jnp.zeros((rows, t), F32)
            for b in range(NUM_BUCKETS):
                val = jnp.where(bucket == b, rb_ref[b, h] * LOG2E, val)
            if idx == 0:
                val = jnp.where(n >= 0, val, NEG)
            o_ref[idx, pl.ds(r0, rows), :] = val
        return carry

    lax.fori_loop(0, t // rows, block, 0)


def _bias_tables(rel_bias, t):
    return pl.pallas_call(
        functools.partial(_bias_table_body, t=t, rows=8),
        out_shape=S((ATT_HEADS, 2, t, t), F32), grid=(ATT_HEADS,),
        in_specs=[BS(memory_space=pltpu.SMEM)],
        out_specs=BS((None, 2, t, t), lambda h: (h, 0, 0, 0)),
        compiler_params=_cparams(("parallel",)), name="bias_tables")(rel_bias.astype(F32))


def _attn_body(lam_ref, cfar_ref, q_ref, k_ref, vt_ref, tab_ref, sw_ref, o_ref,
               q2_ref, s_ref, sb_ref, p_ref, m_ref, l_ref, acc_ref, *, t, rb, post_scale):
    h = pl.program_id(1)
    qi = pl.program_id(2)
    q = q_ref[...]
    lane = lax.broadcasted_iota(jnp.int32, q.shape, 1)
    zero = jnp.zeros_like(q)
    q2_ref[0:t, :] = jnp.where(lane < ATT_DK, q, zero)
    q2_ref[t:2 * t, :] = jnp.where(lane >= ATT_DK, q, zero)
    m_ref[...] = jnp.full(m_ref.shape, -jnp.inf, F32)
    l_ref[...] = jnp.zeros(l_ref.shape, F32)
    acc_ref[...] = jnp.zeros(acc_ref.shape, F32)
    cfar = cfar_ref[h] * LOG2E
    w2 = 2 * t

    def tile(ki, tab_idx):
        s_ref[...] = _dot_nt(k_ref[pl.ds(pl.multiple_of(ki * t, t), t), :], q2_ref[...])
        src_ref = s_ref if tab_idx is None else sb_ref

        def key_max(g, mx):
            rows = pl.ds(pl.multiple_of(g * rb, rb), rb)
            s = s_ref[rows, :]
            if tab_idx is not None:
                tab = tab_ref[tab_idx, rows, :]
                s = s + jnp.concatenate([tab, tab], axis=1)
                sb_ref[rows, :] = s
            for a in range(0, rb, 8):
                mx = jnp.maximum(mx, s[a:a + 8, :])
            return mx

        mx8 = lax.fori_loop(0, t // rb, key_max, jnp.full((8, w2), -jnp.inf, F32), unroll=4)
        tile_max = jnp.max(mx8, axis=0, keepdims=True)
        if tab_idx is None:
            tile_max = tile_max + cfar
        m_old = m_ref[...]
        m_new = jnp.maximum(m_old, tile_max)
        alpha = jnp.exp2(m_old - m_new)
        m_ref[...] = m_new
        shift = jnp.broadcast_to(m_new if tab_idx is not None else m_new - cfar, (rb, w2))

        def key_exp(g, tot):
            rows = pl.ds(pl.multiple_of(g * rb, rb), rb)
            p = jnp.exp2(src_ref[rows, :] - shift)
            p_ref[rows, :] = p.astype(BF16)
            for a in range(0, rb, 8):
                tot = tot + p[a:a + 8, :]
            return tot

        tot8 = lax.fori_loop(0, t // rb, key_exp, jnp.zeros((8, w2), F32), unroll=4)
        l_ref[...] = alpha * l_ref[...] + jnp.sum(tot8, axis=0, keepdims=True)
        acc_ref[...] = alpha * acc_ref[...] + _dot(vt_ref[ki], p_ref[...])

    def far(ki, carry):
        tile(ki, None)
        return carry

    lax.fori_loop(0, jnp.maximum(qi - 1, 0), far, 0)

    @pl.when(qi >= 1)
    def _():
        tile(qi - 1, 1)

    tile(qi, 0)
    ot = acc_ref[...] * (1.0 / l_ref[...])
    o = (ot[:, 0:t] - lam_ref[0] * ot[:, t:w2]).T
    o = o * lax.rsqrt(jnp.mean(o * o, axis=-1, keepdims=True) + EPS) * sw_ref[...] * post_scale
    o_ref[...] = o.astype(o_ref.dtype)


def _attn_prompt(q, k, v, lam, rel_bias, tabs, subln, post_scale, bsz, seq, t):
    nq = seq // t
    cfar = rel_bias[NUM_BUCKETS - 1].astype(F32)
    vt = v.reshape(bsz, nq, t, ATT_HEADS, ATT_DV).transpose(0, 3, 1, 4, 2).reshape(bsz * ATT_HEADS, nq, ATT_DV, t)
    smem = BS(memory_space=pltpu.SMEM)
    return pl.pallas_call(
        functools.partial(_attn_body, t=t, rb=16, post_scale=post_scale),
        out_shape=S((bsz * seq, ATT_W), BF16), grid=(bsz, ATT_HEADS, nq),
        in_specs=[smem, smem,
                  BS((t, ATT_DV), lambda b, h, qi: (b * nq + qi, h)),
                  BS((seq, ATT_DV), lambda b, h, qi: (b, h)),
                  BS((None, nq, ATT_DV, t), lambda b, h, qi: (b * ATT_HEADS + h, 0, 0, 0)),
                  BS((None, 2, t, t), lambda b, h, qi: (h, 0, 0, 0)),
                  BS((1, ATT_DV), lambda b, h, qi: (0, 0))],
        out_specs=BS((t, ATT_DV), lambda b, h, qi: (b * nq + qi, h)),
        scratch_shapes=[pltpu.VMEM((2 * t, ATT_DV), BF16), pltpu.VMEM((t, 2 * t), F32), pltpu.VMEM((t, 2 * t), F32),
                        pltpu.VMEM((t, 2 * t), BF16), pltpu.VMEM((1, 2 * t), F32), pltpu.VMEM((1, 2 * t), F32),
                        pltpu.VMEM((ATT_DV, 2 * t), F32)],
        compiler_params=_cparams(("parallel", "parallel", "arbitrary")), name="attn_prompt",
    )(lam.reshape(1), cfar, q, k, vt, tabs, subln.reshape(1, ATT_DV))


def _decode_body(pt_ref, lam_ref, q8_ref, kn_ref, vn_ref, btab_ref, sw_ref, *rest, pp, n_steps, post_scale):
    k_refs = rest[:pp]
    v_refs = rest[pp:2 * pp]
    o_ref = rest[2 * pp]
    m_ref, l_ref, acc_ref = rest[2 * pp + 1:]
    s_idx = pl.program_id(1)

    @pl.when(s_idx == 0)
    def _():
        m_ref[...] = jnp.full(m_ref.shape, -jnp.inf, F32)
        l_ref[...] = jnp.zeros(l_ref.shape, F32)
        acc_ref[...] = jnp.zeros(acc_ref.shape, F32)

    q8 = q8_ref[...]
    n_pages = n_steps * pp
    scores = []
    for j in range(pp):
        page = s_idx * pp + j
        bias = jnp.where(page == n_pages - 1, btab_ref[0], btab_ref[1])
        scores.append(_dot_nt(q8, k_refs[j][...].astype(BF16)) + bias)
    s_all = jnp.concatenate(scores, axis=1)
    m_old = m_ref[...]
    m_new = jnp.maximum(m_old, jnp.max(s_all, axis=1, keepdims=True))
    alpha = jnp.exp2(m_old - m_new)
    p_all = jnp.exp2(s_all - m_new)
    l_ref[...] = alpha * l_ref[...] + jnp.sum(p_all, axis=1, keepdims=True)
    ps = p_all.shape[1] // pp
    p16 = p_all.astype(BF16)
    pv = _dot(p16[:, :ps], v_refs[0][...].astype(BF16))
    for j in range(1, pp):
        pv = pv + _dot(p16[:, j * ps:(j + 1) * ps], v_refs[j][...].astype(BF16))
    acc_ref[...] = alpha * acc_ref[...] + pv
    m_ref[...] = m_new

    @pl.when(s_idx == n_steps - 1)
    def _():
        s_self = (jnp.sum(q8.astype(F32) * kn_ref[...].astype(F32), axis=1, keepdims=True)
                  + btab_ref[2][:, 0:1])
        m_o = m_ref[...]
        m_f = jnp.maximum(m_o, s_self)
        a_o = jnp.exp2(m_o - m_f)
        p_s = jnp.exp2(s_self - m_f)
        l_f = a_o * l_ref[...] + p_s
        acc = a_o * acc_ref[...] + p_s * vn_ref[...].astype(F32)
        o8 = acc * (1.0 / l_f)
        lam = lam_ref[0]
        outs = []
        for h in range(ATT_HEADS):
            o = o8[2 * h:2 * h + 1, :] - lam * o8[2 * h + 1:2 * h + 2, :]
            o = o * lax.rsqrt(jnp.mean(o * o, axis=-1, keepdims=True) + EPS) * sw_ref[...] * post_scale
            outs.append(o)
        o_ref[...] = jnp.concatenate(outs, axis=1).astype(o_ref.dtype)


def _attn_decode(layer, q, k_new, v_new, cache_k, cache_v, page_table, lam, rel_bias, subln, post_scale, pp):
    bsz = q.shape[0]
    depth, n_phys, page = cache_k.shape[0], cache_k.shape[1], cache_k.shape[2]
    n_pages = page_table.shape[1]
    n_steps = n_pages // pp
    prow = page * ATT_HEADS
    ck = cache_k.reshape(depth, n_phys, prow, ATT_DV)
    cv = cache_v.reshape(depth, n_phys, prow, ATT_DV)
    rows = jnp.arange(2 * ATT_HEADS)
    qh = jnp.repeat(q.reshape(bsz, ATT_HEADS, ATT_DV), 2, axis=1)
    q8 = jnp.where((jnp.arange(ATT_DV)[None, :] // ATT_DK == rows[:, None] % 2)[None], qh, jnp.zeros((), q.dtype))
    kn8 = jnp.repeat(k_new.reshape(bsz, ATT_HEADS, ATT_DV), 2, axis=1)
    vn8 = jnp.repeat(v_new.reshape(bsz, ATT_HEADS, ATT_DV), 2, axis=1)
    same = (jnp.arange(prow)[None, :] % ATT_HEADS) == (rows[:, None] // 2)
    d_last = page - jnp.arange(prow) // ATT_HEADS
    b_last = jnp.repeat(_bias_by_distance(rel_bias, d_last), 2, axis=0)
    b_far = jnp.broadcast_to(jnp.repeat(rel_bias[NUM_BUCKETS - 1].astype(F32), 2)[:, None], (2 * ATT_HEADS, prow))
    b_self = jnp.broadcast_to(jnp.repeat(rel_bias[0].astype(F32), 2)[:, None], (2 * ATT_HEADS, prow))
    btab = jnp.stack([jnp.where(same, b_last * LOG2E, NEG), jnp.where(same, b_far * LOG2E, NEG), b_self * LOG2E])

    def page_spec(j):
        return BS((None, None, prow, ATT_DV), lambda b, s, pt: (layer, pt[b, s * pp + j], 0, 0))

    const = lambda shp: BS(shp, lambda b, s, pt: tuple(0 for _ in shp))
    per_b = BS((None, 2 * ATT_HEADS, ATT_DV), lambda b, s, pt: (b, 0, 0))
    grid_spec = pltpu.PrefetchScalarGridSpec(
        num_scalar_prefetch=1, grid=(bsz, n_steps),
        in_specs=[BS(memory_space=pltpu.SMEM), per_b, per_b, per_b,
                  const((3, 2 * ATT_HEADS, prow)), const((1, ATT_DV))]
        + [page_spec(j) for j in range(pp)] + [page_spec(j) for j in range(pp)],
        out_specs=BS((None, 1, ATT_W), lambda b, s, pt: (b, 0, 0)),
        scratch_shapes=[pltpu.VMEM((2 * ATT_HEADS, 1), F32), pltpu.VMEM((2 * ATT_HEADS, 1), F32),
                        pltpu.VMEM((2 * ATT_HEADS, ATT_DV), F32)])
    out = pl.pallas_call(
        functools.partial(_decode_body, pp=pp, n_steps=n_steps, post_scale=post_scale),
        out_shape=S((bsz, 1, ATT_W), BF16), grid_spec=grid_spec,
        compiler_params=_cparams(("parallel", "arbitrary")), name="attn_decode",
    )(page_table, lam.reshape(1), q8, kn8, vn8, btab, subln.reshape(1, ATT_DV), *([ck] * pp), *([cv] * pp))
    return out.reshape(bsz, ATT_W)


def _causal_conv(x, prev8, cw_ref, c0, c1):
    acc = x * cw_ref[CONV_K - 1:CONV_K, c0:c1]
    row8 = lax.broadcasted_iota(jnp.int32, prev8.shape, 0)
    for s in range(1, CONV_K):
        r = pltpu.roll(x, s, axis=0)
        pr = pltpu.roll(prev8, s, axis=0)
        head = jnp.where(row8 < s, pr, r[:8])
        sh = jnp.concatenate([head, r[8:]], axis=0) if x.shape[0] > 8 else head
        acc = acc + sh * cw_ref[CONV_K - 1 - s:CONV_K - s, c0:c1]
    return acc


def _tri(cl):
    ii = lax.broadcasted_iota(jnp.int32, (cl, cl), 0)
    jj = lax.broadcasted_iota(jnp.int32, (cl, cl), 1)
    return ii, jj


def _ssd_body(z_ref, xbc_ref, ps_ref, cs0_ref, h0_ref, cw_ref, cb_ref, brow_ref, bcol_ref, arow_ref, acol_ref,
              dexp_ref, nw_ref, y_ref, hl_ref, st_ref, halo_ref, yacc_ref, *, cl, nc, valid_len):
    c = pl.program_id(1)

    @pl.when(c == 0)
    def _():
        st_ref[...] = h0_ref[...]
        halo_ref[...] = cs0_ref[...]

    states = [st_ref[h] for h in range(SSM_HEADS)]
    x_raw = xbc_ref[...]
    conv = _causal_conv(x_raw, halo_ref[...], cw_ref, 0, SSM_CONV_DIM)
    xbc = _silu(conv + cb_ref[...])
    x = xbc[:, :SSM_D_INNER]
    ps = ps_ref[...]
    step_c = _softplus(ps + brow_ref[...])
    step_r = _softplus(ps.T + bcol_ref[...])
    if valid_len is not None:
        t_c = lax.broadcasted_iota(jnp.int32, step_c.shape, 0) + c * cl
        t_r = lax.broadcasted_iota(jnp.int32, step_r.shape, 1) + c * cl
        step_c = jnp.where(t_c < valid_len, step_c, 0.0)
        step_r = jnp.where(t_r < valid_len, step_r, 0.0)
    la_c = (step_c * -jnp.exp(arow_ref[...]))[:, PS_DT:PS_DT + SSM_HEADS]
    la_r = (step_r * -jnp.exp(acol_ref[...]))[PS_DT:PS_DT + SSM_HEADS, :]
    ii, jj = _tri(cl)
    incl = ii >= jj
    acs_c = _dot_hi(incl.astype(F32), la_c)
    acs_r = _dot_hi(la_r, (ii <= jj).astype(F32))
    hpg = SSM_HEADS // SSM_GROUPS
    for g in range(SSM_GROUPS):
        b_g = xbc[:, SSM_D_INNER + g * SSM_STATE:SSM_D_INNER + (g + 1) * SSM_STATE].astype(BF16)
        c0 = SSM_D_INNER + SSM_GROUPS * SSM_STATE + g * SSM_STATE
        c_g = xbc[:, c0:c0 + SSM_STATE].astype(BF16)
        cb = _dot_nt(c_g, b_g)
        for r in range(hpg):
            h = g * hpg + r
            col = acs_c[:, h:h + 1]
            row = acs_r[h:h + 1, :]
            dec = jnp.where(incl, jnp.exp(jnp.where(incl, col - row, 0.0)), 0.0)
            xs = x[:, h * SSM_HEAD_DIM:(h + 1) * SSM_HEAD_DIM] * step_c[:, PS_DT + h:PS_DT + h + 1]
            y_diag = _dot((cb * dec).astype(BF16), xs.astype(BF16))
            hst = states[h]
            y_off = _dot_nt(c_g, hst.astype(BF16)) * jnp.exp(col)
            last = acs_c[cl - 1:cl, h:h + 1]
            to_end = jnp.exp(last - col)
            states[h] = hst * jnp.exp(last) + _dot_tn((xs * to_end).astype(BF16), b_g)
            yacc_ref[:, h * SSM_HEAD_DIM:(h + 1) * SSM_HEAD_DIM] = y_diag + y_off
    for h in range(SSM_HEADS):
        st_ref[h] = states[h]
    halo_ref[...] = x_raw[cl - 8:, :]
    y = (yacc_ref[...] + dexp_ref[...] * x) * _silu(z_ref[...])
    gw = SSM_D_INNER // SSM_GROUPS
    parts = []
    for g in range(SSM_GROUPS):
        yg = y[:, g * gw:(g + 1) * gw]
        parts.append(yg * lax.rsqrt(jnp.mean(yg * yg, axis=-1, keepdims=True) + EPS))
    y_ref[...] = (jnp.concatenate(parts, axis=1) * nw_ref[...]).astype(y_ref.dtype)

    @pl.when(c == nc - 1)
    def _():
        hl_ref[...] = st_ref[...]


def _ssd(proj, ps, conv0, h0, conv_w, conv_b, vec, ssm_d, ssm_norm, bsz, seq, cl, valid_len):
    nc = seq // cl
    n = bsz * seq
    brow, bcol, arow, acol = vec
    cs0 = jnp.pad(conv0, ((0, 0), (8 - (CONV_K - 1), 0), (0, 0)))
    dexp = jnp.repeat(ssm_d, SSM_HEAD_DIM).reshape(1, SSM_D_INNER)
    const = lambda shp: BS(shp, lambda b, c: tuple(0 for _ in shp))
    y, h_last = pl.pallas_call(
        functools.partial(_ssd_body, cl=cl, nc=nc, valid_len=valid_len),
        out_shape=(S((n, SSM_D_INNER), BF16), S(h0.shape, F32)), grid=(bsz, nc),
        in_specs=[BS((cl, SSM_D_INNER), lambda b, c: (b * nc + c, C_SZ // SSM_D_INNER)),
                  BS((cl, SSM_CONV_DIM), lambda b, c: (b * nc + c, C_XBC // SSM_CONV_DIM)),
                  BS((cl, PS_W), lambda b, c: (b * nc + c, 0)),
                  BS((None, 8, SSM_CONV_DIM), lambda b, c: (b, 0, 0)),
                  BS((None, SSM_HEADS, SSM_HEAD_DIM, SSM_STATE), lambda b, c: (b, 0, 0, 0)),
                  const((CONV_K, SSM_CONV_DIM)), const((1, SSM_CONV_DIM)),
                  const((1, PS_W)), const((PS_W, 1)), const((1, PS_W)), const((PS_W, 1)),
                  const((1, SSM_D_INNER)), const((1, SSM_D_INNER))],
        out_specs=(BS((cl, SSM_D_INNER), lambda b, c: (b * nc + c, 0)),
                   BS((None, SSM_HEADS, SSM_HEAD_DIM, SSM_STATE), lambda b, c: (b, 0, 0, 0))),
        scratch_shapes=[pltpu.VMEM((SSM_HEADS, SSM_HEAD_DIM, SSM_STATE), F32), pltpu.VMEM((8, SSM_CONV_DIM), F32),
                        pltpu.VMEM((cl, SSM_D_INNER), F32)],
        compiler_params=_cparams(("parallel", "arbitrary")), name="ssd",
    )(proj, proj, ps, cs0, h0, conv_w, conv_b.reshape(1, -1), brow, bcol, arow, acol, dexp,
      ssm_norm.reshape(1, SSM_D_INNER))
    return y, h_last


def _unit_lower_inverses(mats, ii, jj, cl):
    eye = (ii == jj).astype(F32)
    pair = ((ii >> 1) == (jj >> 1)) & (ii > jj)
    invs = [eye - jnp.where(pair, a, 0.0) for a in mats]
    s = 2
    while s < cl:
        blk = ((ii // (2 * s)) == (jj // (2 * s))) & ((ii % (2 * s)) >= s) & ((jj % (2 * s)) < s)
        xs = [_dot3(inv, jnp.where(blk, a, 0.0)) for inv, a in zip(invs, mats)]
        invs = [inv - _dot3(x, inv) for inv, x in zip(invs, xs)]
        s *= 2
    return invs


def _gdn_body(q_ref, k_ref, v_ref, z_ref, ps_ref, cs0_ref, s0_ref, cw_ref, brow_ref, bcol_ref, arow_ref, acol_ref,
              nw_ref, o_ref, sl_ref, st_ref, halo_ref, *, cl, cps, nc, valid_len):
    c = pl.program_id(1)

    @pl.when(c == 0)
    def _():
        st_ref[...] = s0_ref[...]
        halo_ref[...] = cs0_ref[...]

    w = DN_W
    rows = cl * cps
    halo = halo_ref[...]
    states = [st_ref[h] for h in range(DN_HEADS)]
    raws = (q_ref[...], k_ref[...], v_ref[...])
    q, k, v = (_silu(_causal_conv(raws[p], halo[:, p * w:(p + 1) * w], cw_ref, p * w, (p + 1) * w)) for p in range(3))
    z = z_ref[...]
    ps = ps_ref[...]
    g_c = -jnp.exp(arow_ref[...]) * _softplus(ps + brow_ref[...])
    g_r = -jnp.exp(acol_ref[...]) * _softplus(ps.T + bcol_ref[...])
    beta = _sigmoid(ps)
    if valid_len is not None:
        t_c = lax.broadcasted_iota(jnp.int32, g_c.shape, 0) + c * rows
        t_r = lax.broadcasted_iota(jnp.int32, g_r.shape, 1) + c * rows
        g_c = jnp.where(t_c < valid_len, g_c, 0.0)
        beta = jnp.where(t_c < valid_len, beta, 0.0)
        g_r = jnp.where(t_r < valid_len, g_r, 0.0)
    ii, jj = _tri(cl)
    incl = ii >= jj
    strict = ii > jj
    lower_ones = incl.astype(F32)
    upper_ones = (ii <= jj).astype(F32)
    off = PS_DECAY - 16
    nw = nw_ref[...]
    pre = []
    for ci in range(cps):
        rs = slice(ci * cl, (ci + 1) * cl)
        gcs_c = _dot_hi(lower_ones, g_c[rs, 16:24])
        gcs_r = _dot_hi(g_r[16:24, rs], upper_ones)
        for h in range(DN_HEADS):
            sl = slice(h * DN_DK, (h + 1) * DN_DK)
            qh = q[rs, sl]
            kh = k[rs, sl]
            qh = qh * lax.rsqrt(jnp.sum(qh * qh, axis=-1, keepdims=True) + EPS) * (DN_DK ** -0.5)
            kh = kh * lax.rsqrt(jnp.sum(kh * kh, axis=-1, keepdims=True) + EPS)
            bh = beta[rs, PS_BETA + h:PS_BETA + h + 1]
            col = gcs_c[:, off + h:off + h + 1]
            row = gcs_r[off + h:off + h + 1, :]
            last = gcs_c[cl - 1:cl, off + h:off + h + 1]
            e_col = jnp.exp(col)
            kb = kh * bh
            pre.append(dict(
                dec=jnp.where(incl, jnp.exp(jnp.where(incl, col - row, 0.0)), 0.0),
                q16=qh.astype(BF16), k16=kh.astype(BF16), kb16=kb.astype(BF16),
                vb=v[rs, sl] * bh, kbe=kb * e_col,
                qd16=(qh * e_col).astype(BF16),
                kd16=(kh * jnp.exp(last - col)).astype(BF16),
                g_end=jnp.exp(last),
                gate=nw * _silu(z[rs, sl])))
    lowers = [jnp.where(strict, _dot_nt(t["kb16"], t["k16"]) * t["dec"], 0.0) for t in pre]
    qks = [(_dot_nt(t["q16"], t["k16"]) * t["dec"]).astype(BF16) for t in pre]
    tinvs = _unit_lower_inverses(lowers, ii, jj, cl)
    us = [_dot3(ti, t["vb"]) for ti, t in zip(tinvs, pre)]
    ws = [_dot3(ti, t["kbe"]).astype(BF16) for ti, t in zip(tinvs, pre)]
    outs = []
    for ci in range(cps):
        items = range(ci * DN_HEADS, (ci + 1) * DN_HEADS)
        s16 = [st.astype(BF16) for st in states]
        v16 = [(us[i] - _dot(ws[i], s16[h])).astype(BF16) for h, i in enumerate(items)]
        os_ = [_dot(pre[i]["qd16"], s16[h]) + _dot(qks[i], v16[h]) for h, i in enumerate(items)]
        states = [states[h] * pre[i]["g_end"] + _dot_tn(pre[i]["kd16"], v16[h]) for h, i in enumerate(items)]
        outs += [o * lax.rsqrt(jnp.mean(o * o, axis=-1, keepdims=True) + EPS) * pre[i]["gate"]
                 for o, i in zip(os_, items)]
    for ci in range(cps):
        for h in range(DN_HEADS):
            o_ref[ci * cl:(ci + 1) * cl, h * DN_DV:(h + 1) * DN_DV] = outs[ci * DN_HEADS + h].astype(o_ref.dtype)
    for h in range(DN_HEADS):
        st_ref[h] = states[h]
    for p in range(3):
        halo_ref[:, p * w:(p + 1) * w] = raws[p][rows - 8:, :]

    @pl.when(c == nc - 1)
    def _():
        sl_ref[...] = st_ref[...]


def _gdn(proj, ps, conv0, s0, conv_w, vec, dn_norm, bsz, seq, cl, cps, valid_len):
    rows = cl * cps
    nc = seq // rows
    n = bsz * seq
    brow, bcol, arow, acol = vec
    cs0 = jnp.pad(conv0, ((0, 0), (8 - (CONV_K - 1), 0), (0, 0)))
    w = DN_W
    const = lambda shp: BS(shp, lambda b, c: tuple(0 for _ in shp))
    blk = lambda col: BS((rows, w), lambda b, c: (b * nc + c, col // w))
    o, s_last = pl.pallas_call(
        functools.partial(_gdn_body, cl=cl, cps=cps, nc=nc, valid_len=valid_len),
        out_shape=(S((n, w), BF16), S(s0.shape, F32)), grid=(bsz, nc),
        in_specs=[blk(C_DQ), blk(C_DK), blk(C_DV), blk(C_DZ),
                  BS((rows, PS_W), lambda b, c: (b * nc + c, 0)),
                  BS((None, 8, DN_CONV_DIM), lambda b, c: (b, 0, 0)),
                  BS((None, DN_HEADS, DN_DK, DN_DV), lambda b, c: (b, 0, 0, 0)),
                  const((CONV_K, DN_CONV_DIM)),
                  const((1, PS_W)), const((PS_W, 1)), const((1, PS_W)), const((PS_W, 1)), const((1, DN_DV))],
        out_specs=(BS((rows, w), lambda b, c: (b * nc + c, 0)),
                   BS((None, DN_HEADS, DN_DK, DN_DV), lambda b, c: (b, 0, 0, 0))),
        scratch_shapes=[pltpu.VMEM((DN_HEADS, DN_DK, DN_DV), F32), pltpu.VMEM((8, DN_CONV_DIM), F32)],
        compiler_params=_cparams(("parallel", "arbitrary")), name="gdn",
    )(proj, proj, proj, proj, ps, cs0, s0, conv_w, brow, bcol, arow, acol, dn_norm.reshape(1, DN_DV))
    return o, s_last


def _router_body(h_ref, lw_ref, rw_ref, rb_ref, meta_ref):
    x = h_ref[...]
    xn = x * lax.rsqrt(jnp.mean(x * x, axis=-1, keepdims=True) + EPS) * lw_ref[...]
    logits = _dot_hi(xn, rw_ref[...]) + rb_ref[...]
    lane = lax.broadcasted_iota(jnp.int32, logits.shape, 1)
    big = jnp.int32(logits.shape[1])
    m1 = jnp.max(logits, axis=1, keepdims=True)
    i1 = jnp.min(jnp.where(logits == m1, lane, big), axis=1, keepdims=True)
    rest = jnp.where(lane == i1, NEG, logits)
    m2 = jnp.max(rest, axis=1, keepdims=True)
    i2 = jnp.min(jnp.where(rest == m2, lane, big), axis=1, keepdims=True)
    e2 = jnp.exp(m2 - m1)
    g1 = 1.0 / (1.0 + e2)
    g2 = e2 * g1
    meta = jnp.where(lane == 0, i1.astype(F32), jnp.where(lane == 1, i2.astype(F32),
                     jnp.where(lane == 2, g1, jnp.where(lane == 3, g2, 0.0))))
    meta_ref[...] = meta


def _router(h, ln_w, router_w, router_b, tm):
    n, d = h.shape
    rw = jnp.pad(router_w, ((0, 0), (0, 128 - N_EXPERTS)))
    rb = jnp.pad(router_b, (0, 128 - N_EXPERTS), constant_values=NEG).reshape(1, 128)
    const = lambda shp: BS(shp, lambda i: (0, 0))
    return pl.pallas_call(
        _router_body, out_shape=S((n, 128), F32), grid=(n // tm,),
        in_specs=[BS((tm, d), lambda i: (i, 0)), const((1, d)), const((d, 128)), const((1, 128))],
        out_specs=BS((tm, 128), lambda i: (i, 0)),
        compiler_params=_cparams(("parallel",)), name="router")(h, ln_w.reshape(1, d), rw, rb)


def _experts_body(be_ref, tok_ref, nused_ref, h_hbm, lw_ref, wg_ref, wu_ref, wd_ref, y_ref,
                  xg_ref, xb_ref, acc_ref, sem, *, tm, nf):
    i = pl.program_id(0)
    f = pl.program_id(1)
    used = i < nused_ref[0]
    slot = lax.rem(i, 2)

    def gather(blk, dst):
        def start(r, carry):
            pltpu.make_async_copy(h_hbm.at[pl.ds(tok_ref[blk * tm + r], 1), :],
                                  xg_ref.at[dst, pl.ds(r, 1), :], sem.at[dst]).start()
            return carry

        lax.fori_loop(0, tm, start, 0, unroll=8)

    @pl.when((i == 0) & (f == 0))
    def _():
        gather(0, 0)

    @pl.when(used & (f == 0))
    def _():
        pltpu.make_async_copy(h_hbm.at[pl.ds(0, tm), :], xg_ref.at[slot], sem.at[slot]).wait()
        x = xg_ref[slot]
        xn = x * lax.rsqrt(jnp.mean(x * x, axis=-1, keepdims=True) + EPS) * lw_ref[...]
        xb_ref[...] = xn.astype(BF16)
        acc_ref[...] = jnp.zeros(acc_ref.shape, F32)

    @pl.when((f == 0) & (i + 1 < nused_ref[0]))
    def _():
        gather(i + 1, 1 - slot)

    @pl.when(used)
    def _():
        xb = xb_ref[...]
        hid = _silu(_dot(xb, wg_ref[...])) * _dot(xb, wu_ref[...])
        acc_ref[...] += _dot(hid.astype(BF16), wd_ref[...])

    @pl.when(used & (f == nf - 1))
    def _():
        y_ref[...] = acc_ref[...]

    @pl.when(jnp.logical_not(used) & (f == nf - 1))
    def _():
        y_ref[...] = jnp.zeros(y_ref.shape, F32)


def _experts(h, ln_w, block_expert, row_token, n_used, wg, wu, wd, tm, tf):
    n, d = h.shape
    rows = row_token.shape[0]
    n_blocks = rows // tm
    ff = wg.shape[2]
    nf = ff // tf
    grid_spec = pltpu.PrefetchScalarGridSpec(
        num_scalar_prefetch=3, grid=(n_blocks, nf),
        in_specs=[BS(memory_space=pl.ANY),
                  BS((1, d), lambda i, f, be, tok, nu: (0, 0)),
                  BS((None, d, tf), lambda i, f, be, tok, nu: (be[i], 0, f)),
                  BS((None, d, tf), lambda i, f, be, tok, nu: (be[i], 0, f)),
                  BS((None, tf, d), lambda i, f, be, tok, nu: (be[i], f, 0))],
        out_specs=BS((tm, d), lambda i, f, be, tok, nu: (i, 0)),
        scratch_shapes=[pltpu.VMEM((2, tm, d), F32), pltpu.VMEM((tm, d), BF16), pltpu.VMEM((tm, d), F32),
                        pltpu.SemaphoreType.DMA((2,))])
    return pl.pallas_call(
        functools.partial(_experts_body, tm=tm, nf=nf),
        out_shape=S((rows, d), F32), grid_spec=grid_spec,
        compiler_params=_cparams(("arbitrary", "arbitrary")), name="experts",
    )(block_expert, row_token, n_used, h, ln_w.reshape(1, d), wg, wu, wd)


def _combine_body(pos_ref, y_hbm, h_ref, meta_ref, o_ref, ya_ref, yb_ref, sem, *, tc, n_steps):
    i = pl.program_id(0)
    slot = lax.rem(i, 2)

    def gather(step, dst):
        def start(r, carry):
            t = step * tc + r
            pltpu.make_async_copy(y_hbm.at[pl.ds(pos_ref[2 * t], 1), :], ya_ref.at[dst, pl.ds(r, 1), :],
                                  sem.at[dst]).start()
            pltpu.make_async_copy(y_hbm.at[pl.ds(pos_ref[2 * t + 1], 1), :], yb_ref.at[dst, pl.ds(r, 1), :],
                                  sem.at[dst]).start()
            return carry

        lax.fori_loop(0, tc, start, 0, unroll=8)

    @pl.when(i == 0)
    def _():
        gather(0, 0)

    @pl.when(i + 1 < n_steps)
    def _():
        gather(i + 1, 1 - slot)

    pltpu.make_async_copy(y_hbm.at[pl.ds(0, tc), :], ya_ref.at[slot], sem.at[slot]).wait()
    pltpu.make_async_copy(y_hbm.at[pl.ds(0, tc), :], yb_ref.at[slot], sem.at[slot]).wait()
    meta = meta_ref[...]
    o_ref[...] = h_ref[...] + (meta[:, 2:3] * ya_ref[slot] + meta[:, 3:4] * yb_ref[slot])


def _combine(h, y_sorted, pos, meta, tc):
    n, d = h.shape
    n_steps = n // tc
    grid_spec = pltpu.PrefetchScalarGridSpec(
        num_scalar_prefetch=1, grid=(n_steps,),
        in_specs=[BS(memory_space=pl.ANY), BS((tc, d), lambda i, pos: (i, 0)), BS((tc, 128), lambda i, pos: (i, 0))],
        out_specs=BS((tc, d), lambda i, pos: (i, 0)),
        scratch_shapes=[pltpu.VMEM((2, tc, d), F32), pltpu.VMEM((2, tc, d), F32), pltpu.SemaphoreType.DMA((2,))])
    return pl.pallas_call(
        functools.partial(_combine_body, tc=tc, n_steps=n_steps), out_shape=S((n, d), F32), grid_spec=grid_spec,
        compiler_params=_cparams(("arbitrary",)), name="moe_combine")(pos, y_sorted, h, meta)


def _moe(h, ln_w, router_w, router_b, wg, wu, wd, tm_r, tm, tf, tc):
    n, d = h.shape
    meta = _router(h, ln_w, router_w, router_b, tm_r)
    idx = meta[:, 0:2].astype(jnp.int32)
    member = jnp.sum(jax.nn.one_hot(idx, N_EXPERTS, dtype=jnp.int32), axis=1)
    before = jnp.cumsum(member, axis=0) - member
    counts = jnp.sum(member, axis=0)
    padded = (counts + tm - 1) // tm * tm
    pad_ends = jnp.cumsum(padded)
    pad_starts = pad_ends - padded
    pos = (pad_starts[idx] + jnp.take_along_axis(before, idx, axis=1)).astype(jnp.int32)
    n_blocks = -(-(2 * n + N_EXPERTS * (tm - 1)) // tm)
    rows = n_blocks * tm
    tok = jnp.broadcast_to(jnp.arange(n, dtype=jnp.int32)[:, None], (n, 2))
    row_token = jnp.zeros((rows,), jnp.int32).at[pos.reshape(-1)].set(tok.reshape(-1))
    n_used = (pad_ends[-1] // tm).astype(jnp.int32).reshape(1)
    blk_start = jnp.minimum(jnp.arange(n_blocks, dtype=jnp.int32), n_used[0] - 1) * tm
    block_expert = jnp.minimum(jnp.searchsorted(pad_ends, blk_start, side="right"), N_EXPERTS - 1).astype(jnp.int32)
    y_sorted = _experts(h, ln_w, block_expert, row_token, n_used, wg, wu, wd, tm, tf)
    return _combine(h, y_sorted, pos.reshape(-1), meta, tc)


def _layer(i, h, p, lw, cfg, att_fn, ssm_conv0, ssm_h0, dn_conv0, dn_s0):
    bsz, seq, tm, cl_ssd, cl_gdn, pad_to, valid_len = (cfg[k] for k in
                                                      ("bsz", "seq", "tm", "cl_ssd", "cl_gdn", "pad_to", "valid_len"))
    n = bsz * seq
    xn = _rmsnorm(h, lw["ln_mix"], tm)
    ident = lambda d, e: d[0]
    proj = _fused_matmul([xn], [(0, lw["w_main"])], ident, [], F32, tm, 512, "in_proj")
    ps = _fused_matmul([xn], [(0, lw["w_small"])], ident, [], F32, tm, PS_W, "in_proj_small")

    q16, k32, k16, v16 = _qkv_prep(proj, lw["q_norm"], lw["k_norm"], tm)
    att = att_fn(q16, k16, v16)

    if pad_to is None:
        proj_r, ps_r, seq_r = proj, ps, seq
    else:
        seq_r = pad_to
        proj_r = jnp.pad(proj.reshape(bsz, seq, -1), ((0, 0), (0, pad_to - seq), (0, 0))).reshape(bsz * pad_to, -1)
        ps_r = jnp.pad(ps.reshape(bsz, seq, -1), ((0, 0), (0, pad_to - seq), (0, 0))).reshape(bsz * pad_to, -1)
    y, ssm_h1 = _ssd(proj_r, ps_r, ssm_conv0, ssm_h0, lw["ssm_conv_w"], lw["ssm_conv_b"], lw["vec"], lw["ssm_d"],
                     lw["ssm_norm"], bsz, seq_r, cl_ssd, valid_len)
    o, dn_s1 = _gdn(proj_r, ps_r, dn_conv0, dn_s0, lw["dn_conv_w"], lw["vec"], lw["dn_norm"], bsz, seq_r, cl_gdn,
                    cfg["cps_gdn"], valid_len)
    if pad_to is not None:
        y = y.reshape(bsz, pad_to, -1)[:, :seq].reshape(n, -1)
        o = o.reshape(bsz, pad_to, -1)[:, :seq].reshape(n, -1)

    def merge(d, e):
        return _sigmoid(d[0]) * d[3] + _sigmoid(d[1]) * d[4] + _sigmoid(d[2]) * d[5]

    merged = _fused_matmul([xn, att, y, o],
                           [(0, lw["w_gate_a"]), (0, lw["w_gate_b"]), (0, lw["w_gate_c"]),
                            (1, lw["w_up_att"]), (2, lw["w_up_ssm"]), (3, lw["w_up_dn"])],
                           merge, [], BF16, tm, 512, "merge")
    resid = lambda d, e: e[0] + d[0]
    h = _fused_matmul([merged], [(0, lw["w_out"])], resid, [h], F32, tm, 512, "out_proj")

    if i % 2 == 0:
        hn = _rmsnorm(h, lw["ln_ffn"], tm)
        ff = _fused_matmul([hn], [(0, lw["ffn_w_gate"]), (0, lw["ffn_w_up"])], lambda d, e: _silu(d[0]) * d[1], [],
                           BF16, min(tm, 512), 1408, "ffn_up")
        h = _fused_matmul([ff], [(0, lw["ffn_w_down"])], resid, [h], F32, min(tm, 512), 512, "ffn_down")
    else:
        h = _moe(h, lw["ln_ffn"], lw["router_w"], lw["router_b"], lw["exp_w_gate"], lw["exp_w_up"], lw["exp_w_down"],
                 min(tm, 512), cfg["tm_moe"], TF_MOE, cfg["tc"])

    hn = _rmsnorm(h, lw["ln_ple"], tm)
    h = _fused_matmul([hn, p.astype(BF16)], [(0, lw["ple_w_gate"]), (1, lw["ple_w_proj"])],
                      lambda d, e: e[0] + _sigmoid(d[0]) * d[1], [h], F32, tm, 512, "ple")

    xbc_raw = proj[:, C_XBC:C_XBC + SSM_CONV_DIM].reshape(bsz, seq, -1)
    dn_raw = proj[:, C_DQ:C_DQ + DN_CONV_DIM].reshape(bsz, seq, -1)
    keep = CONV_K - 1
    if seq >= keep:
        ssm_conv1, dn_conv1 = xbc_raw[:, seq - keep:], dn_raw[:, seq - keep:]
    else:
        ssm_conv1 = jnp.concatenate([ssm_conv0, xbc_raw], axis=1)[:, -keep:]
        dn_conv1 = jnp.concatenate([dn_conv0, dn_raw], axis=1)[:, -keep:]
    new_k = k32.reshape(bsz, seq, ATT_HEADS, 2 * ATT_DK)
    new_v = proj[:, C_AV:C_AV + ATT_W].reshape(bsz, seq, ATT_HEADS, ATT_DV)
    return h, (new_k, new_v, ssm_conv1, ssm_h1, dn_conv1, dn_s1)


def _prep_layer_weights(i, W):
    w_in = W["w_in"][i]
    o = 0
    cols = {}
    for name, width in (("aq", 512), ("ak", 512), ("av", 512), ("sz", 1024), ("xbc", 1536), ("dt", 16),
                        ("dq", 512), ("dk", 512), ("dv", 512), ("dz", 512), ("beta", 4), ("decay", 4)):
        cols[name] = w_in[:, o:o + width]
        o += width
    w_main = jnp.concatenate([cols[k] for k in ("xbc", "aq", "sz", "ak", "av", "dq", "dk", "dv", "dz")], axis=1)
    w_small = jnp.concatenate([cols["dt"], cols["beta"], cols["decay"],
                               jnp.zeros((D_MODEL, PS_W - 24), w_in.dtype)], axis=1)
    bias_row = jnp.zeros((PS_W,), F32).at[PS_DT:PS_DT + SSM_HEADS].set(W["ssm_dt_bias"][i])
    bias_row = bias_row.at[PS_DECAY:PS_DECAY + DN_HEADS].set(W["dn_dt_bias"][i])
    alog_row = jnp.zeros((PS_W,), F32).at[PS_DT:PS_DT + SSM_HEADS].set(W["ssm_a_log"][i])
    alog_row = alog_row.at[PS_DECAY:PS_DECAY + DN_HEADS].set(W["dn_a_log"][i])
    vec = (bias_row.reshape(1, PS_W), bias_row.reshape(PS_W, 1), alog_row.reshape(1, PS_W), alog_row.reshape(PS_W, 1))
    wg = W["w_gate"][i]
    lw = {
        "ln_mix": W["ln_mix"][i], "w_main": w_main.astype(BF16), "w_small": w_small.astype(BF16),
        "q_norm": W["q_norm"][i], "k_norm": W["k_norm"][i], "att_subln": W["att_subln"][i],
        "ssm_conv_w": W["ssm_conv_w"][i], "ssm_conv_b": W["ssm_conv_b"][i], "vec": vec,
        "ssm_d": W["ssm_d"][i], "ssm_norm": W["ssm_norm"][i],
        "dn_conv_w": W["dn_conv_w"][i], "dn_norm": W["dn_norm"][i],
        "w_gate_a": wg[:, :D_MODEL].astype(BF16), "w_gate_b": wg[:, D_MODEL:2 * D_MODEL].astype(BF16),
        "w_gate_c": wg[:, 2 * D_MODEL:].astype(BF16),
        "w_up_att": W["w_up_att"][i].astype(BF16), "w_up_ssm": W["w_up_ssm"][i].astype(BF16),
        "w_up_dn": W["w_up_dn"][i].astype(BF16), "w_out": W["w_out"][i].astype(BF16),
        "ln_ffn": W["ln_ffn"][i], "ln_ple": W["ln_ple"][i],
        "ple_w_gate": W["ple_w_gate"][i].astype(BF16), "ple_w_proj": W["ple_w_proj"][i].astype(BF16),
    }
    if i % 2 == 0:
        lw.update(ffn_w_gate=W["ffn_w_gate"][i // 2].astype(BF16), ffn_w_up=W["ffn_w_up"][i // 2].astype(BF16),
                  ffn_w_down=W["ffn_w_down"][i // 2].astype(BF16))
    else:
        lw.update(router_w=W["router_w"][i // 2], router_b=W["router_b"][i // 2],
                  exp_w_gate=W["exp_w_gate"][i // 2].astype(BF16), exp_w_up=W["exp_w_up"][i // 2].astype(BF16),
                  exp_w_down=W["exp_w_down"][i // 2].astype(BF16))
    lam_init = 0.8 - 0.6 * math.exp(-0.3 * i)
    lam = (jnp.exp(jnp.sum(W["lam_q1"][i] * W["lam_k1"][i])) - jnp.exp(jnp.sum(W["lam_q2"][i] * W["lam_k2"][i]))
           + lam_init).astype(F32)
    return lw, lam, lam_init


def _pick(n, pref):
    t = min(n, pref)
    while n % t:
        t //= 2
    return t


def kernel(x_prompt, x_sample, cache_k, cache_v, state_ssm_conv, state_ssm, state_dn_conv, state_dn, page_table, p_prompt, p_sample, ln_mix, w_in, w_gate, q_norm, k_norm, lam_q1, lam_k1, lam_q2, lam_k2, att_subln, rel_bias, ssm_conv_w, ssm_conv_b, ssm_dt_bias, ssm_a_log, ssm_d, ssm_norm, dn_conv_w, dn_dt_bias, dn_a_log, dn_norm, w_up_att, w_up_ssm, w_up_dn, w_out, ln_ffn, ffn_w_gate, ffn_w_up, ffn_w_down, router_w, router_b, exp_w_gate, exp_w_up, exp_w_down, ln_ple, ple_w_gate, ple_w_proj):
    W = dict(ln_mix=ln_mix, w_in=w_in, w_gate=w_gate, q_norm=q_norm, k_norm=k_norm, lam_q1=lam_q1, lam_k1=lam_k1,
             lam_q2=lam_q2, lam_k2=lam_k2, att_subln=att_subln, ssm_conv_w=ssm_conv_w, ssm_conv_b=ssm_conv_b,
             ssm_dt_bias=ssm_dt_bias, ssm_a_log=ssm_a_log, ssm_d=ssm_d, ssm_norm=ssm_norm, dn_conv_w=dn_conv_w,
             dn_dt_bias=dn_dt_bias, dn_a_log=dn_a_log, dn_norm=dn_norm, w_up_att=w_up_att, w_up_ssm=w_up_ssm,
             w_up_dn=w_up_dn, w_out=w_out, ln_ffn=ln_ffn, ffn_w_gate=ffn_w_gate, ffn_w_up=ffn_w_up,
             ffn_w_down=ffn_w_down, router_w=router_w, router_b=router_b, exp_w_gate=exp_w_gate, exp_w_up=exp_w_up,
             exp_w_down=exp_w_down, ln_ple=ln_ple, ple_w_gate=ple_w_gate, ple_w_proj=ple_w_proj)
    depth = ln_mix.shape[0]
    bp, lp, d = x_prompt.shape
    bs, ls, _ = x_sample.shape
    n_pages = page_table.shape[1]
    t_att = _pick(lp, T_ATT)
    cfg_p = dict(bsz=bp, seq=lp, tm=_pick(bp * lp, TM), cl_ssd=_pick(lp, CL_SSD), cl_gdn=_pick(lp, CL_GDN), pad_to=None,
                 valid_len=None, tm_moe=_pick(bp * lp, TM_MOE), tc=_pick(bp * lp, TC_MOE))
    cfg_p["cps_gdn"] = _pick(lp // cfg_p["cl_gdn"], CPS_GDN)
    cfg_s = dict(bsz=bs, seq=ls, tm=bs * ls, cl_ssd=64, cl_gdn=64, cps_gdn=1, pad_to=64, valid_len=ls,
                 tm_moe=64, tc=bs * ls)
    hp = x_prompt.reshape(bp * lp, d)
    hs = x_sample.reshape(bs * ls, d)
    st_p, st_s = [], []
    tabs = _bias_tables(rel_bias, t_att)
    for i in range(depth):
        lw, lam, lam_init = _prep_layer_weights(i, W)
        post = 1.0 - lam_init
        att_p = lambda q, k, v: _attn_prompt(q, k, v, lam, rel_bias, tabs, lw["att_subln"], post, bp, lp, t_att)
        hp, st = _layer(i, hp, p_prompt[i].reshape(bp * lp, -1), lw, cfg_p, att_p,
                        jnp.zeros((bp, CONV_K - 1, SSM_CONV_DIM), F32),
                        jnp.zeros((bp, SSM_HEADS, SSM_HEAD_DIM, SSM_STATE), F32),
                        jnp.zeros((bp, CONV_K - 1, DN_CONV_DIM), F32),
                        jnp.zeros((bp, DN_HEADS, DN_DK, DN_DV), F32))
        st_p.append(st)
        att_s = lambda q, k, v: _attn_decode(i, q, k, v, cache_k, cache_v, page_table, lam, rel_bias,
                                             lw["att_subln"], post, _pick(n_pages, PAGES_PER_STEP))
        hs, st = _layer(i, hs, p_sample[i].reshape(bs * ls, -1), lw, cfg_s, att_s,
                        state_ssm_conv[i], state_ssm[i], state_dn_conv[i], state_dn[i])
        st_s.append(st)
    stk = lambda sts, j: jnp.stack([s[j] for s in sts])
    return (hp.reshape(bp, lp, d), hs.reshape(bs, ls, d),
            stk(st_p, 0), stk(st_p, 1), stk(st_p, 2), stk(st_p, 3), stk(st_p, 4), stk(st_p, 5),
            stk(st_s, 0), stk(st_s, 1), stk(st_s, 2), stk(st_s, 3), stk(st_s, 4), stk(st_s, 5))
```

```python
import functools
import math

import jax
import jax.numpy as jnp
from jax import lax
from jax.experimental import pallas as pl
from jax.experimental.pallas import tpu as pltpu

F32 = jnp.float32
BF16 = jnp.bfloat16
S = jax.ShapeDtypeStruct
BS = pl.BlockSpec

D_MODEL = 1024
ATT_HEADS = 4
ATT_DK = 64
ATT_DV = 128
ATT_W = ATT_HEADS * ATT_DV
NUM_BUCKETS = 32
MAX_DISTANCE = 128
SSM_HEADS = 16
SSM_HEAD_DIM = 64
SSM_GROUPS = 2
SSM_STATE = 128
SSM_D_INNER = 1024
SSM_CONV_DIM = 1536
DN_HEADS = 4
DN_DK = 128
DN_DV = 128
DN_W = 512
DN_CONV_DIM = 1536
CONV_K = 4
N_EXPERTS = 8
EPS = 1e-6
NEG = -1e30
LOG2E = 1.4426950408889634

C_XBC, C_AQ, C_SZ, C_AK, C_AV, C_DQ, C_DK, C_DV, C_DZ = 0, 1536, 2048, 3072, 3584, 4096, 4608, 5120, 5632
PROJ_W = 6144
PS_DT, PS_BETA, PS_DECAY, PS_W = 0, 16, 20, 128

VMEM_LIMIT = 56 * 1024 * 1024
T_ATT = 512
TM = 1024
CL_SSD = 128
CL_GDN = 64
CPS_GDN = 2
CL_SAMPLE = 16
PAGES_PER_STEP = 8
TM_MOE = 512
TF_MOE = 896
TC_MOE = 256


def _cparams(sem):
    return pltpu.CompilerParams(dimension_semantics=sem, vmem_limit_bytes=VMEM_LIMIT)


def _dot(a, b):
    return jnp.dot(a, b, preferred_element_type=F32)


def _dot_nt(a, b):
    return lax.dot_general(a, b, (((1,), (1,)), ((), ())), preferred_element_type=F32)


def _dot_tn(a, b):
    return lax.dot_general(a, b, (((0,), (0,)), ((), ())), preferred_element_type=F32)


def _dot_hi(a, b):
    return jnp.dot(a, b, preferred_element_type=F32, precision=lax.Precision.HIGHEST)


def _split(a):
    hi = a.astype(BF16)
    lo = (a - hi.astype(F32)).astype(BF16)
    return hi, lo


def _dot3(a, b):
    ah, al = _split(a)
    bh, bl = _split(b)
    return _dot(ah, bh) + (_dot(ah, bl) + _dot(al, bh))


def _sigmoid(x):
    return 1.0 / (1.0 + jnp.exp(-x))


def _silu(x):
    return x * _sigmoid(x)


def _softplus(x):
    return jnp.maximum(x, 0.0) + jnp.log1p(jnp.exp(-jnp.abs(x)))


def _rmsnorm_body(x_ref, w_ref, o_ref):
    x = x_ref[...]
    y = x * lax.rsqrt(jnp.mean(x * x, axis=-1, keepdims=True) + EPS) * w_ref[...]
    o_ref[...] = y.astype(o_ref.dtype)


def _rmsnorm(x, w, tm):
    n, d = x.shape
    return pl.pallas_call(
        _rmsnorm_body, out_shape=S((n, d), BF16), grid=(n // tm,),
        in_specs=[BS((tm, d), lambda i: (i, 0)), BS((1, d), lambda i: (0, 0))],
        out_specs=BS((tm, d), lambda i: (i, 0)),
        compiler_params=_cparams(("parallel",)), name="rmsnorm")(x, w.reshape(1, d))


def _fused_matmul(acts, pairs, combine, extras, out_dtype, m, tm, tn, name):
    n = acts[0].shape[0]
    na, npair, nex = len(acts), len(pairs), len(extras)
    act_idx = [a for a, _, _ in pairs]

    def body(*refs):
        a_vals = [r[...] for r in refs[:na]]
        dots = [_dot(a_vals[act_idx[j]], refs[na + j][...].astype(BF16)) for j in range(npair)]
        ex = [r[...] for r in refs[na + npair:na + npair + nex]]
        o_ref = refs[na + npair + nex]
        o_ref[...] = combine(dots, ex).astype(o_ref.dtype)

    def w_spec(w, c0):
        off = c0 // tn
        if isinstance(w, tuple):
            arr, layer = w
            return BS((None, arr.shape[1], tn), lambda i, j: (layer, 0, j + off))
        return BS((w.shape[0], tn), lambda i, j: (0, j + off))

    in_specs = ([BS((tm, a.shape[1]), lambda i, j: (i, 0)) for a in acts]
                + [w_spec(w, c0) for _, w, c0 in pairs]
                + [BS((tm, tn), lambda i, j: (i, j)) for _ in extras])
    return pl.pallas_call(
        body, out_shape=S((n, m), out_dtype), grid=(n // tm, m // tn),
        in_specs=in_specs, out_specs=BS((tm, tn), lambda i, j: (i, j)),
        compiler_params=_cparams(("parallel", "arbitrary")), name=name,
    )(*acts, *[w[0] if isinstance(w, tuple) else w for _, w, _ in pairs], *extras)


def _qkv_body(aq_ref, ak_ref, av_ref, g_ref, qw_ref, kw_ref, q_ref, kf_ref, kb_ref, vb_ref):
    g = g_ref[...]

    def gnorm(x, w):
        hi, lo = _split(x * x)
        ss = _dot(hi, g) + _dot(lo, g)
        return x * lax.rsqrt(ss * (1.0 / ATT_DK) + EPS) * w

    q = gnorm(aq_ref[...], qw_ref[...]) * (ATT_DK ** -0.5 * LOG2E)
    k = gnorm(ak_ref[...], kw_ref[...])
    q_ref[...] = q.astype(BF16)
    kf_ref[...] = k
    kb_ref[...] = k.astype(BF16)
    vb_ref[...] = av_ref[...].astype(BF16)


def _qkv_prep(proj, q_norm, k_norm, tm):
    n = proj.shape[0]
    w = ATT_W
    gi = jnp.arange(w) // ATT_DK
    gmat = (gi[:, None] == gi[None, :]).astype(BF16)
    qw = jnp.tile(q_norm, w // ATT_DK).reshape(1, w)
    kw = jnp.tile(k_norm, w // ATT_DK).reshape(1, w)
    row = lambda c: BS((tm, w), lambda i: (i, c))
    const = lambda shp: BS(shp, lambda i: (0, 0))
    return pl.pallas_call(
        _qkv_body,
        out_shape=(S((n, w), BF16), S((n, w), F32), S((n, w), BF16), S((n, w), BF16)),
        grid=(n // tm,),
        in_specs=[row(C_AQ // w), row(C_AK // w), row(C_AV // w), const((w, w)), const((1, w)), const((1, w))],
        out_specs=(row(0), row(0), row(0), row(0)),
        compiler_params=_cparams(("parallel",)), name="qkv_prep")(proj, proj, proj, gmat, qw, kw)


def _t5_bucket(n):
    max_exact = NUM_BUCKETS // 2
    scaled = jnp.log(jnp.maximum(n, 1).astype(F32) / max_exact) / math.log(MAX_DISTANCE / max_exact)
    large = jnp.minimum(max_exact + (scaled * (NUM_BUCKETS - max_exact)).astype(jnp.int32), NUM_BUCKETS - 1)
    return jnp.where(n < max_exact, n, large)


def _bias_by_distance(rel_bias, n):
    return jnp.moveaxis(rel_bias[_t5_bucket(n)], -1, 0).astype(F32)


def _bias_table_body(rb_ref, o_ref, *, t, rows):
    h = pl.program_id(0)
    max_exact = NUM_BUCKETS // 2

    def block(i, carry):
        r0 = pl.multiple_of(i * rows, rows)
        c = lax.broadcasted_iota(jnp.int32, (rows, t), 0) + r0
        r = lax.broadcasted_iota(jnp.int32, (rows, t), 1)
        for idx in range(2):
            n = r - c + idx * t
            nn = jnp.maximum(n, 0)
            scaled = jnp.log(jnp.maximum(nn, 1).astype(F32) / max_exact) / math.log(MAX_DISTANCE / max_exact)
            large = jnp.minimum(max_exact + (scaled * (NUM_BUCKETS - max_exact)).astype(jnp.int32), NUM_BUCKETS - 1)
            bucket = jnp.where(nn < max_exact, nn, large)
            val = jnp.zeros((rows, t), F32)
            for b in range(NUM_BUCKETS):
                val = jnp.where(bucket == b, rb_ref[b, h] * LOG2E, val)
            if idx == 0:
                val = jnp.where(n >= 0, val, NEG)
            o_ref[idx, pl.ds(r0, rows), :] = val
        return carry

    lax.fori_loop(0, t // rows, block, 0)


def _bias_tables(rel_bias, t):
    return pl.pallas_call(
        functools.partial(_bias_table_body, t=t, rows=8),
        out_shape=S((ATT_HEADS, 2, t, t), F32), grid=(ATT_HEADS,),
        in_specs=[BS(memory_space=pltpu.SMEM)],
        out_specs=BS((None, 2, t, t), lambda h: (h, 0, 0, 0)),
        compiler_params=_cparams(("parallel",)), name="bias_tables")(rel_bias.astype(F32))


def _attn_body(lam_ref, cfar_ref, q_ref, k_ref, vt_ref, tab_ref, sw_ref, o_ref,
               q2_ref, s_ref, sb_ref, p_ref, m_ref, l_ref, acc_ref, *, t, rb, post_scale):
    h = pl.program_id(1)
    qi = pl.program_id(2)
    q = q_ref[...]
    lane = lax.broadcasted_iota(jnp.int32, q.shape, 1)
    zero = jnp.zeros_like(q)
    q2_ref[0:t, :] = jnp.where(lane < ATT_DK, q, zero)
    q2_ref[t:2 * t, :] = jnp.where(lane >= ATT_DK, q, zero)
    m_ref[...] = jnp.full(m_ref.shape, -jnp.inf, F32)
    l_ref[...] = jnp.zeros(l_ref.shape, F32)
    acc_ref[...] = jnp.zeros(acc_ref.shape, F32)
    cfar = cfar_ref[h] * LOG2E
    w2 = 2 * t

    def tile(ki, tab_idx):
        s_ref[...] = _dot_nt(k_ref[pl.ds(pl.multiple_of(ki * t, t), t), :], q2_ref[...])
        src_ref = s_ref if tab_idx is None else sb_ref

        def key_max(g, mx):
            rows = pl.ds(pl.multiple_of(g * rb, rb), rb)
            s = s_ref[rows, :]
            if tab_idx is not None:
                tab = tab_ref[tab_idx, rows, :]
                s = s + jnp.concatenate([tab, tab], axis=1)
                sb_ref[rows, :] = s
            for a in range(0, rb, 8):
                mx = jnp.maximum(mx, s[a:a + 8, :])
            return mx

        mx8 = lax.fori_loop(0, t // rb, key_max, jnp.full((8, w2), -jnp.inf, F32), unroll=4)
        tile_max = jnp.max(mx8, axis=0, keepdims=True)
        if tab_idx is None:
            tile_max = tile_max + cfar
        m_old = m_ref[...]
        m_new = jnp.maximum(m_old, tile_max)
        alpha = jnp.exp2(m_old - m_new)
        m_ref[...] = m_new
        shift = jnp.broadcast_to(m_new if tab_idx is not None else m_new - cfar, (rb, w2))

        def key_exp(g, tot):
            rows = pl.ds(pl.multiple_of(g * rb, rb), rb)
            p = jnp.exp2(src_ref[rows, :] - shift)
            p_ref[rows, :] = p.astype(BF16)
            for a in range(0, rb, 8):
                tot = tot + p[a:a + 8, :]
            return tot

        tot8 = lax.fori_loop(0, t // rb, key_exp, jnp.zeros((8, w2), F32), unroll=4)
        l_ref[...] = alpha * l_ref[...] + jnp.sum(tot8, axis=0, keepdims=True)
        acc_ref[...] = alpha * acc_ref[...] + _dot(vt_ref[ki], p_ref[...])

    def far(ki, carry):
        tile(ki, None)
        return carry

    lax.fori_loop(0, jnp.maximum(qi - 1, 0), far, 0)

    @pl.when(qi >= 1)
    def _():
        tile(qi - 1, 1)

    tile(qi, 0)
    ot = acc_ref[...] * (1.0 / l_ref[...])
    o = (ot[:, 0:t] - lam_ref[0] * ot[:, t:w2]).T
    o = o * lax.rsqrt(jnp.mean(o * o, axis=-1, keepdims=True) + EPS) * sw_ref[...] * post_scale
    o_ref[...] = o.astype(o_ref.dtype)


def _attn_prompt(q, k, v, lam, rel_bias, tabs, subln, post_scale, bsz, seq, t):
    nq = seq // t
    cfar = rel_bias[NUM_BUCKETS - 1].astype(F32)
    vt = v.reshape(bsz, nq, t, ATT_HEADS, ATT_DV).transpose(0, 3, 1, 4, 2).reshape(bsz * ATT_HEADS, nq, ATT_DV, t)
    smem = BS(memory_space=pltpu.SMEM)
    return pl.pallas_call(
        functools.partial(_attn_body, t=t, rb=16, post_scale=post_scale),
        out_shape=S((bsz * seq, ATT_W), BF16), grid=(bsz, ATT_HEADS, nq),
        in_specs=[smem, smem,
                  BS((t, ATT_DV), lambda b, h, qi: (b * nq + qi, h)),
                  BS((seq, ATT_DV), lambda b, h, qi: (b, h)),
                  BS((None, nq, ATT_DV, t), lambda b, h, qi: (b * ATT_HEADS + h, 0, 0, 0)),
                  BS((None, 2, t, t), lambda b, h, qi: (h, 0, 0, 0)),
                  BS((1, ATT_DV), lambda b, h, qi: (0, 0))],
        out_specs=BS((t, ATT_DV), lambda b, h, qi: (b * nq + qi, h)),
        scratch_shapes=[pltpu.VMEM((2 * t, ATT_DV), BF16), pltpu.VMEM((t, 2 * t), F32), pltpu.VMEM((t, 2 * t), F32),
                        pltpu.VMEM((t, 2 * t), BF16), pltpu.VMEM((1, 2 * t), F32), pltpu.VMEM((1, 2 * t), F32),
                        pltpu.VMEM((ATT_DV, 2 * t), F32)],
        compiler_params=_cparams(("parallel", "parallel", "arbitrary")), name="attn_prompt",
    )(lam.reshape(1), cfar, q, k, vt, tabs, subln.reshape(1, ATT_DV))


def _decode_body(pt_ref, lam_ref, q8_ref, kn_ref, vn_ref, btab_ref, sw_ref, *rest, pp, n_steps, post_scale):
    k_refs = rest[:pp]
    v_refs = rest[pp:2 * pp]
    o_ref = rest[2 * pp]
    m_ref, l_ref, acc_ref = rest[2 * pp + 1:]
    s_idx = pl.program_id(1)

    @pl.when(s_idx == 0)
    def _():
        m_ref[...] = jnp.full(m_ref.shape, -jnp.inf, F32)
        l_ref[...] = jnp.zeros(l_ref.shape, F32)
        acc_ref[...] = jnp.zeros(acc_ref.shape, F32)

    q8 = q8_ref[...]
    n_pages = n_steps * pp
    scores = []
    for j in range(pp):
        page = s_idx * pp + j
        bias = jnp.where(page == n_pages - 1, btab_ref[0], btab_ref[1])
        scores.append(_dot_nt(q8, k_refs[j][...].astype(BF16)) + bias)
    s_all = jnp.concatenate(scores, axis=1)
    m_old = m_ref[...]
    m_new = jnp.maximum(m_old, jnp.max(s_all, axis=1, keepdims=True))
    alpha = jnp.exp2(m_old - m_new)
    p_all = jnp.exp2(s_all - m_new)
    l_ref[...] = alpha * l_ref[...] + jnp.sum(p_all, axis=1, keepdims=True)
    ps = p_all.shape[1] // pp
    p16 = p_all.astype(BF16)
    pv = _dot(p16[:, :ps], v_refs[0][...].astype(BF16))
    for j in range(1, pp):
        pv = pv + _dot(p16[:, j * ps:(j + 1) * ps], v_refs[j][...].astype(BF16))
    acc_ref[...] = alpha * acc_ref[...] + pv
    m_ref[...] = m_new

    @pl.when(s_idx == n_steps - 1)
    def _():
        s_self = (jnp.sum(q8.astype(F32) * kn_ref[...].astype(F32), axis=1, keepdims=True)
                  + btab_ref[2][:, 0:1])
        m_o = m_ref[...]
        m_f = jnp.maximum(m_o, s_self)
        a_o = jnp.exp2(m_o - m_f)
        p_s = jnp.exp2(s_self - m_f)
        l_f = a_o * l_ref[...] + p_s
        acc = a_o * acc_ref[...] + p_s * vn_ref[...].astype(F32)
        o8 = acc * (1.0 / l_f)
        lam = lam_ref[0]
        outs = []
        for h in range(ATT_HEADS):
            o = o8[2 * h:2 * h + 1, :] - lam * o8[2 * h + 1:2 * h + 2, :]
            o = o * lax.rsqrt(jnp.mean(o * o, axis=-1, keepdims=True) + EPS) * sw_ref[...] * post_scale
            outs.append(o)
        o_ref[...] = jnp.concatenate(outs, axis=1).astype(o_ref.dtype)


def _attn_decode(layer, q, k_new, v_new, cache_k, cache_v, page_table, lam, rel_bias, subln, post_scale, pp):
    bsz = q.shape[0]
    depth, n_phys, page = cache_k.shape[0], cache_k.shape[1], cache_k.shape[2]
    n_pages = page_table.shape[1]
    n_steps = n_pages // pp
    prow = page * ATT_HEADS
    ck = cache_k.reshape(depth, n_phys, prow, ATT_DV)
    cv = cache_v.reshape(depth, n_phys, prow, ATT_DV)
    rows = jnp.arange(2 * ATT_HEADS)
    qh = jnp.repeat(q.reshape(bsz, ATT_HEADS, ATT_DV), 2, axis=1)
    q8 = jnp.where((jnp.arange(ATT_DV)[None, :] // ATT_DK == rows[:, None] % 2)[None], qh, jnp.zeros((), q.dtype))
    kn8 = jnp.repeat(k_new.reshape(bsz, ATT_HEADS, ATT_DV), 2, axis=1)
    vn8 = jnp.repeat(v_new.reshape(bsz, ATT_HEADS, ATT_DV), 2, axis=1)
    same = (jnp.arange(prow)[None, :] % ATT_HEADS) == (rows[:, None] // 2)
    d_last = page - jnp.arange(prow) // ATT_HEADS
    b_last = jnp.repeat(_bias_by_distance(rel_bias, d_last), 2, axis=0)
    b_far = jnp.broadcast_to(jnp.repeat(rel_bias[NUM_BUCKETS - 1].astype(F32), 2)[:, None], (2 * ATT_HEADS, prow))
    b_self = jnp.broadcast_to(jnp.repeat(rel_bias[0].astype(F32), 2)[:, None], (2 * ATT_HEADS, prow))
    btab = jnp.stack([jnp.where(same, b_last * LOG2E, NEG), jnp.where(same, b_far * LOG2E, NEG), b_self * LOG2E])

    def page_spec(j):
        return BS((None, None, prow, ATT_DV), lambda b, s, pt: (layer, pt[b, s * pp + j], 0, 0))

    const = lambda shp: BS(shp, lambda b, s, pt: tuple(0 for _ in shp))
    per_b = BS((None, 2 * ATT_HEADS, ATT_DV), lambda b, s, pt: (b, 0, 0))
    grid_spec = pltpu.PrefetchScalarGridSpec(
        num_scalar_prefetch=1, grid=(bsz, n_steps),
        in_specs=[BS(memory_space=pltpu.SMEM), per_b, per_b, per_b,
                  const((3, 2 * ATT_HEADS, prow)), const((1, ATT_DV))]
        + [page_spec(j) for j in range(pp)] + [page_spec(j) for j in range(pp)],
        out_specs=BS((None, 1, ATT_W), lambda b, s, pt: (b, 0, 0)),
        scratch_shapes=[pltpu.VMEM((2 * ATT_HEADS, 1), F32), pltpu.VMEM((2 * ATT_HEADS, 1), F32),
                        pltpu.VMEM((2 * ATT_HEADS, ATT_DV), F32)])
    out = pl.pallas_call(
        functools.partial(_decode_body, pp=pp, n_steps=n_steps, post_scale=post_scale),
        out_shape=S((bsz, 1, ATT_W), BF16), grid_spec=grid_spec,
        compiler_params=_cparams(("parallel", "arbitrary")), name="attn_decode",
    )(page_table, lam.reshape(1), q8, kn8, vn8, btab, subln.reshape(1, ATT_DV), *([ck] * pp), *([cv] * pp))
    return out.reshape(bsz, ATT_W)


def _causal_conv(x, prev8, cw_ref, c0, c1):
    acc = x * cw_ref[CONV_K - 1:CONV_K, c0:c1]
    row8 = lax.broadcasted_iota(jnp.int32, prev8.shape, 0)
    for s in range(1, CONV_K):
        r = pltpu.roll(x, s, axis=0)
        pr = pltpu.roll(prev8, s, axis=0)
        head = jnp.where(row8 < s, pr, r[:8])
        sh = jnp.concatenate([head, r[8:]], axis=0) if x.shape[0] > 8 else head
        acc = acc + sh * cw_ref[CONV_K - 1 - s:CONV_K - s, c0:c1]
    return acc


def _tri(cl):
    ii = lax.broadcasted_iota(jnp.int32, (cl, cl), 0)
    jj = lax.broadcasted_iota(jnp.int32, (cl, cl), 1)
    return ii, jj


def _ssd_body(z_ref, xbc_ref, ps_ref, cs0_ref, h0_ref, cw_ref, cb_ref, brow_ref, bcol_ref, arow_ref, acol_ref,
              dexp_ref, nw_ref, ex_ref, y_ref, hl_ref, st_ref, halo_ref, yacc_ref, *, cl, nc, valid_len):
    c = pl.program_id(1)

    @pl.when(c == 0)
    def _():
        st_ref[...] = h0_ref[...]
        halo_ref[...] = cs0_ref[...]

    n_pairs = SSM_HEADS // 2
    pw = 2 * SSM_HEAD_DIM
    states = [st_ref[j] for j in range(n_pairs)]
    x_raw = xbc_ref[...]
    conv = _causal_conv(x_raw, halo_ref[...], cw_ref, 0, SSM_CONV_DIM)
    xbc = _silu(conv + cb_ref[...])
    x = xbc[:, :SSM_D_INNER]
    ps = ps_ref[...]
    step_c = _softplus(ps + brow_ref[...])
    step_r = _softplus(ps.T + bcol_ref[...])
    if valid_len is not None:
        t_c = lax.broadcasted_iota(jnp.int32, step_c.shape, 0) + c * cl
        t_r = lax.broadcasted_iota(jnp.int32, step_r.shape, 1) + c * cl
        step_c = jnp.where(t_c < valid_len, step_c, 0.0)
        step_r = jnp.where(t_r < valid_len, step_r, 0.0)
    la_c = step_c * -jnp.exp(arow_ref[...])
    la_r = (step_r * -jnp.exp(acol_ref[...]))[PS_DT:PS_DT + SSM_HEADS, :]
    ii, jj = _tri(cl)
    incl = ii >= jj
    acs_c = _dot_hi(incl.astype(F32), la_c)
    acs_r = _dot_hi(la_r, (ii <= jj).astype(F32))
    last = acs_c[cl - 1:cl, :]
    e_last = jnp.exp(last)

    def expand(a):
        hi, lo = _split(a)
        return _dot(hi, ex_ref[...]) + _dot(lo, ex_ref[...])

    xs = x * expand(step_c)
    xe = (xs * expand(jnp.exp(last - acs_c))).astype(BF16)
    e_acs = expand(jnp.exp(acs_c))
    xs16 = xs.astype(BF16)
    lane = lax.broadcasted_iota(jnp.int32, (cl, pw), 1)
    srow = lax.broadcasted_iota(jnp.int32, (pw, 1), 0)
    ppg = n_pairs // SSM_GROUPS
    for g in range(SSM_GROUPS):
        b_g = xbc[:, SSM_D_INNER + g * SSM_STATE:SSM_D_INNER + (g + 1) * SSM_STATE].astype(BF16)
        c0 = SSM_D_INNER + SSM_GROUPS * SSM_STATE + g * SSM_STATE
        c_g = xbc[:, c0:c0 + SSM_STATE].astype(BF16)
        cb = _dot_nt(c_g, b_g)
        for r in range(ppg):
            j = g * ppg + r
            cols = slice(j * pw, (j + 1) * pw)
            y_heads = []
            for h in (2 * j, 2 * j + 1):
                dec = jnp.where(incl, jnp.exp(jnp.where(incl, acs_c[:, h:h + 1] - acs_r[h:h + 1, :], 0.0)), 0.0)
                y_heads.append(_dot((cb * dec).astype(BF16), xs16[:, cols]))
            y_diag = jnp.where(lane < SSM_HEAD_DIM, y_heads[0], y_heads[1])
            hst = states[j]
            y_off = _dot_nt(c_g, hst.astype(BF16)) * e_acs[:, cols]
            keep = jnp.where(srow < SSM_HEAD_DIM, e_last[:, 2 * j:2 * j + 1], e_last[:, 2 * j + 1:2 * j + 2])
            states[j] = hst * keep + _dot_tn(xe[:, cols], b_g)
            yacc_ref[:, cols] = y_diag + y_off
    for j in range(n_pairs):
        st_ref[j] = states[j]
    halo_ref[...] = x_raw[cl - 8:, :]
    y = (yacc_ref[...] + dexp_ref[...] * x) * _silu(z_ref[...])
    gw = SSM_D_INNER // SSM_GROUPS
    parts = []
    for g in range(SSM_GROUPS):
        yg = y[:, g * gw:(g + 1) * gw]
        parts.append(yg * lax.rsqrt(jnp.mean(yg * yg, axis=-1, keepdims=True) + EPS))
    y_ref[...] = (jnp.concatenate(parts, axis=1) * nw_ref[...]).astype(y_ref.dtype)

    @pl.when(c == nc - 1)
    def _():
        hl_ref[...] = st_ref[...]


def _ssd(proj, ps, conv0, h0, conv_w, conv_b, vec, ssm_d, ssm_norm, bsz, seq, cl, valid_len):
    nc = seq // cl
    n = bsz * seq
    brow, bcol, arow, acol = vec
    cs0 = jnp.pad(conv0, ((0, 0), (8 - (CONV_K - 1), 0), (0, 0)))
    dexp = jnp.repeat(ssm_d, SSM_HEAD_DIM).reshape(1, SSM_D_INNER)
    expand = (jnp.arange(SSM_D_INNER)[None, :] // SSM_HEAD_DIM == jnp.arange(PS_W)[:, None] - PS_DT).astype(BF16)
    n_pairs, pw = SSM_HEADS // 2, 2 * SSM_HEAD_DIM
    h0p = h0.reshape(bsz, n_pairs, pw, SSM_STATE)
    const = lambda shp: BS(shp, lambda b, c: tuple(0 for _ in shp))
    y, h_last = pl.pallas_call(
        functools.partial(_ssd_body, cl=cl, nc=nc, valid_len=valid_len),
        out_shape=(S((n, SSM_D_INNER), BF16), S(h0p.shape, F32)), grid=(bsz, nc),
        in_specs=[BS((cl, SSM_D_INNER), lambda b, c: (b * nc + c, C_SZ // SSM_D_INNER)),
                  BS((cl, SSM_CONV_DIM), lambda b, c: (b * nc + c, C_XBC // SSM_CONV_DIM)),
                  BS((cl, PS_W), lambda b, c: (b * nc + c, 0)),
                  BS((None, 8, SSM_CONV_DIM), lambda b, c: (b, 0, 0)),
                  BS((None, n_pairs, pw, SSM_STATE), lambda b, c: (b, 0, 0, 0)),
                  const((CONV_K, SSM_CONV_DIM)), const((1, SSM_CONV_DIM)),
                  const((1, PS_W)), const((PS_W, 1)), const((1, PS_W)), const((PS_W, 1)),
                  const((1, SSM_D_INNER)), const((1, SSM_D_INNER)), const((PS_W, SSM_D_INNER))],
        out_specs=(BS((cl, SSM_D_INNER), lambda b, c: (b * nc + c, 0)),
                   BS((None, n_pairs, pw, SSM_STATE), lambda b, c: (b, 0, 0, 0))),
        scratch_shapes=[pltpu.VMEM((n_pairs, pw, SSM_STATE), F32), pltpu.VMEM((8, SSM_CONV_DIM), F32),
                        pltpu.VMEM((cl, SSM_D_INNER), F32)],
        compiler_params=_cparams(("parallel", "arbitrary")), name="ssd",
    )(proj, proj, ps, cs0, h0p, conv_w, conv_b.reshape(1, -1), brow, bcol, arow, acol, dexp,
      ssm_norm.reshape(1, SSM_D_INNER), expand)
    return y, h_last.reshape(h0.shape)


def _unit_lower_inverses(mats, ii, jj, cl):
    eye = (ii == jj).astype(F32)
    pair = ((ii >> 1) == (jj >> 1)) & (ii > jj)
    invs = [eye - jnp.where(pair, a, 0.0) for a in mats]
    s = 2
    while s < cl:
        blk = ((ii // (2 * s)) == (jj // (2 * s))) & ((ii % (2 * s)) >= s) & ((jj % (2 * s)) < s)
        xs = [_dot3(inv, jnp.where(blk, a, 0.0)) for inv, a in zip(invs, mats)]
        invs = [inv - _dot3(x, inv) for inv, x in zip(invs, xs)]
        s *= 2
    return invs


def _gdn_body(q_ref, k_ref, v_ref, z_ref, ps_ref, cs0_ref, s0_ref, cw_ref, brow_ref, bcol_ref, arow_ref, acol_ref,
              nw_ref, o_ref, sl_ref, st_ref, halo_ref, *, cl, cps, nc, valid_len):
    c = pl.program_id(1)

    @pl.when(c == 0)
    def _():
        st_ref[...] = s0_ref[...]
        halo_ref[...] = cs0_ref[...]

    w = DN_W
    rows = cl * cps
    halo = halo_ref[...]
    states = [st_ref[h] for h in range(DN_HEADS)]
    raws = (q_ref[...], k_ref[...], v_ref[...])
    q, k, v = (_silu(_causal_conv(raws[p], halo[:, p * w:(p + 1) * w], cw_ref, p * w, (p + 1) * w)) for p in range(3))
    z = z_ref[...]
    ps = ps_ref[...]
    g_c = -jnp.exp(arow_ref[...]) * _softplus(ps + brow_ref[...])
    g_r = -jnp.exp(acol_ref[...]) * _softplus(ps.T + bcol_ref[...])
    beta = _sigmoid(ps)
    if valid_len is not None:
        t_c = lax.broadcasted_iota(jnp.int32, g_c.shape, 0) + c * rows
        t_r = lax.broadcasted_iota(jnp.int32, g_r.shape, 1) + c * rows
        g_c = jnp.where(t_c < valid_len, g_c, 0.0)
        beta = jnp.where(t_c < valid_len, beta, 0.0)
        g_r = jnp.where(t_r < valid_len, g_r, 0.0)
    ii, jj = _tri(cl)
    incl = ii >= jj
    strict = ii > jj
    lower_ones = incl.astype(F32)
    upper_ones = (ii <= jj).astype(F32)
    off = PS_DECAY - 16
    nw = nw_ref[...]
    pre = []
    for ci in range(cps):
        rs = slice(ci * cl, (ci + 1) * cl)
        gcs_c = _dot_hi(lower_ones, g_c[rs, 16:24])
        gcs_r = _dot_hi(g_r[16:24, rs], upper_ones)
        for h in range(DN_HEADS):
            sl = slice(h * DN_DK, (h + 1) * DN_DK)
            qh = q[rs, sl]
            kh = k[rs, sl]
            qh = qh * lax.rsqrt(jnp.sum(qh * qh, axis=-1, keepdims=True) + EPS) * (DN_DK ** -0.5)
            kh = kh * lax.rsqrt(jnp.sum(kh * kh, axis=-1, keepdims=True) + EPS)
            bh = beta[rs, PS_BETA + h:PS_BETA + h + 1]
            col = gcs_c[:, off + h:off + h + 1]
            row = gcs_r[off + h:off + h + 1, :]
            last = gcs_c[cl - 1:cl, off + h:off + h + 1]
            e_col = jnp.exp(col)
            kb = kh * bh
            pre.append(dict(
                dec=jnp.where(incl, jnp.exp(jnp.where(incl, col - row, 0.0)), 0.0),
                q16=qh.astype(BF16), k16=kh.astype(BF16), kb16=kb.astype(BF16),
                vb=v[rs, sl] * bh, kbe=kb * e_col,
                qd16=(qh * e_col).astype(BF16),
                kd16=(kh * jnp.exp(last - col)).astype(BF16),
                g_end=jnp.exp(last),
                gate=nw * _silu(z[rs, sl])))
    lowers = [jnp.where(strict, _dot_nt(t["kb16"], t["k16"]) * t["dec"], 0.0) for t in pre]
    qks = [(_dot_nt(t["q16"], t["k16"]) * t["dec"]).astype(BF16) for t in pre]
    tinvs = _unit_lower_inverses(lowers, ii, jj, cl)
    us = [_dot3(ti, t["vb"]) for ti, t in zip(tinvs, pre)]
    ws = [_dot3(ti, t["kbe"]).astype(BF16) for ti, t in zip(tinvs, pre)]
    outs = []
    for ci in range(cps):
        items = range(ci * DN_HEADS, (ci + 1) * DN_HEADS)
        s16 = [st.astype(BF16) for st in states]
        v16 = [(us[i] - _dot(ws[i], s16[h])).astype(BF16) for h, i in enumerate(items)]
        os_ = [_dot(pre[i]["qd16"], s16[h]) + _dot(qks[i], v16[h]) for h, i in enumerate(items)]
        states = [states[h] * pre[i]["g_end"] + _dot_tn(pre[i]["kd16"], v16[h]) for h, i in enumerate(items)]
        outs += [o * lax.rsqrt(jnp.mean(o * o, axis=-1, keepdims=True) + EPS) * pre[i]["gate"]
                 for o, i in zip(os_, items)]
    for ci in range(cps):
        for h in range(DN_HEADS):
            o_ref[ci * cl:(ci + 1) * cl, h * DN_DV:(h + 1) * DN_DV] = outs[ci * DN_HEADS + h].astype(o_ref.dtype)
    for h in range(DN_HEADS):
        st_ref[h] = states[h]
    for p in range(3):
        halo_ref[:, p * w:(p + 1) * w] = raws[p][rows - 8:, :]

    @pl.when(c == nc - 1)
    def _():
        sl_ref[...] = st_ref[...]


def _gdn(proj, ps, conv0, s0, conv_w, vec, dn_norm, bsz, seq, cl, cps, valid_len):
    rows = cl * cps
    nc = seq // rows
    n = bsz * seq
    brow, bcol, arow, acol = vec
    cs0 = jnp.pad(conv0, ((0, 0), (8 - (CONV_K - 1), 0), (0, 0)))
    w = DN_W
    const = lambda shp: BS(shp, lambda b, c: tuple(0 for _ in shp))
    blk = lambda col: BS((rows, w), lambda b, c: (b * nc + c, col // w))
    o, s_last = pl.pallas_call(
        functools.partial(_gdn_body, cl=cl, cps=cps, nc=nc, valid_len=valid_len),
        out_shape=(S((n, w), BF16), S(s0.shape, F32)), grid=(bsz, nc),
        in_specs=[blk(C_DQ), blk(C_DK), blk(C_DV), blk(C_DZ),
                  BS((rows, PS_W), lambda b, c: (b * nc + c, 0)),
                  BS((None, 8, DN_CONV_DIM), lambda b, c: (b, 0, 0)),
                  BS((None, DN_HEADS, DN_DK, DN_DV), lambda b, c: (b, 0, 0, 0)),
                  const((CONV_K, DN_CONV_DIM)),
                  const((1, PS_W)), const((PS_W, 1)), const((1, PS_W)), const((PS_W, 1)), const((1, DN_DV))],
        out_specs=(BS((rows, w), lambda b, c: (b * nc + c, 0)),
                   BS((None, DN_HEADS, DN_DK, DN_DV), lambda b, c: (b, 0, 0, 0))),
        scratch_shapes=[pltpu.VMEM((DN_HEADS, DN_DK, DN_DV), F32), pltpu.VMEM((8, DN_CONV_DIM), F32)],
        compiler_params=_cparams(("parallel", "arbitrary")), name="gdn",
    )(proj, proj, proj, proj, ps, cs0, s0, conv_w, brow, bcol, arow, acol, dn_norm.reshape(1, DN_DV))
    return o, s_last


def _router_body(h_ref, lw_ref, rw_ref, rb_ref, meta_ref):
    x = h_ref[...]
    xn = x * lax.rsqrt(jnp.mean(x * x, axis=-1, keepdims=True) + EPS) * lw_ref[...]
    logits = _dot_hi(xn, rw_ref[...]) + rb_ref[...]
    lane = lax.broadcasted_iota(jnp.int32, logits.shape, 1)
    big = jnp.int32(logits.shape[1])
    m1 = jnp.max(logits, axis=1, keepdims=True)
    i1 = jnp.min(jnp.where(logits == m1, lane, big), axis=1, keepdims=True)
    rest = jnp.where(lane == i1, NEG, logits)
    m2 = jnp.max(rest, axis=1, keepdims=True)
    i2 = jnp.min(jnp.where(rest == m2, lane, big), axis=1, keepdims=True)
    e2 = jnp.exp(m2 - m1)
    g1 = 1.0 / (1.0 + e2)
    g2 = e2 * g1
    meta = jnp.where(lane == 0, i1.astype(F32), jnp.where(lane == 1, i2.astype(F32),
                     jnp.where(lane == 2, g1, jnp.where(lane == 3, g2, 0.0))))
    meta_ref[...] = meta


def _router(h, ln_w, router_w, router_b, tm):
    n, d = h.shape
    rw = jnp.pad(router_w, ((0, 0), (0, 128 - N_EXPERTS)))
    rb = jnp.pad(router_b, (0, 128 - N_EXPERTS), constant_values=NEG).reshape(1, 128)
    const = lambda shp: BS(shp, lambda i: (0, 0))
    return pl.pallas_call(
        _router_body, out_shape=S((n, 128), F32), grid=(n // tm,),
        in_specs=[BS((tm, d), lambda i: (i, 0)), const((1, d)), const((d, 128)), const((1, 128))],
        out_specs=BS((tm, 128), lambda i: (i, 0)),
        compiler_params=_cparams(("parallel",)), name="router")(h, ln_w.reshape(1, d), rw, rb)


def _experts_body(be_ref, tok_ref, nused_ref, h_hbm, lw_ref, wg_ref, wu_ref, wd_ref, y_ref,
                  xg_ref, xb_ref, acc_ref, sem, *, tm, nf):
    i = pl.program_id(0)
    f = pl.program_id(1)
    used = i < nused_ref[0]
    slot = lax.rem(i, 2)

    def gather(blk, dst):
        def start(r, carry):
            pltpu.make_async_copy(h_hbm.at[pl.ds(tok_ref[blk * tm + r], 1), :],
                                  xg_ref.at[dst, pl.ds(r, 1), :], sem.at[dst]).start()
            return carry

        lax.fori_loop(0, tm, start, 0, unroll=8)

    @pl.when((i == 0) & (f == 0))
    def _():
        gather(0, 0)

    @pl.when(used & (f == 0))
    def _():
        pltpu.make_async_copy(h_hbm.at[pl.ds(0, tm), :], xg_ref.at[slot], sem.at[slot]).wait()
        x = xg_ref[slot]
        xn = x * lax.rsqrt(jnp.mean(x * x, axis=-1, keepdims=True) + EPS) * lw_ref[...]
        xb_ref[...] = xn.astype(BF16)
        acc_ref[...] = jnp.zeros(acc_ref.shape, F32)

    @pl.when((f == 0) & (i + 1 < nused_ref[0]))
    def _():
        gather(i + 1, 1 - slot)

    @pl.when(used)
    def _():
        xb = xb_ref[...]
        hid = _silu(_dot(xb, wg_ref[...].astype(BF16))) * _dot(xb, wu_ref[...].astype(BF16))
        acc_ref[...] += _dot(hid.astype(BF16), wd_ref[...].astype(BF16))

    @pl.when(used & (f == nf - 1))
    def _():
        y_ref[...] = acc_ref[...]

    @pl.when(jnp.logical_not(used) & (f == nf - 1))
    def _():
        y_ref[...] = jnp.zeros(y_ref.shape, F32)


def _experts(h, ln_w, block_expert, row_token, n_used, wg, wu, wd, tm, tf):
    n, d = h.shape
    rows = row_token.shape[0]
    n_blocks = rows // tm
    (wg, layer), (wu, _), (wd, _) = wg, wu, wd
    ff = wg.shape[3]
    nf = ff // tf
    grid_spec = pltpu.PrefetchScalarGridSpec(
        num_scalar_prefetch=3, grid=(n_blocks, nf),
        in_specs=[BS(memory_space=pl.ANY),
                  BS((1, d), lambda i, f, be, tok, nu: (0, 0)),
                  BS((None, None, d, tf), lambda i, f, be, tok, nu: (layer, be[i], 0, f)),
                  BS((None, None, d, tf), lambda i, f, be, tok, nu: (layer, be[i], 0, f)),
                  BS((None, None, tf, d), lambda i, f, be, tok, nu: (layer, be[i], f, 0))],
        out_specs=BS((tm, d), lambda i, f, be, tok, nu: (i, 0)),
        scratch_shapes=[pltpu.VMEM((2, tm, d), F32), pltpu.VMEM((tm, d), BF16), pltpu.VMEM((tm, d), F32),
                        pltpu.SemaphoreType.DMA((2,))])
    return pl.pallas_call(
        functools.partial(_experts_body, tm=tm, nf=nf),
        out_shape=S((rows, d), F32), grid_spec=grid_spec,
        compiler_params=_cparams(("arbitrary", "arbitrary")), name="experts",
    )(block_expert, row_token, n_used, h, ln_w.reshape(1, d), wg, wu, wd)


def _combine_body(pos_ref, y_hbm, h_ref, meta_ref, o_ref, ya_ref, yb_ref, sem, *, tc, n_steps):
    i = pl.program_id(0)
    slot = lax.rem(i, 2)

    def gather(step, dst):
        def start(r, carry):
            t = step * tc + r
            pltpu.make_async_copy(y_hbm.at[pl.ds(pos_ref[2 * t], 1), :], ya_ref.at[dst, pl.ds(r, 1), :],
                                  sem.at[dst]).start()
            pltpu.make_async_copy(y_hbm.at[pl.ds(pos_ref[2 * t + 1], 1), :], yb_ref.at[dst, pl.ds(r, 1), :],
                                  sem.at[dst]).start()
            return carry

        lax.fori_loop(0, tc, start, 0, unroll=8)

    @pl.when(i == 0)
    def _():
        gather(0, 0)

    @pl.when(i + 1 < n_steps)
    def _():
        gather(i + 1, 1 - slot)

    pltpu.make_async_copy(y_hbm.at[pl.ds(0, tc), :], ya_ref.at[slot], sem.at[slot]).wait()
    pltpu.make_async_copy(y_hbm.at[pl.ds(0, tc), :], yb_ref.at[slot], sem.at[slot]).wait()
    meta = meta_ref[...]
    o_ref[...] = h_ref[...] + (meta[:, 2:3] * ya_ref[slot] + meta[:, 3:4] * yb_ref[slot])


def _combine(h, y_sorted, pos, meta, tc):
    n, d = h.shape
    n_steps = n // tc
    grid_spec = pltpu.PrefetchScalarGridSpec(
        num_scalar_prefetch=1, grid=(n_steps,),
        in_specs=[BS(memory_space=pl.ANY), BS((tc, d), lambda i, pos: (i, 0)), BS((tc, 128), lambda i, pos: (i, 0))],
        out_specs=BS((tc, d), lambda i, pos: (i, 0)),
        scratch_shapes=[pltpu.VMEM((2, tc, d), F32), pltpu.VMEM((2, tc, d), F32), pltpu.SemaphoreType.DMA((2,))])
    return pl.pallas_call(
        functools.partial(_combine_body, tc=tc, n_steps=n_steps), out_shape=S((n, d), F32), grid_spec=grid_spec,
        compiler_params=_cparams(("arbitrary",)), name="moe_combine")(pos, y_sorted, h, meta)


def _moe(h, ln_w, router_w, router_b, wg, wu, wd, tm_r, tm, tf, tc):
    n, d = h.shape
    meta = _router(h, ln_w, router_w, router_b, tm_r)
    idx = meta[:, 0:2].astype(jnp.int32)
    member = jnp.sum(jax.nn.one_hot(idx, N_EXPERTS, dtype=jnp.int32), axis=1)
    before = jnp.cumsum(member, axis=0) - member
    counts = jnp.sum(member, axis=0)
    padded = (counts + tm - 1) // tm * tm
    pad_ends = jnp.cumsum(padded)
    pad_starts = pad_ends - padded
    pos = (pad_starts[idx] + jnp.take_along_axis(before, idx, axis=1)).astype(jnp.int32)
    n_blocks = -(-(2 * n + N_EXPERTS * (tm - 1)) // tm)
    rows = n_blocks * tm
    tok = jnp.broadcast_to(jnp.arange(n, dtype=jnp.int32)[:, None], (n, 2))
    row_token = jnp.zeros((rows,), jnp.int32).at[pos.reshape(-1)].set(tok.reshape(-1))
    n_used = (pad_ends[-1] // tm).astype(jnp.int32).reshape(1)
    blk_start = jnp.minimum(jnp.arange(n_blocks, dtype=jnp.int32), n_used[0] - 1) * tm
    block_expert = jnp.minimum(jnp.searchsorted(pad_ends, blk_start, side="right"), N_EXPERTS - 1).astype(jnp.int32)
    y_sorted = _experts(h, ln_w, block_expert, row_token, n_used, wg, wu, wd, tm, tf)
    return _combine(h, y_sorted, pos.reshape(-1), meta, tc)


def _layer(i, h, p, lw, cfg, att_fn, ssm_conv0, ssm_h0, dn_conv0, dn_s0):
    bsz, seq, tm, cl_ssd, cl_gdn, pad_to, valid_len = (cfg[k] for k in
                                                      ("bsz", "seq", "tm", "cl_ssd", "cl_gdn", "pad_to", "valid_len"))
    n = bsz * seq
    xn = _rmsnorm(h, lw["ln_mix"], tm)
    ident = lambda d, e: d[0]
    proj = _fused_matmul([xn], [(0, lw["w_main"], 0)], ident, [], F32, PROJ_W, tm, 512, "in_proj")
    ps = _fused_matmul([xn], [(0, lw["w_small"], 0)], ident, [], F32, PS_W, tm, PS_W, "in_proj_small")

    q16, k32, k16, v16 = _qkv_prep(proj, lw["q_norm"], lw["k_norm"], tm)
    att = att_fn(q16, k16, v16)

    if pad_to is None:
        proj_r, ps_r, seq_r = proj, ps, seq
    else:
        seq_r = pad_to
        proj_r = jnp.pad(proj.reshape(bsz, seq, -1), ((0, 0), (0, pad_to - seq), (0, 0))).reshape(bsz * pad_to, -1)
        ps_r = jnp.pad(ps.reshape(bsz, seq, -1), ((0, 0), (0, pad_to - seq), (0, 0))).reshape(bsz * pad_to, -1)
    y, ssm_h1 = _ssd(proj_r, ps_r, ssm_conv0, ssm_h0, lw["ssm_conv_w"], lw["ssm_conv_b"], lw["vec"], lw["ssm_d"],
                     lw["ssm_norm"], bsz, seq_r, cl_ssd, valid_len)
    o, dn_s1 = _gdn(proj_r, ps_r, dn_conv0, dn_s0, lw["dn_conv_w"], lw["vec"], lw["dn_norm"], bsz, seq_r, cl_gdn,
                    cfg["cps_gdn"], valid_len)
    if pad_to is not None:
        y = y.reshape(bsz, pad_to, -1)[:, :seq].reshape(n, -1)
        o = o.reshape(bsz, pad_to, -1)[:, :seq].reshape(n, -1)

    def merge(d, e):
        return _sigmoid(d[0]) * d[3] + _sigmoid(d[1]) * d[4] + _sigmoid(d[2]) * d[5]

    d_m = D_MODEL
    merged = _fused_matmul([xn, att, y, o],
                           [(0, lw["w_gate"], 0), (0, lw["w_gate"], d_m), (0, lw["w_gate"], 2 * d_m),
                            (1, lw["w_up_att"], 0), (2, lw["w_up_ssm"], 0), (3, lw["w_up_dn"], 0)],
                           merge, [], BF16, d_m, tm, 512, "merge")
    resid = lambda d, e: e[0] + d[0]
    h = _fused_matmul([merged], [(0, lw["w_out"], 0)], resid, [h], F32, d_m, tm, 512, "out_proj")

    if i % 2 == 0:
        hn = _rmsnorm(h, lw["ln_ffn"], tm)
        d_ff = lw["d_ff"]
        ff = _fused_matmul([hn], [(0, lw["ffn_w_gate"], 0), (0, lw["ffn_w_up"], 0)],
                           lambda d, e: _silu(d[0]) * d[1], [], BF16, d_ff, min(tm, 512), 1408, "ffn_up")
        h = _fused_matmul([ff], [(0, lw["ffn_w_down"], 0)], resid, [h], F32, d_m, min(tm, 512), 512, "ffn_down")
    else:
        h = _moe(h, lw["ln_ffn"], lw["router_w"], lw["router_b"], lw["exp_w_gate"], lw["exp_w_up"], lw["exp_w_down"],
                 min(tm, 512), cfg["tm_moe"], TF_MOE, cfg["tc"])

    hn = _rmsnorm(h, lw["ln_ple"], tm)
    h = _fused_matmul([hn, p.astype(BF16)], [(0, lw["ple_w_gate"], 0), (1, lw["ple_w_proj"], 0)],
                      lambda d, e: e[0] + _sigmoid(d[0]) * d[1], [h], F32, d_m, tm, 512, "ple")

    xbc_raw = proj[:, C_XBC:C_XBC + SSM_CONV_DIM].reshape(bsz, seq, -1)
    dn_raw = proj[:, C_DQ:C_DQ + DN_CONV_DIM].reshape(bsz, seq, -1)
    keep = CONV_K - 1
    if seq >= keep:
        ssm_conv1, dn_conv1 = xbc_raw[:, seq - keep:], dn_raw[:, seq - keep:]
    else:
        ssm_conv1 = jnp.concatenate([ssm_conv0, xbc_raw], axis=1)[:, -keep:]
        dn_conv1 = jnp.concatenate([dn_conv0, dn_raw], axis=1)[:, -keep:]
    new_k = k32.reshape(bsz, seq, ATT_HEADS, 2 * ATT_DK)
    new_v = proj[:, C_AV:C_AV + ATT_W].reshape(bsz, seq, ATT_HEADS, ATT_DV)
    return h, (new_k, new_v, ssm_conv1, ssm_h1, dn_conv1, dn_s1)


def _prep_layer_weights(i, W):
    w_in = W["w_in"][i]
    o = 0
    cols = {}
    for name, width in (("aq", 512), ("ak", 512), ("av", 512), ("sz", 1024), ("xbc", 1536), ("dt", 16),
                        ("dq", 512), ("dk", 512), ("dv", 512), ("dz", 512), ("beta", 4), ("decay", 4)):
        cols[name] = w_in[:, o:o + width]
        o += width
    w_main = jnp.concatenate([cols[k] for k in ("xbc", "aq", "sz", "ak", "av", "dq", "dk", "dv", "dz")], axis=1)
    w_small = jnp.concatenate([cols["dt"], cols["beta"], cols["decay"],
                               jnp.zeros((D_MODEL, PS_W - 24), w_in.dtype)], axis=1)
    bias_row = jnp.zeros((PS_W,), F32).at[PS_DT:PS_DT + SSM_HEADS].set(W["ssm_dt_bias"][i])
    bias_row = bias_row.at[PS_DECAY:PS_DECAY + DN_HEADS].set(W["dn_dt_bias"][i])
    alog_row = jnp.zeros((PS_W,), F32).at[PS_DT:PS_DT + SSM_HEADS].set(W["ssm_a_log"][i])
    alog_row = alog_row.at[PS_DECAY:PS_DECAY + DN_HEADS].set(W["dn_a_log"][i])
    vec = (bias_row.reshape(1, PS_W), bias_row.reshape(PS_W, 1), alog_row.reshape(1, PS_W), alog_row.reshape(PS_W, 1))
    lw = {
        "ln_mix": W["ln_mix"][i], "w_main": w_main.astype(BF16), "w_small": w_small.astype(BF16),
        "q_norm": W["q_norm"][i], "k_norm": W["k_norm"][i], "att_subln": W["att_subln"][i],
        "ssm_conv_w": W["ssm_conv_w"][i], "ssm_conv_b": W["ssm_conv_b"][i], "vec": vec,
        "ssm_d": W["ssm_d"][i], "ssm_norm": W["ssm_norm"][i],
        "dn_conv_w": W["dn_conv_w"][i], "dn_norm": W["dn_norm"][i],
        "w_gate": (W["w_gate"], i), "w_up_att": (W["w_up_att"], i), "w_up_ssm": (W["w_up_ssm"], i),
        "w_up_dn": (W["w_up_dn"], i), "w_out": (W["w_out"], i),
        "ln_ffn": W["ln_ffn"][i], "ln_ple": W["ln_ple"][i],
        "ple_w_gate": (W["ple_w_gate"], i), "ple_w_proj": (W["ple_w_proj"], i),
    }
    if i % 2 == 0:
        lw.update(ffn_w_gate=(W["ffn_w_gate"], i // 2), ffn_w_up=(W["ffn_w_up"], i // 2),
                  ffn_w_down=(W["ffn_w_down"], i // 2), d_ff=W["ffn_w_gate"].shape[2])
    else:
        lw.update(router_w=W["router_w"][i // 2], router_b=W["router_b"][i // 2],
                  exp_w_gate=(W["exp_w_gate"], i // 2), exp_w_up=(W["exp_w_up"], i // 2),
                  exp_w_down=(W["exp_w_down"], i // 2))
    lam_init = 0.8 - 0.6 * math.exp(-0.3 * i)
    lam = (jnp.exp(jnp.sum(W["lam_q1"][i] * W["lam_k1"][i])) - jnp.exp(jnp.sum(W["lam_q2"][i] * W["lam_k2"][i]))
           + lam_init).astype(F32)
    return lw, lam, lam_init


def _pick(n, pref):
    t = min(n, pref)
    while n % t:
        t //= 2
    return t


def kernel(x_prompt, x_sample, cache_k, cache_v, state_ssm_conv, state_ssm, state_dn_conv, state_dn, page_table, p_prompt, p_sample, ln_mix, w_in, w_gate, q_norm, k_norm, lam_q1, lam_k1, lam_q2, lam_k2, att_subln, rel_bias, ssm_conv_w, ssm_conv_b, ssm_dt_bias, ssm_a_log, ssm_d, ssm_norm, dn_conv_w, dn_dt_bias, dn_a_log, dn_norm, w_up_att, w_up_ssm, w_up_dn, w_out, ln_ffn, ffn_w_gate, ffn_w_up, ffn_w_down, router_w, router_b, exp_w_gate, exp_w_up, exp_w_down, ln_ple, ple_w_gate, ple_w_proj):
    W = dict(ln_mix=ln_mix, w_in=w_in, w_gate=w_gate, q_norm=q_norm, k_norm=k_norm, lam_q1=lam_q1, lam_k1=lam_k1,
             lam_q2=lam_q2, lam_k2=lam_k2, att_subln=att_subln, ssm_conv_w=ssm_conv_w, ssm_conv_b=ssm_conv_b,
             ssm_dt_bias=ssm_dt_bias, ssm_a_log=ssm_a_log, ssm_d=ssm_d, ssm_norm=ssm_norm, dn_conv_w=dn_conv_w,
             dn_dt_bias=dn_dt_bias, dn_a_log=dn_a_log, dn_norm=dn_norm, w_up_att=w_up_att, w_up_ssm=w_up_ssm,
             w_up_dn=w_up_dn, w_out=w_out, ln_ffn=ln_ffn, ffn_w_gate=ffn_w_gate, ffn_w_up=ffn_w_up,
             ffn_w_down=ffn_w_down, router_w=router_w, router_b=router_b, exp_w_gate=exp_w_gate, exp_w_up=exp_w_up,
             exp_w_down=exp_w_down, ln_ple=ln_ple, ple_w_gate=ple_w_gate, ple_w_proj=ple_w_proj)
    depth = ln_mix.shape[0]
    bp, lp, d = x_prompt.shape
    bs, ls, _ = x_sample.shape
    n_pages = page_table.shape[1]
    t_att = _pick(lp, T_ATT)
    cfg_p = dict(bsz=bp, seq=lp, tm=_pick(bp * lp, TM), cl_ssd=_pick(lp, CL_SSD), cl_gdn=_pick(lp, CL_GDN), pad_to=None,
                 valid_len=None, tm_moe=_pick(bp * lp, TM_MOE), tc=_pick(bp * lp, TC_MOE))
    cfg_p["cps_gdn"] = _pick(lp // cfg_p["cl_gdn"], CPS_GDN)
    cfg_s = dict(bsz=bs, seq=ls, tm=bs * ls, cl_ssd=CL_SAMPLE, cl_gdn=CL_SAMPLE, cps_gdn=1, pad_to=CL_SAMPLE,
                 valid_len=ls,
                 tm_moe=64, tc=bs * ls)
    hp = x_prompt.reshape(bp * lp, d)
    hs = x_sample.reshape(bs * ls, d)
    st_p, st_s = [], []
    tabs = _bias_tables(rel_bias, t_att)
    for i in range(depth):
        lw, lam, lam_init = _prep_layer_weights(i, W)
        post = 1.0 - lam_init
        att_p = lambda q, k, v: _attn_prompt(q, k, v, lam, rel_bias, tabs, lw["att_subln"], post, bp, lp, t_att)
        hp, st = _layer(i, hp, p_prompt[i].reshape(bp * lp, -1), lw, cfg_p, att_p,
                        jnp.zeros((bp, CONV_K - 1, SSM_CONV_DIM), F32),
                        jnp.zeros((bp, SSM_HEADS, SSM_HEAD_DIM, SSM_STATE), F32),
                        jnp.zeros((bp, CONV_K - 1, DN_CONV_DIM), F32),
                        jnp.zeros((bp, DN_HEADS, DN_DK, DN_DV), F32))
        st_p.append(st)
        att_s = lambda q, k, v: _attn_decode(i, q, k, v, cache_k, cache_v, page_table, lam, rel_bias,
                                             lw["att_subln"], post, _pick(n_pages, PAGES_PER_STEP))
        hs, st = _layer(i, hs, p_sample[i].reshape(bs * ls, -1), lw, cfg_s, att_s,
                        state_ssm_conv[i], state_ssm[i], state_dn_conv[i], state_dn[i])
        st_s.append(st)
    stk = lambda sts, j: jnp.stack([s[j] for s in sts])
    return (hp.reshape(bp, lp, d), hs.reshape(bs, ls, d),
            stk(st_p, 0), stk(st_p, 1), stk(st_p, 2), stk(st_p, 3), stk(st_p, 4), stk(st_p, 5),
            stk(st_s, 0), stk(st_s, 1), stk(st_s, 2), stk(st_s, 3), stk(st_s, 4), stk(st_s, 5))
```

```python
import functools
import math

import jax
import jax.numpy as jnp
from jax import lax
from jax.experimental import pallas as pl
from jax.experimental.pallas import tpu as pltpu

F32 = jnp.float32
BF16 = jnp.bfloat16
S = jax.ShapeDtypeStruct
BS = pl.BlockSpec

D_MODEL = 1024
ATT_HEADS = 4
ATT_DK = 64
ATT_DV = 128
ATT_W = ATT_HEADS * ATT_DV
NUM_BUCKETS = 32
MAX_DISTANCE = 128
SSM_HEADS = 16
SSM_HEAD_DIM = 64
SSM_GROUPS = 2
SSM_STATE = 128
SSM_D_INNER = 1024
SSM_CONV_DIM = 1536
DN_HEADS = 4
DN_DK = 128
DN_DV = 128
DN_W = 512
DN_CONV_DIM = 1536
CONV_K = 4
N_EXPERTS = 8
EPS = 1e-6
NEG = -1e30
LOG2E = 1.4426950408889634

PROJ_A_W, PROJ_A_ROT = 4096, 2560
C_XBC, C_AQ, C_AK, C_AV, C_SZ = 0, 1536, 2048, 2560, 3072
PROJ_B_W = 2048
C_DQ, C_DK, C_DV, C_DZ = 0, 512, 1024, 1536
PS_DT, PS_BETA, PS_DECAY, PS_W = 0, 16, 20, 128

VMEM_LIMIT = 56 * 1024 * 1024
T_ATT = 512
TM = 1024
CL_SSD = 128
CL_GDN = 64
CPS_GDN = 2
CL_SAMPLE = 16
PAGES_PER_STEP = 8
TM_MOE = 512
TF_MOE = 896
TC_MOE = 256


def _cparams(sem):
    return pltpu.CompilerParams(dimension_semantics=sem, vmem_limit_bytes=VMEM_LIMIT)


def _dot(a, b):
    return jnp.dot(a, b, preferred_element_type=F32)


def _dot_nt(a, b):
    return lax.dot_general(a, b, (((1,), (1,)), ((), ())), preferred_element_type=F32)


def _dot_tn(a, b):
    return lax.dot_general(a, b, (((0,), (0,)), ((), ())), preferred_element_type=F32)


def _dot_hi(a, b):
    return jnp.dot(a, b, preferred_element_type=F32, precision=lax.Precision.HIGHEST)


def _split(a):
    hi = a.astype(BF16)
    lo = (a - hi.astype(F32)).astype(BF16)
    return hi, lo


def _dot3(a, b):
    ah, al = _split(a)
    bh, bl = _split(b)
    return _dot(ah, bh) + (_dot(ah, bl) + _dot(al, bh))


def _sigmoid(x):
    return 1.0 / (1.0 + jnp.exp(-x))


def _silu(x):
    return x * _sigmoid(x)


def _softplus(x):
    return jnp.maximum(x, 0.0) + jnp.log1p(jnp.exp(-jnp.abs(x)))


def _rmsnorm_body(x_ref, w_ref, o_ref):
    x = x_ref[...]
    y = x * lax.rsqrt(jnp.mean(x * x, axis=-1, keepdims=True) + EPS) * w_ref[...]
    o_ref[...] = y.astype(o_ref.dtype)


def _rmsnorm(x, w, tm):
    n, d = x.shape
    return pl.pallas_call(
        _rmsnorm_body, out_shape=S((n, d), BF16), grid=(n // tm,),
        in_specs=[BS((tm, d), lambda i: (i, 0)), BS((1, d), lambda i: (0, 0))],
        out_specs=BS((tm, d), lambda i: (i, 0)),
        compiler_params=_cparams(("parallel",)), name="rmsnorm")(x, w.reshape(1, d))


def _fused_matmul(acts, pairs, combine, extras, out_dtype, m, tm, tn, name, out_tile=None):
    n = acts[0].shape[0]
    na, npair, nex = len(acts), len(pairs), len(extras)
    act_idx = [a for a, _, _ in pairs]

    def body(*refs):
        a_vals = [r[...] for r in refs[:na]]
        dots = [_dot(a_vals[act_idx[j]], refs[na + j][...].astype(BF16)) for j in range(npair)]
        ex = [r[...] for r in refs[na + npair:na + npair + nex]]
        o_ref = refs[na + npair + nex]
        o_ref[...] = combine(dots, ex).astype(o_ref.dtype)

    def w_spec(w, c0):
        off = c0 // tn
        if isinstance(w, tuple):
            arr, layer = w
            return BS((None, arr.shape[1], tn), lambda i, j: (layer, 0, j + off))
        return BS((w.shape[0], tn), lambda i, j: (0, j + off))

    in_specs = ([BS((tm, a.shape[1]), lambda i, j: (i, 0)) for a in acts]
                + [w_spec(w, c0) for _, w, c0 in pairs]
                + [BS((tm, tn), lambda i, j: (i, j)) for _ in extras])
    out_tile = out_tile or (lambda j: j)
    return pl.pallas_call(
        body, out_shape=S((n, m), out_dtype), grid=(n // tm, m // tn),
        in_specs=in_specs, out_specs=BS((tm, tn), lambda i, j: (i, out_tile(j))),
        compiler_params=_cparams(("parallel", "arbitrary")), name=name,
    )(*acts, *[w[0] if isinstance(w, tuple) else w for _, w, _ in pairs], *extras)


def _qkv_body(aq_ref, ak_ref, av_ref, g_ref, qw_ref, kw_ref, q_ref, kb_ref, vb_ref, kf_ref, vf_ref):
    g = g_ref[...]

    def gnorm(x, w):
        hi, lo = _split(x * x)
        ss = _dot(hi, g) + _dot(lo, g)
        return x * lax.rsqrt(ss * (1.0 / ATT_DK) + EPS) * w

    q = gnorm(aq_ref[...], qw_ref[...]) * (ATT_DK ** -0.5 * LOG2E)
    k = gnorm(ak_ref[...], kw_ref[...])
    v = av_ref[...]
    q_ref[...] = q.astype(BF16)
    kb_ref[...] = k.astype(BF16)
    vb_ref[...] = v.astype(BF16)
    for h in range(ATT_HEADS):
        kf_ref[:, h, :] = k[:, h * ATT_DV:(h + 1) * ATT_DV]
        vf_ref[:, h, :] = v[:, h * ATT_DV:(h + 1) * ATT_DV]


def _qkv_prep(proj, q_norm, k_norm, tm):
    n = proj.shape[0]
    w = ATT_W
    gi = jnp.arange(w) // ATT_DK
    gmat = (gi[:, None] == gi[None, :]).astype(BF16)
    qw = jnp.tile(q_norm, w // ATT_DK).reshape(1, w)
    kw = jnp.tile(k_norm, w // ATT_DK).reshape(1, w)
    row = lambda c: BS((tm, w), lambda i: (i, c))
    const = lambda shp: BS(shp, lambda i: (0, 0))
    thd = BS((tm, ATT_HEADS, ATT_DV), lambda i: (i, 0, 0))
    return pl.pallas_call(
        _qkv_body,
        out_shape=(S((n, w), BF16), S((n, w), BF16), S((n, w), BF16),
                   S((n, ATT_HEADS, ATT_DV), F32), S((n, ATT_HEADS, ATT_DV), F32)),
        grid=(n // tm,),
        in_specs=[row(C_AQ // w), row(C_AK // w), row(C_AV // w), const((w, w)), const((1, w)), const((1, w))],
        out_specs=(row(0), row(0), row(0), thd, thd),
        compiler_params=_cparams(("parallel",)), name="qkv_prep")(proj, proj, proj, gmat, qw, kw)


def _t5_bucket(n):
    max_exact = NUM_BUCKETS // 2
    scaled = jnp.log(jnp.maximum(n, 1).astype(F32) / max_exact) / math.log(MAX_DISTANCE / max_exact)
    large = jnp.minimum(max_exact + (scaled * (NUM_BUCKETS - max_exact)).astype(jnp.int32), NUM_BUCKETS - 1)
    return jnp.where(n < max_exact, n, large)


def _bias_by_distance(rel_bias, n):
    return jnp.moveaxis(rel_bias[_t5_bucket(n)], -1, 0).astype(F32)


def _bias_table_body(rb_ref, o_ref, *, t, rows):
    h = pl.program_id(0)
    max_exact = NUM_BUCKETS // 2

    def block(i, carry):
        r0 = pl.multiple_of(i * rows, rows)
        c = lax.broadcasted_iota(jnp.int32, (rows, t), 0) + r0
        r = lax.broadcasted_iota(jnp.int32, (rows, t), 1)
        for idx in range(2):
            n = r - c + idx * t
            nn = jnp.maximum(n, 0)
            scaled = jnp.log(jnp.maximum(nn, 1).astype(F32) / max_exact) / math.log(MAX_DISTANCE / max_exact)
            large = jnp.minimum(max_exact + (scaled * (NUM_BUCKETS - max_exact)).astype(jnp.int32), NUM_BUCKETS - 1)
            bucket = jnp.where(nn < max_exact, nn, large)
            val = jnp.zeros((rows, t), F32)
            for b in range(NUM_BUCKETS):
                val = jnp.where(bucket == b, rb_ref[b, h] * LOG2E, val)
            if idx == 0:
                val = jnp.where(n >= 0, val, NEG)
            o_ref[idx, pl.ds(r0, rows), :] = val
        return carry

    lax.fori_loop(0, t // rows, block, 0)


def _bias_tables(rel_bias, t):
    return pl.pallas_call(
        functools.partial(_bias_table_body, t=t, rows=8),
        out_shape=S((ATT_HEADS, 2, t, t), F32), grid=(ATT_HEADS,),
        in_specs=[BS(memory_space=pltpu.SMEM)],
        out_specs=BS((None, 2, t, t), lambda h: (h, 0, 0, 0)),
        compiler_params=_cparams(("parallel",)), name="bias_tables")(rel_bias.astype(F32))


def _attn_body(lam_ref, cfar_ref, q_ref, k_ref, vt_ref, tab_ref, sw_ref, o_ref,
               q2_ref, s_ref, sb_ref, p_ref, m_ref, l_ref, acc_ref, *, t, rb, post_scale):
    h = pl.program_id(1)
    qi = pl.program_id(2)
    q = q_ref[...]
    lane = lax.broadcasted_iota(jnp.int32, q.shape, 1)
    zero = jnp.zeros_like(q)
    q2_ref[0:t, :] = jnp.where(lane < ATT_DK, q, zero)
    q2_ref[t:2 * t, :] = jnp.where(lane >= ATT_DK, q, zero)
    m_ref[...] = jnp.full(m_ref.shape, -jnp.inf, F32)
    l_ref[...] = jnp.zeros(l_ref.shape, F32)
    acc_ref[...] = jnp.zeros(acc_ref.shape, F32)
    cfar = cfar_ref[h] * LOG2E
    w2 = 2 * t

    def tile(ki, tab_idx):
        s_ref[...] = _dot_nt(k_ref[pl.ds(pl.multiple_of(ki * t, t), t), :], q2_ref[...])
        src_ref = s_ref if tab_idx is None else sb_ref

        def key_max(g, mx):
            rows = pl.ds(pl.multiple_of(g * rb, rb), rb)
            s = s_ref[rows, :]
            if tab_idx is not None:
                tab = tab_ref[tab_idx, rows, :]
                s = s + jnp.concatenate([tab, tab], axis=1)
                sb_ref[rows, :] = s
            for a in range(0, rb, 8):
                mx = jnp.maximum(mx, s[a:a + 8, :])
            return mx

        mx8 = lax.fori_loop(0, t // rb, key_max, jnp.full((8, w2), -jnp.inf, F32), unroll=4)
        tile_max = jnp.max(mx8, axis=0, keepdims=True)
        if tab_idx is None:
            tile_max = tile_max + cfar
        m_old = m_ref[...]
        m_new = jnp.maximum(m_old, tile_max)
        alpha = jnp.exp2(m_old - m_new)
        m_ref[...] = m_new
        shift = jnp.broadcast_to(m_new if tab_idx is not None else m_new - cfar, (rb, w2))

        def key_exp(g, tot):
            rows = pl.ds(pl.multiple_of(g * rb, rb), rb)
            p = jnp.exp2(src_ref[rows, :] - shift)
            p_ref[rows, :] = p.astype(BF16)
            for a in range(0, rb, 8):
                tot = tot + p[a:a + 8, :]
            return tot

        tot8 = lax.fori_loop(0, t // rb, key_exp, jnp.zeros((8, w2), F32), unroll=4)
        l_ref[...] = alpha * l_ref[...] + jnp.sum(tot8, axis=0, keepdims=True)
        acc_ref[...] = alpha * acc_ref[...] + _dot(vt_ref[ki], p_ref[...])

    def far(ki, carry):
        tile(ki, None)
        return carry

    lax.fori_loop(0, jnp.maximum(qi - 1, 0), far, 0)

    @pl.when(qi >= 1)
    def _():
        tile(qi - 1, 1)

    tile(qi, 0)
    ot = acc_ref[...] * (1.0 / l_ref[...])
    o = (ot[:, 0:t] - lam_ref[0] * ot[:, t:w2]).T
    o = o * lax.rsqrt(jnp.mean(o * o, axis=-1, keepdims=True) + EPS) * sw_ref[...] * post_scale
    o_ref[...] = o.astype(o_ref.dtype)


def _attn_prompt(q, k, v, lam, rel_bias, tabs, subln, post_scale, bsz, seq, t):
    nq = seq // t
    cfar = rel_bias[NUM_BUCKETS - 1].astype(F32)
    vt = v.reshape(bsz, nq, t, ATT_HEADS, ATT_DV).transpose(0, 3, 1, 4, 2).reshape(bsz * ATT_HEADS, nq, ATT_DV, t)
    smem = BS(memory_space=pltpu.SMEM)
    return pl.pallas_call(
        functools.partial(_attn_body, t=t, rb=16, post_scale=post_scale),
        out_shape=S((bsz * seq, ATT_W), BF16), grid=(bsz, ATT_HEADS, nq),
        in_specs=[smem, smem,
                  BS((t, ATT_DV), lambda b, h, qi: (b * nq + qi, h)),
                  BS((seq, ATT_DV), lambda b, h, qi: (b, h)),
                  BS((None, nq, ATT_DV, t), lambda b, h, qi: (b * ATT_HEADS + h, 0, 0, 0)),
                  BS((None, 2, t, t), lambda b, h, qi: (h, 0, 0, 0)),
                  BS((1, ATT_DV), lambda b, h, qi: (0, 0))],
        out_specs=BS((t, ATT_DV), lambda b, h, qi: (b * nq + qi, h)),
        scratch_shapes=[pltpu.VMEM((2 * t, ATT_DV), BF16), pltpu.VMEM((t, 2 * t), F32), pltpu.VMEM((t, 2 * t), F32),
                        pltpu.VMEM((t, 2 * t), BF16), pltpu.VMEM((1, 2 * t), F32), pltpu.VMEM((1, 2 * t), F32),
                        pltpu.VMEM((ATT_DV, 2 * t), F32)],
        compiler_params=_cparams(("parallel", "parallel", "arbitrary")), name="attn_prompt",
    )(lam.reshape(1), cfar, q, k, vt, tabs, subln.reshape(1, ATT_DV))


def _decode_body(pt_ref, lam_ref, q8_ref, kn_ref, vn_ref, btab_ref, sw_ref, *rest, pp, n_steps, post_scale):
    k_refs = rest[:pp]
    v_refs = rest[pp:2 * pp]
    o_ref = rest[2 * pp]
    m_ref, l_ref, acc_ref = rest[2 * pp + 1:]
    s_idx = pl.program_id(1)

    @pl.when(s_idx == 0)
    def _():
        m_ref[...] = jnp.full(m_ref.shape, -jnp.inf, F32)
        l_ref[...] = jnp.zeros(l_ref.shape, F32)
        acc_ref[...] = jnp.zeros(acc_ref.shape, F32)

    q8 = q8_ref[...]
    n_pages = n_steps * pp
    scores = []
    for j in range(pp):
        page = s_idx * pp + j
        bias = jnp.where(page == n_pages - 1, btab_ref[0], btab_ref[1])
        scores.append(_dot_nt(q8, k_refs[j][...].astype(BF16)) + bias)
    s_all = jnp.concatenate(scores, axis=1)
    m_old = m_ref[...]
    m_new = jnp.maximum(m_old, jnp.max(s_all, axis=1, keepdims=True))
    alpha = jnp.exp2(m_old - m_new)
    p_all = jnp.exp2(s_all - m_new)
    l_ref[...] = alpha * l_ref[...] + jnp.sum(p_all, axis=1, keepdims=True)
    ps = p_all.shape[1] // pp
    p16 = p_all.astype(BF16)
    pv = _dot(p16[:, :ps], v_refs[0][...].astype(BF16))
    for j in range(1, pp):
        pv = pv + _dot(p16[:, j * ps:(j + 1) * ps], v_refs[j][...].astype(BF16))
    acc_ref[...] = alpha * acc_ref[...] + pv
    m_ref[...] = m_new

    @pl.when(s_idx == n_steps - 1)
    def _():
        s_self = (jnp.sum(q8.astype(F32) * kn_ref[...].astype(F32), axis=1, keepdims=True)
                  + btab_ref[2][:, 0:1])
        m_o = m_ref[...]
        m_f = jnp.maximum(m_o, s_self)
        a_o = jnp.exp2(m_o - m_f)
        p_s = jnp.exp2(s_self - m_f)
        l_f = a_o * l_ref[...] + p_s
        acc = a_o * acc_ref[...] + p_s * vn_ref[...].astype(F32)
        o8 = acc * (1.0 / l_f)
        lam = lam_ref[0]
        outs = []
        for h in range(ATT_HEADS):
            o = o8[2 * h:2 * h + 1, :] - lam * o8[2 * h + 1:2 * h + 2, :]
            o = o * lax.rsqrt(jnp.mean(o * o, axis=-1, keepdims=True) + EPS) * sw_ref[...] * post_scale
            outs.append(o)
        o_ref[...] = jnp.concatenate(outs, axis=1).astype(o_ref.dtype)


def _attn_decode(layer, q, k_new, v_new, cache_k, cache_v, page_table, lam, rel_bias, subln, post_scale, pp):
    bsz = q.shape[0]
    depth, n_phys, page = cache_k.shape[0], cache_k.shape[1], cache_k.shape[2]
    n_pages = page_table.shape[1]
    n_steps = n_pages // pp
    prow = page * ATT_HEADS
    ck = cache_k.reshape(depth, n_phys, prow, ATT_DV)
    cv = cache_v.reshape(depth, n_phys, prow, ATT_DV)
    rows = jnp.arange(2 * ATT_HEADS)
    qh = jnp.repeat(q.reshape(bsz, ATT_HEADS, ATT_DV), 2, axis=1)
    q8 = jnp.where((jnp.arange(ATT_DV)[None, :] // ATT_DK == rows[:, None] % 2)[None], qh, jnp.zeros((), q.dtype))
    kn8 = jnp.repeat(k_new.reshape(bsz, ATT_HEADS, ATT_DV), 2, axis=1)
    vn8 = jnp.repeat(v_new.reshape(bsz, ATT_HEADS, ATT_DV), 2, axis=1)
    same = (jnp.arange(prow)[None, :] % ATT_HEADS) == (rows[:, None] // 2)
    d_last = page - jnp.arange(prow) // ATT_HEADS
    b_last = jnp.repeat(_bias_by_distance(rel_bias, d_last), 2, axis=0)
    b_far = jnp.broadcast_to(jnp.repeat(rel_bias[NUM_BUCKETS - 1].astype(F32), 2)[:, None], (2 * ATT_HEADS, prow))
    b_self = jnp.broadcast_to(jnp.repeat(rel_bias[0].astype(F32), 2)[:, None], (2 * ATT_HEADS, prow))
    btab = jnp.stack([jnp.where(same, b_last * LOG2E, NEG), jnp.where(same, b_far * LOG2E, NEG), b_self * LOG2E])

    def page_spec(j):
        return BS((None, None, prow, ATT_DV), lambda b, s, pt: (layer, pt[b, s * pp + j], 0, 0))

    const = lambda shp: BS(shp, lambda b, s, pt: tuple(0 for _ in shp))
    per_b = BS((None, 2 * ATT_HEADS, ATT_DV), lambda b, s, pt: (b, 0, 0))
    grid_spec = pltpu.PrefetchScalarGridSpec(
        num_scalar_prefetch=1, grid=(bsz, n_steps),
        in_specs=[BS(memory_space=pltpu.SMEM), per_b, per_b, per_b,
                  const((3, 2 * ATT_HEADS, prow)), const((1, ATT_DV))]
        + [page_spec(j) for j in range(pp)] + [page_spec(j) for j in range(pp)],
        out_specs=BS((None, 1, ATT_W), lambda b, s, pt: (b, 0, 0)),
        scratch_shapes=[pltpu.VMEM((2 * ATT_HEADS, 1), F32), pltpu.VMEM((2 * ATT_HEADS, 1), F32),
                        pltpu.VMEM((2 * ATT_HEADS, ATT_DV), F32)])
    out = pl.pallas_call(
        functools.partial(_decode_body, pp=pp, n_steps=n_steps, post_scale=post_scale),
        out_shape=S((bsz, 1, ATT_W), BF16), grid_spec=grid_spec,
        compiler_params=_cparams(("parallel", "arbitrary")), name="attn_decode",
    )(page_table, lam.reshape(1), q8, kn8, vn8, btab, subln.reshape(1, ATT_DV), *([ck] * pp), *([cv] * pp))
    return out.reshape(bsz, ATT_W)


def _causal_conv(x, prev8, cw_ref, c0, c1):
    acc = x * cw_ref[CONV_K - 1:CONV_K, c0:c1]
    row8 = lax.broadcasted_iota(jnp.int32, prev8.shape, 0)
    for s in range(1, CONV_K):
        r = pltpu.roll(x, s, axis=0)
        pr = pltpu.roll(prev8, s, axis=0)
        head = jnp.where(row8 < s, pr, r[:8])
        sh = jnp.concatenate([head, r[8:]], axis=0) if x.shape[0] > 8 else head
        acc = acc + sh * cw_ref[CONV_K - 1 - s:CONV_K - s, c0:c1]
    return acc


def _tri(cl):
    ii = lax.broadcasted_iota(jnp.int32, (cl, cl), 0)
    jj = lax.broadcasted_iota(jnp.int32, (cl, cl), 1)
    return ii, jj


def _ssd_body(z_ref, xbc_ref, ps_ref, cs0_ref, h0_ref, cw_ref, cb_ref, brow_ref, bcol_ref, arow_ref, acol_ref,
              dexp_ref, nw_ref, ex_ref, y_ref, hl_ref, st_ref, halo_ref, yacc_ref, *, cl, nc, valid_len):
    c = pl.program_id(1)

    @pl.when(c == 0)
    def _():
        st_ref[...] = h0_ref[...]
        halo_ref[...] = cs0_ref[...]

    n_pairs = SSM_HEADS // 2
    pw = 2 * SSM_HEAD_DIM
    states = [st_ref[j] for j in range(n_pairs)]
    x_raw = xbc_ref[...]
    conv = _causal_conv(x_raw, halo_ref[...], cw_ref, 0, SSM_CONV_DIM)
    xbc = _silu(conv + cb_ref[...])
    x = xbc[:, :SSM_D_INNER]
    ps = ps_ref[...]
    step_c = _softplus(ps + brow_ref[...])
    step_r = _softplus(ps.T + bcol_ref[...])
    if valid_len is not None:
        t_c = lax.broadcasted_iota(jnp.int32, step_c.shape, 0) + c * cl
        t_r = lax.broadcasted_iota(jnp.int32, step_r.shape, 1) + c * cl
        step_c = jnp.where(t_c < valid_len, step_c, 0.0)
        step_r = jnp.where(t_r < valid_len, step_r, 0.0)
    la_c = step_c * -jnp.exp(arow_ref[...])
    la_r = (step_r * -jnp.exp(acol_ref[...]))[PS_DT:PS_DT + SSM_HEADS, :]
    ii, jj = _tri(cl)
    incl = ii >= jj
    acs_c = _dot_hi(incl.astype(F32), la_c)
    acs_r = _dot_hi(la_r, (ii <= jj).astype(F32))
    last = acs_c[cl - 1:cl, :]
    e_last = jnp.exp(last)

    def expand(a):
        hi, lo = _split(a)
        return _dot(hi, ex_ref[...]) + _dot(lo, ex_ref[...])

    xs = x * expand(step_c)
    xe = (xs * expand(jnp.exp(last - acs_c))).astype(BF16)
    e_acs = expand(jnp.exp(acs_c))
    xs16 = xs.astype(BF16)
    lane = lax.broadcasted_iota(jnp.int32, (cl, pw), 1)
    srow = lax.broadcasted_iota(jnp.int32, (pw, 1), 0)
    ppg = n_pairs // SSM_GROUPS
    for g in range(SSM_GROUPS):
        b_g = xbc[:, SSM_D_INNER + g * SSM_STATE:SSM_D_INNER + (g + 1) * SSM_STATE].astype(BF16)
        c0 = SSM_D_INNER + SSM_GROUPS * SSM_STATE + g * SSM_STATE
        c_g = xbc[:, c0:c0 + SSM_STATE].astype(BF16)
        cb = _dot_nt(c_g, b_g)
        for r in range(ppg):
            j = g * ppg + r
            cols = slice(j * pw, (j + 1) * pw)
            y_heads = []
            for h in (2 * j, 2 * j + 1):
                dec = jnp.where(incl, jnp.exp(jnp.where(incl, acs_c[:, h:h + 1] - acs_r[h:h + 1, :], 0.0)), 0.0)
                y_heads.append(_dot((cb * dec).astype(BF16), xs16[:, cols]))
            y_diag = jnp.where(lane < SSM_HEAD_DIM, y_heads[0], y_heads[1])
            hst = states[j]
            y_off = _dot_nt(c_g, hst.astype(BF16)) * e_acs[:, cols]
            keep = jnp.where(srow < SSM_HEAD_DIM, e_last[:, 2 * j:2 * j + 1], e_last[:, 2 * j + 1:2 * j + 2])
            states[j] = hst * keep + _dot_tn(xe[:, cols], b_g)
            yacc_ref[:, cols] = y_diag + y_off
    for j in range(n_pairs):
        st_ref[j] = states[j]
    halo_ref[...] = x_raw[cl - 8:, :]
    y = (yacc_ref[...] + dexp_ref[...] * x) * _silu(z_ref[...])
    gw = SSM_D_INNER // SSM_GROUPS
    parts = []
    for g in range(SSM_GROUPS):
        yg = y[:, g * gw:(g + 1) * gw]
        parts.append(yg * lax.rsqrt(jnp.mean(yg * yg, axis=-1, keepdims=True) + EPS))
    y_ref[...] = (jnp.concatenate(parts, axis=1) * nw_ref[...]).astype(y_ref.dtype)

    @pl.when(c == nc - 1)
    def _():
        hl_ref[...] = st_ref[...]


def _ssd(proj, ps, conv0, h0, conv_w, conv_b, vec, ssm_d, ssm_norm, bsz, seq, cl, valid_len):
    nc = seq // cl
    n = bsz * seq
    brow, bcol, arow, acol = vec
    cs0 = jnp.pad(conv0, ((0, 0), (8 - (CONV_K - 1), 0), (0, 0)))
    dexp = jnp.repeat(ssm_d, SSM_HEAD_DIM).reshape(1, SSM_D_INNER)
    expand = (jnp.arange(SSM_D_INNER)[None, :] // SSM_HEAD_DIM == jnp.arange(PS_W)[:, None] - PS_DT).astype(BF16)
    n_pairs, pw = SSM_HEADS // 2, 2 * SSM_HEAD_DIM
    h0p = h0.reshape(bsz, n_pairs, pw, SSM_STATE)
    const = lambda shp: BS(shp, lambda b, c: tuple(0 for _ in shp))
    y, h_last = pl.pallas_call(
        functools.partial(_ssd_body, cl=cl, nc=nc, valid_len=valid_len),
        out_shape=(S((n, SSM_D_INNER), BF16), S(h0p.shape, F32)), grid=(bsz, nc),
        in_specs=[BS((cl, SSM_D_INNER), lambda b, c: (b * nc + c, C_SZ // SSM_D_INNER)),
                  BS((cl, SSM_CONV_DIM), lambda b, c: (b * nc + c, C_XBC // SSM_CONV_DIM)),
                  BS((cl, PS_W), lambda b, c: (b * nc + c, 0)),
                  BS((None, 8, SSM_CONV_DIM), lambda b, c: (b, 0, 0)),
                  BS((None, n_pairs, pw, SSM_STATE), lambda b, c: (b, 0, 0, 0)),
                  const((CONV_K, SSM_CONV_DIM)), const((1, SSM_CONV_DIM)),
                  const((1, PS_W)), const((PS_W, 1)), const((1, PS_W)), const((PS_W, 1)),
                  const((1, SSM_D_INNER)), const((1, SSM_D_INNER)), const((PS_W, SSM_D_INNER))],
        out_specs=(BS((cl, SSM_D_INNER), lambda b, c: (b * nc + c, 0)),
                   BS((None, n_pairs, pw, SSM_STATE), lambda b, c: (b, 0, 0, 0))),
        scratch_shapes=[pltpu.VMEM((n_pairs, pw, SSM_STATE), F32), pltpu.VMEM((8, SSM_CONV_DIM), F32),
                        pltpu.VMEM((cl, SSM_D_INNER), F32)],
        compiler_params=_cparams(("parallel", "arbitrary")), name="ssd",
    )(proj, proj, ps, cs0, h0p, conv_w, conv_b.reshape(1, -1), brow, bcol, arow, acol, dexp,
      ssm_norm.reshape(1, SSM_D_INNER), expand)
    return y, h_last.reshape(h0.shape)


def _unit_lower_inverses(mats, ii, jj, cl):
    eye = (ii == jj).astype(F32)
    pair = ((ii >> 1) == (jj >> 1)) & (ii > jj)
    invs = [eye - jnp.where(pair, a, 0.0) for a in mats]
    s = 2
    while s < cl:
        blk = ((ii // (2 * s)) == (jj // (2 * s))) & ((ii % (2 * s)) >= s) & ((jj % (2 * s)) < s)
        xs = [_dot3(inv, jnp.where(blk, a, 0.0)) for inv, a in zip(invs, mats)]
        invs = [inv - _dot3(x, inv) for inv, x in zip(invs, xs)]
        s *= 2
    return invs


def _gdn_body(q_ref, k_ref, v_ref, z_ref, ps_ref, cs0_ref, s0_ref, cw_ref, brow_ref, bcol_ref, arow_ref, acol_ref,
              nw_ref, o_ref, sl_ref, st_ref, halo_ref, *, cl, cps, nc, valid_len):
    c = pl.program_id(1)

    @pl.when(c == 0)
    def _():
        st_ref[...] = s0_ref[...]
        halo_ref[...] = cs0_ref[...]

    w = DN_W
    rows = cl * cps
    halo = halo_ref[...]
    states = [st_ref[h] for h in range(DN_HEADS)]
    raws = (q_ref[...], k_ref[...], v_ref[...])
    q, k, v = (_silu(_causal_conv(raws[p], halo[:, p * w:(p + 1) * w], cw_ref, p * w, (p + 1) * w)) for p in range(3))
    z = z_ref[...]
    ps = ps_ref[...]
    g_c = -jnp.exp(arow_ref[...]) * _softplus(ps + brow_ref[...])
    g_r = -jnp.exp(acol_ref[...]) * _softplus(ps.T + bcol_ref[...])
    beta = _sigmoid(ps)
    if valid_len is not None:
        t_c = lax.broadcasted_iota(jnp.int32, g_c.shape, 0) + c * rows
        t_r = lax.broadcasted_iota(jnp.int32, g_r.shape, 1) + c * rows
        g_c = jnp.where(t_c < valid_len, g_c, 0.0)
        beta = jnp.where(t_c < valid_len, beta, 0.0)
        g_r = jnp.where(t_r < valid_len, g_r, 0.0)
    ii, jj = _tri(cl)
    incl = ii >= jj
    strict = ii > jj
    lower_ones = incl.astype(F32)
    upper_ones = (ii <= jj).astype(F32)
    off = PS_DECAY - 16
    nw = nw_ref[...]
    pre = []
    for ci in range(cps):
        rs = slice(ci * cl, (ci + 1) * cl)
        gcs_c = _dot_hi(lower_ones, g_c[rs, 16:24])
        gcs_r = _dot_hi(g_r[16:24, rs], upper_ones)
        for h in range(DN_HEADS):
            sl = slice(h * DN_DK, (h + 1) * DN_DK)
            qh = q[rs, sl]
            kh = k[rs, sl]
            qh = qh * lax.rsqrt(jnp.sum(qh * qh, axis=-1, keepdims=True) + EPS) * (DN_DK ** -0.5)
            kh = kh * lax.rsqrt(jnp.sum(kh * kh, axis=-1, keepdims=True) + EPS)
            bh = beta[rs, PS_BETA + h:PS_BETA + h + 1]
            col = gcs_c[:, off + h:off + h + 1]
            row = gcs_r[off + h:off + h + 1, :]
            last = gcs_c[cl - 1:cl, off + h:off + h + 1]
            e_col = jnp.exp(col)
            kb = kh * bh
            pre.append(dict(
                dec=jnp.where(incl, jnp.exp(jnp.where(incl, col - row, 0.0)), 0.0),
                q16=qh.astype(BF16), k16=kh.astype(BF16), kb16=kb.astype(BF16),
                vb=v[rs, sl] * bh, kbe=kb * e_col,
                qd16=(qh * e_col).astype(BF16),
                kd16=(kh * jnp.exp(last - col)).astype(BF16),
                g_end=jnp.exp(last),
                gate=nw * _silu(z[rs, sl])))
    lowers = [jnp.where(strict, _dot_nt(t["kb16"], t["k16"]) * t["dec"], 0.0) for t in pre]
    qks = [(_dot_nt(t["q16"], t["k16"]) * t["dec"]).astype(BF16) for t in pre]
    tinvs = _unit_lower_inverses(lowers, ii, jj, cl)
    us = [_dot3(ti, t["vb"]) for ti, t in zip(tinvs, pre)]
    ws = [_dot3(ti, t["kbe"]).astype(BF16) for ti, t in zip(tinvs, pre)]
    outs = []
    for ci in range(cps):
        items = range(ci * DN_HEADS, (ci + 1) * DN_HEADS)
        s16 = [st.astype(BF16) for st in states]
        v16 = [(us[i] - _dot(ws[i], s16[h])).astype(BF16) for h, i in enumerate(items)]
        os_ = [_dot(pre[i]["qd16"], s16[h]) + _dot(qks[i], v16[h]) for h, i in enumerate(items)]
        states = [states[h] * pre[i]["g_end"] + _dot_tn(pre[i]["kd16"], v16[h]) for h, i in enumerate(items)]
        outs += [o * lax.rsqrt(jnp.mean(o * o, axis=-1, keepdims=True) + EPS) * pre[i]["gate"]
                 for o, i in zip(os_, items)]
    for ci in range(cps):
        for h in range(DN_HEADS):
            o_ref[ci * cl:(ci + 1) * cl, h * DN_DV:(h + 1) * DN_DV] = outs[ci * DN_HEADS + h].astype(o_ref.dtype)
    for h in range(DN_HEADS):
        st_ref[h] = states[h]
    for p in range(3):
        halo_ref[:, p * w:(p + 1) * w] = raws[p][rows - 8:, :]

    @pl.when(c == nc - 1)
    def _():
        sl_ref[...] = st_ref[...]


def _gdn(proj, ps, conv0, s0, conv_w, vec, dn_norm, bsz, seq, cl, cps, valid_len):
    rows = cl * cps
    nc = seq // rows
    n = bsz * seq
    brow, bcol, arow, acol = vec
    cs0 = jnp.pad(conv0, ((0, 0), (8 - (CONV_K - 1), 0), (0, 0)))
    w = DN_W
    const = lambda shp: BS(shp, lambda b, c: tuple(0 for _ in shp))
    blk = lambda col: BS((rows, w), lambda b, c: (b * nc + c, col // w))
    o, s_last = pl.pallas_call(
        functools.partial(_gdn_body, cl=cl, cps=cps, nc=nc, valid_len=valid_len),
        out_shape=(S((n, w), BF16), S(s0.shape, F32)), grid=(bsz, nc),
        in_specs=[blk(C_DQ), blk(C_DK), blk(C_DV), blk(C_DZ),
                  BS((rows, PS_W), lambda b, c: (b * nc + c, 0)),
                  BS((None, 8, DN_CONV_DIM), lambda b, c: (b, 0, 0)),
                  BS((None, DN_HEADS, DN_DK, DN_DV), lambda b, c: (b, 0, 0, 0)),
                  const((CONV_K, DN_CONV_DIM)),
                  const((1, PS_W)), const((PS_W, 1)), const((1, PS_W)), const((PS_W, 1)), const((1, DN_DV))],
        out_specs=(BS((rows, w), lambda b, c: (b * nc + c, 0)),
                   BS((None, DN_HEADS, DN_DK, DN_DV), lambda b, c: (b, 0, 0, 0))),
        scratch_shapes=[pltpu.VMEM((DN_HEADS, DN_DK, DN_DV), F32), pltpu.VMEM((8, DN_CONV_DIM), F32)],
        compiler_params=_cparams(("parallel", "arbitrary")), name="gdn",
    )(proj, proj, proj, proj, ps, cs0, s0, conv_w, brow, bcol, arow, acol, dn_norm.reshape(1, DN_DV))
    return o, s_last


def _router_body(h_ref, lw_ref, rw_ref, rb_ref, meta_ref):
    x = h_ref[...]
    xn = x * lax.rsqrt(jnp.mean(x * x, axis=-1, keepdims=True) + EPS) * lw_ref[...]
    logits = _dot_hi(xn, rw_ref[...]) + rb_ref[...]
    lane = lax.broadcasted_iota(jnp.int32, logits.shape, 1)
    big = jnp.int32(logits.shape[1])
    m1 = jnp.max(logits, axis=1, keepdims=True)
    i1 = jnp.min(jnp.where(logits == m1, lane, big), axis=1, keepdims=True)
    rest = jnp.where(lane == i1, NEG, logits)
    m2 = jnp.max(rest, axis=1, keepdims=True)
    i2 = jnp.min(jnp.where(rest == m2, lane, big), axis=1, keepdims=True)
    e2 = jnp.exp(m2 - m1)
    g1 = 1.0 / (1.0 + e2)
    g2 = e2 * g1
    meta = jnp.where(lane == 0, i1.astype(F32), jnp.where(lane == 1, i2.astype(F32),
                     jnp.where(lane == 2, g1, jnp.where(lane == 3, g2, 0.0))))
    meta_ref[...] = meta


def _router(h, ln_w, router_w, router_b, tm):
    n, d = h.shape
    rw = jnp.pad(router_w, ((0, 0), (0, 128 - N_EXPERTS)))
    rb = jnp.pad(router_b, (0, 128 - N_EXPERTS), constant_values=NEG).reshape(1, 128)
    const = lambda shp: BS(shp, lambda i: (0, 0))
    return pl.pallas_call(
        _router_body, out_shape=S((n, 128), F32), grid=(n // tm,),
        in_specs=[BS((tm, d), lambda i: (i, 0)), const((1, d)), const((d, 128)), const((1, 128))],
        out_specs=BS((tm, 128), lambda i: (i, 0)),
        compiler_params=_cparams(("parallel",)), name="router")(h, ln_w.reshape(1, d), rw, rb)


def _experts_body(be_ref, tok_ref, nused_ref, h_hbm, lw_ref, wg_ref, wu_ref, wd_ref, y_ref,
                  xg_ref, xb_ref, acc_ref, sem, *, tm, nf):
    i = pl.program_id(0)
    f = pl.program_id(1)
    used = i < nused_ref[0]
    slot = lax.rem(i, 2)

    def gather(blk, dst):
        def start(r, carry):
            pltpu.make_async_copy(h_hbm.at[pl.ds(tok_ref[blk * tm + r], 1), :],
                                  xg_ref.at[dst, pl.ds(r, 1), :], sem.at[dst]).start()
            return carry

        lax.fori_loop(0, tm, start, 0, unroll=8)

    @pl.when((i == 0) & (f == 0))
    def _():
        gather(0, 0)

    @pl.when(used & (f == 0))
    def _():
        pltpu.make_async_copy(h_hbm.at[pl.ds(0, tm), :], xg_ref.at[slot], sem.at[slot]).wait()
        x = xg_ref[slot]
        xn = x * lax.rsqrt(jnp.mean(x * x, axis=-1, keepdims=True) + EPS) * lw_ref[...]
        xb_ref[...] = xn.astype(BF16)
        acc_ref[...] = jnp.zeros(acc_ref.shape, F32)

    chunk = tm // nf

    @pl.when(used)
    def _():
        nxt = jnp.minimum(i + 1, nused_ref[0] - 1)
        for r in range(chunk):
            pltpu.make_async_copy(h_hbm.at[pl.ds(tok_ref[nxt * tm + f * chunk + r], 1), :],
                                  xg_ref.at[1 - slot, pl.ds(f * chunk + r, 1), :], sem.at[1 - slot]).start()
        xb = xb_ref[...]
        hid = _silu(_dot(xb, wg_ref[...].astype(BF16))) * _dot(xb, wu_ref[...].astype(BF16))
        acc_ref[...] += _dot(hid.astype(BF16), wd_ref[...].astype(BF16))

    @pl.when((i == nused_ref[0] - 1) & (f == nf - 1))
    def _():
        pltpu.make_async_copy(h_hbm.at[pl.ds(0, tm), :], xg_ref.at[1 - slot], sem.at[1 - slot]).wait()

    @pl.when(used & (f == nf - 1))
    def _():
        y_ref[...] = acc_ref[...]

    @pl.when(jnp.logical_not(used) & (f == nf - 1))
    def _():
        y_ref[...] = jnp.zeros(y_ref.shape, F32)


def _experts(h, ln_w, block_expert, row_token, n_used, wg, wu, wd, tm, tf):
    n, d = h.shape
    rows = row_token.shape[0]
    n_blocks = rows // tm
    (wg, layer), (wu, _), (wd, _) = wg, wu, wd
    ff = wg.shape[3]
    nf = ff // tf
    grid_spec = pltpu.PrefetchScalarGridSpec(
        num_scalar_prefetch=3, grid=(n_blocks, nf),
        in_specs=[BS(memory_space=pl.ANY),
                  BS((1, d), lambda i, f, be, tok, nu: (0, 0)),
                  BS((None, None, d, tf), lambda i, f, be, tok, nu: (layer, be[i], 0, f)),
                  BS((None, None, d, tf), lambda i, f, be, tok, nu: (layer, be[i], 0, f)),
                  BS((None, None, tf, d), lambda i, f, be, tok, nu: (layer, be[i], f, 0))],
        out_specs=BS((tm, d), lambda i, f, be, tok, nu: (i, 0)),
        scratch_shapes=[pltpu.VMEM((2, tm, d), F32), pltpu.VMEM((tm, d), BF16), pltpu.VMEM((tm, d), F32),
                        pltpu.SemaphoreType.DMA((2,))])
    return pl.pallas_call(
        functools.partial(_experts_body, tm=tm, nf=nf),
        out_shape=S((rows, d), F32), grid_spec=grid_spec,
        compiler_params=_cparams(("arbitrary", "arbitrary")), name="experts",
    )(block_expert, row_token, n_used, h, ln_w.reshape(1, d), wg, wu, wd)


def _combine_body(pos_ref, y_hbm, h_ref, meta_ref, o_ref, ya_ref, yb_ref, sem, *, tc, n_steps):
    i = pl.program_id(0)
    slot = lax.rem(i, 2)

    def gather(step, dst):
        def start(r, carry):
            t = step * tc + r
            pltpu.make_async_copy(y_hbm.at[pl.ds(pos_ref[2 * t], 1), :], ya_ref.at[dst, pl.ds(r, 1), :],
                                  sem.at[dst]).start()
            pltpu.make_async_copy(y_hbm.at[pl.ds(pos_ref[2 * t + 1], 1), :], yb_ref.at[dst, pl.ds(r, 1), :],
                                  sem.at[dst]).start()
            return carry

        lax.fori_loop(0, tc, start, 0, unroll=8)

    @pl.when(i == 0)
    def _():
        gather(0, 0)

    @pl.when(i + 1 < n_steps)
    def _():
        gather(i + 1, 1 - slot)

    pltpu.make_async_copy(y_hbm.at[pl.ds(0, tc), :], ya_ref.at[slot], sem.at[slot]).wait()
    pltpu.make_async_copy(y_hbm.at[pl.ds(0, tc), :], yb_ref.at[slot], sem.at[slot]).wait()
    meta = meta_ref[...]
    o_ref[...] = h_ref[...] + (meta[:, 2:3] * ya_ref[slot] + meta[:, 3:4] * yb_ref[slot])


def _combine(h, y_sorted, pos, meta, tc):
    n, d = h.shape
    n_steps = n // tc
    grid_spec = pltpu.PrefetchScalarGridSpec(
        num_scalar_prefetch=1, grid=(n_steps,),
        in_specs=[BS(memory_space=pl.ANY), BS((tc, d), lambda i, pos: (i, 0)), BS((tc, 128), lambda i, pos: (i, 0))],
        out_specs=BS((tc, d), lambda i, pos: (i, 0)),
        scratch_shapes=[pltpu.VMEM((2, tc, d), F32), pltpu.VMEM((2, tc, d), F32), pltpu.SemaphoreType.DMA((2,))])
    return pl.pallas_call(
        functools.partial(_combine_body, tc=tc, n_steps=n_steps), out_shape=S((n, d), F32), grid_spec=grid_spec,
        compiler_params=_cparams(("arbitrary",)), name="moe_combine")(pos, y_sorted, h, meta)


def _moe(h, ln_w, router_w, router_b, wg, wu, wd, tm_r, tm, tf, tc):
    n, d = h.shape
    meta = _router(h, ln_w, router_w, router_b, tm_r)
    idx = meta[:, 0:2].astype(jnp.int32)
    member = jnp.sum(jax.nn.one_hot(idx, N_EXPERTS, dtype=jnp.int32), axis=1)
    before = jnp.cumsum(member, axis=0) - member
    counts = jnp.sum(member, axis=0)
    padded = (counts + tm - 1) // tm * tm
    pad_ends = jnp.cumsum(padded)
    pad_starts = pad_ends - padded
    pos = (pad_starts[idx] + jnp.take_along_axis(before, idx, axis=1)).astype(jnp.int32)
    n_blocks = -(-(2 * n + N_EXPERTS * (tm - 1)) // tm)
    rows = n_blocks * tm
    tok = jnp.broadcast_to(jnp.arange(n, dtype=jnp.int32)[:, None], (n, 2))
    row_token = jnp.zeros((rows,), jnp.int32).at[pos.reshape(-1)].set(tok.reshape(-1))
    n_used = (pad_ends[-1] // tm).astype(jnp.int32).reshape(1)
    blk_start = jnp.minimum(jnp.arange(n_blocks, dtype=jnp.int32), n_used[0] - 1) * tm
    block_expert = jnp.minimum(jnp.searchsorted(pad_ends, blk_start, side="right"), N_EXPERTS - 1).astype(jnp.int32)
    y_sorted = _experts(h, ln_w, block_expert, row_token, n_used, wg, wu, wd, tm, tf)
    return _combine(h, y_sorted, pos.reshape(-1), meta, tc)


def _layer(i, h, p, lw, cfg, att_fn, ssm_conv0, ssm_h0, dn_conv0, dn_s0):
    bsz, seq, tm, cl_ssd, cl_gdn, pad_to, valid_len = (cfg[k] for k in
                                                      ("bsz", "seq", "tm", "cl_ssd", "cl_gdn", "pad_to", "valid_len"))
    n = bsz * seq
    xn = _rmsnorm(h, lw["ln_mix"], tm)
    ident = lambda d, e: d[0]
    tn = 512
    rot = PROJ_A_ROT // tn
    n_a = PROJ_A_W // tn
    proj_a = _fused_matmul([xn], [(0, lw["w_in"], 0)], ident, [], F32, PROJ_A_W, tm, tn, "in_proj_a",
                           out_tile=lambda j: lax.rem(j + (n_a - rot), n_a))
    proj_b = _fused_matmul([xn], [(0, lw["w_tail"], 0)], ident, [], F32, PROJ_B_W, tm, tn, "in_proj_b")
    ps = _fused_matmul([xn], [(0, lw["w_small"], 0)], ident, [], F32, PS_W, tm, PS_W, "in_proj_small")

    q16, k16, v16, k32, v32 = _qkv_prep(proj_a, lw["q_norm"], lw["k_norm"], tm)
    att = att_fn(q16, k16, v16)

    if pad_to is None:
        pa_r, pb_r, ps_r, seq_r = proj_a, proj_b, ps, seq
    else:
        seq_r = pad_to
        padded = lambda a: jnp.pad(a.reshape(bsz, seq, -1), ((0, 0), (0, pad_to - seq), (0, 0))).reshape(bsz * pad_to, -1)
        pa_r, pb_r, ps_r = padded(proj_a), padded(proj_b), padded(ps)
    y, ssm_h1 = _ssd(pa_r, ps_r, ssm_conv0, ssm_h0, lw["ssm_conv_w"], lw["ssm_conv_b"], lw["vec"], lw["ssm_d"],
                     lw["ssm_norm"], bsz, seq_r, cl_ssd, valid_len)
    o, dn_s1 = _gdn(pb_r, ps_r, dn_conv0, dn_s0, lw["dn_conv_w"], lw["vec"], lw["dn_norm"], bsz, seq_r, cl_gdn,
                    cfg["cps_gdn"], valid_len)
    if pad_to is not None:
        y = y.reshape(bsz, pad_to, -1)[:, :seq].reshape(n, -1)
        o = o.reshape(bsz, pad_to, -1)[:, :seq].reshape(n, -1)

    def merge(d, e):
        return _sigmoid(d[0]) * d[3] + _sigmoid(d[1]) * d[4] + _sigmoid(d[2]) * d[5]

    d_m = D_MODEL
    merged = _fused_matmul([xn, att, y, o],
                           [(0, lw["w_gate"], 0), (0, lw["w_gate"], d_m), (0, lw["w_gate"], 2 * d_m),
                            (1, lw["w_up_att"], 0), (2, lw["w_up_ssm"], 0), (3, lw["w_up_dn"], 0)],
                           merge, [], BF16, d_m, tm, 512, "merge")
    resid = lambda d, e: e[0] + d[0]
    h = _fused_matmul([merged], [(0, lw["w_out"], 0)], resid, [h], F32, d_m, tm, 512, "out_proj")

    if i % 2 == 0:
        hn = _rmsnorm(h, lw["ln_ffn"], tm)
        d_ff = lw["d_ff"]
        ff = _fused_matmul([hn], [(0, lw["ffn_w_gate"], 0), (0, lw["ffn_w_up"], 0)],
                           lambda d, e: _silu(d[0]) * d[1], [], BF16, d_ff, min(tm, 512), 1408, "ffn_up")
        h = _fused_matmul([ff], [(0, lw["ffn_w_down"], 0)], resid, [h], F32, d_m, min(tm, 512), 512, "ffn_down")
    else:
        h = _moe(h, lw["ln_ffn"], lw["router_w"], lw["router_b"], lw["exp_w_gate"], lw["exp_w_up"], lw["exp_w_down"],
                 min(tm, 512), cfg["tm_moe"], TF_MOE, cfg["tc"])

    hn = _rmsnorm(h, lw["ln_ple"], tm)
    h = _fused_matmul([hn, p.astype(BF16)], [(0, lw["ple_w_gate"], 0), (1, lw["ple_w_proj"], 0)],
                      lambda d, e: e[0] + _sigmoid(d[0]) * d[1], [h], F32, d_m, tm, 512, "ple")

    xbc_raw = proj_a[:, C_XBC:C_XBC + SSM_CONV_DIM].reshape(bsz, seq, -1)
    dn_raw = proj_b[:, C_DQ:C_DQ + DN_CONV_DIM].reshape(bsz, seq, -1)
    keep = CONV_K - 1
    if seq >= keep:
        ssm_conv1, dn_conv1 = xbc_raw[:, seq - keep:], dn_raw[:, seq - keep:]
    else:
        ssm_conv1 = jnp.concatenate([ssm_conv0, xbc_raw], axis=1)[:, -keep:]
        dn_conv1 = jnp.concatenate([dn_conv0, dn_raw], axis=1)[:, -keep:]
    new_k = k32.reshape(bsz, seq, ATT_HEADS, 2 * ATT_DK)
    new_v = v32.reshape(bsz, seq, ATT_HEADS, ATT_DV)
    return h, (new_k, new_v, ssm_conv1, ssm_h1, dn_conv1, dn_s1)


def _prep_layer_weights(i, W):
    w_in = W["w_in"][i]
    o_dt = PROJ_A_W
    o_tail = o_dt + SSM_HEADS
    o_beta = o_tail + PROJ_B_W
    w_tail = w_in[:, o_tail:o_beta].astype(BF16)
    w_small = jnp.concatenate([w_in[:, o_dt:o_tail], w_in[:, o_beta:o_beta + 2 * DN_HEADS],
                               jnp.zeros((D_MODEL, PS_W - SSM_HEADS - 2 * DN_HEADS), w_in.dtype)], axis=1)
    bias_row = jnp.zeros((PS_W,), F32).at[PS_DT:PS_DT + SSM_HEADS].set(W["ssm_dt_bias"][i])
    bias_row = bias_row.at[PS_DECAY:PS_DECAY + DN_HEADS].set(W["dn_dt_bias"][i])
    alog_row = jnp.zeros((PS_W,), F32).at[PS_DT:PS_DT + SSM_HEADS].set(W["ssm_a_log"][i])
    alog_row = alog_row.at[PS_DECAY:PS_DECAY + DN_HEADS].set(W["dn_a_log"][i])
    vec = (bias_row.reshape(1, PS_W), bias_row.reshape(PS_W, 1), alog_row.reshape(1, PS_W), alog_row.reshape(PS_W, 1))
    lw = {
        "ln_mix": W["ln_mix"][i], "w_in": (W["w_in"], i), "w_tail": w_tail, "w_small": w_small.astype(BF16),
        "q_norm": W["q_norm"][i], "k_norm": W["k_norm"][i], "att_subln": W["att_subln"][i],
        "ssm_conv_w": W["ssm_conv_w"][i], "ssm_conv_b": W["ssm_conv_b"][i], "vec": vec,
        "ssm_d": W["ssm_d"][i], "ssm_norm": W["ssm_norm"][i],
        "dn_conv_w": W["dn_conv_w"][i], "dn_norm": W["dn_norm"][i],
        "w_gate": (W["w_gate"], i), "w_up_att": (W["w_up_att"], i), "w_up_ssm": (W["w_up_ssm"], i),
        "w_up_dn": (W["w_up_dn"], i), "w_out": (W["w_out"], i),
        "ln_ffn": W["ln_ffn"][i], "ln_ple": W["ln_ple"][i],
        "ple_w_gate": (W["ple_w_gate"], i), "ple_w_proj": (W["ple_w_proj"], i),
    }
    if i % 2 == 0:
        lw.update(ffn_w_gate=(W["ffn_w_gate"], i // 2), ffn_w_up=(W["ffn_w_up"], i // 2),
                  ffn_w_down=(W["ffn_w_down"], i // 2), d_ff=W["ffn_w_gate"].shape[2])
    else:
        lw.update(router_w=W["router_w"][i // 2], router_b=W["router_b"][i // 2],
                  exp_w_gate=(W["exp_w_gate"], i // 2), exp_w_up=(W["exp_w_up"], i // 2),
                  exp_w_down=(W["exp_w_down"], i // 2))
    lam_init = 0.8 - 0.6 * math.exp(-0.3 * i)
    lam = (jnp.exp(jnp.sum(W["lam_q1"][i] * W["lam_k1"][i])) - jnp.exp(jnp.sum(W["lam_q2"][i] * W["lam_k2"][i]))
           + lam_init).astype(F32)
    return lw, lam, lam_init


def _pick(n, pref):
    t = min(n, pref)
    while n % t:
        t //= 2
    return t


def kernel(x_prompt, x_sample, cache_k, cache_v, state_ssm_conv, state_ssm, state_dn_conv, state_dn, page_table, p_prompt, p_sample, ln_mix, w_in, w_gate, q_norm, k_norm, lam_q1, lam_k1, lam_q2, lam_k2, att_subln, rel_bias, ssm_conv_w, ssm_conv_b, ssm_dt_bias, ssm_a_log, ssm_d, ssm_norm, dn_conv_w, dn_dt_bias, dn_a_log, dn_norm, w_up_att, w_up_ssm, w_up_dn, w_out, ln_ffn, ffn_w_gate, ffn_w_up, ffn_w_down, router_w, router_b, exp_w_gate, exp_w_up, exp_w_down, ln_ple, ple_w_gate, ple_w_proj):
    W = dict(ln_mix=ln_mix, w_in=w_in, w_gate=w_gate, q_norm=q_norm, k_norm=k_norm, lam_q1=lam_q1, lam_k1=lam_k1,
             lam_q2=lam_q2, lam_k2=lam_k2, att_subln=att_subln, ssm_conv_w=ssm_conv_w, ssm_conv_b=ssm_conv_b,
             ssm_dt_bias=ssm_dt_bias, ssm_a_log=ssm_a_log, ssm_d=ssm_d, ssm_norm=ssm_norm, dn_conv_w=dn_conv_w,
             dn_dt_bias=dn_dt_bias, dn_a_log=dn_a_log, dn_norm=dn_norm, w_up_att=w_up_att, w_up_ssm=w_up_ssm,
             w_up_dn=w_up_dn, w_out=w_out, ln_ffn=ln_ffn, ffn_w_gate=ffn_w_gate, ffn_w_up=ffn_w_up,
             ffn_w_down=ffn_w_down, router_w=router_w, router_b=router_b, exp_w_gate=exp_w_gate, exp_w_up=exp_w_up,
             exp_w_down=exp_w_down, ln_ple=ln_ple, ple_w_gate=ple_w_gate, ple_w_proj=ple_w_proj)
    depth = ln_mix.shape[0]
    bp, lp, d = x_prompt.shape
    bs, ls, _ = x_sample.shape
    n_pages = page_table.shape[1]
    t_att = _pick(lp, T_ATT)
    cfg_p = dict(bsz=bp, seq=lp, tm=_pick(bp * lp, TM), cl_ssd=_pick(lp, CL_SSD), cl_gdn=_pick(lp, CL_GDN), pad_to=None,
                 valid_len=None, tm_moe=_pick(bp * lp, TM_MOE), tc=_pick(bp * lp, TC_MOE))
    cfg_p["cps_gdn"] = _pick(lp // cfg_p["cl_gdn"], CPS_GDN)
    cfg_s = dict(bsz=bs, seq=ls, tm=bs * ls, cl_ssd=CL_SAMPLE, cl_gdn=CL_SAMPLE, cps_gdn=1, pad_to=CL_SAMPLE,
                 valid_len=ls,
                 tm_moe=64, tc=bs * ls)
    hp = x_prompt.reshape(bp * lp, d)
    hs = x_sample.reshape(bs * ls, d)
    st_p, st_s = [], []
    tabs = _bias_tables(rel_bias, t_att)
    for i in range(depth):
        lw, lam, lam_init = _prep_layer_weights(i, W)
        post = 1.0 - lam_init
        att_p = lambda q, k, v: _attn_prompt(q, k, v, lam, rel_bias, tabs, lw["att_subln"], post, bp, lp, t_att)
        hp, st = _layer(i, hp, p_prompt[i].reshape(bp * lp, -1), lw, cfg_p, att_p,
                        jnp.zeros((bp, CONV_K - 1, SSM_CONV_DIM), F32),
                        jnp.zeros((bp, SSM_HEADS, SSM_HEAD_DIM, SSM_STATE), F32),
                        jnp.zeros((bp, CONV_K - 1, DN_CONV_DIM), F32),
                        jnp.zeros((bp, DN_HEADS, DN_DK, DN_DV), F32))
        st_p.append(st)
        att_s = lambda q, k, v: _attn_decode(i, q, k, v, cache_k, cache_v, page_table, lam, rel_bias,
                                             lw["att_subln"], post, _pick(n_pages, PAGES_PER_STEP))
        hs, st = _layer(i, hs, p_sample[i].reshape(bs * ls, -1), lw, cfg_s, att_s,
                        state_ssm_conv[i], state_ssm[i], state_dn_conv[i], state_dn[i])
        st_s.append(st)
    stk = lambda sts, j: jnp.stack([s[j] for s in sts])
    return (hp.reshape(bp, lp, d), hs.reshape(bs, ls, d),
            stk(st_p, 0), stk(st_p, 1), stk(st_p, 2), stk(st_p, 3), stk(st_p, 4), stk(st_p, 5),
            stk(st_s, 0), stk(st_s, 1), stk(st_s, 2), stk(st_s, 3), stk(st_s, 4), stk(st_s, 5))
```

```python
import functools
import math

import jax
import jax.numpy as jnp
from jax import lax
from jax.experimental import pallas as pl
from jax.experimental.pallas import tpu as pltpu

F32 = jnp.float32
BF16 = jnp.bfloat16
S = jax.ShapeDtypeStruct
BS = pl.BlockSpec

D_MODEL = 1024
ATT_HEADS = 4
ATT_DK = 64
ATT_DV = 128
ATT_W = ATT_HEADS * ATT_DV
NUM_BUCKETS = 32
MAX_DISTANCE = 128
SSM_HEADS = 16
SSM_HEAD_DIM = 64
SSM_GROUPS = 2
SSM_STATE = 128
SSM_D_INNER = 1024
SSM_CONV_DIM = 1536
DN_HEADS = 4
DN_DK = 128
DN_DV = 128
DN_W = 512
DN_CONV_DIM = 1536
CONV_K = 4
N_EXPERTS = 8
EPS = 1e-6
NEG = -1e30
LOG2E = 1.4426950408889634

PROJ_A_W, PROJ_A_ROT = 4096, 2560
C_XBC, C_AQ, C_AK, C_AV, C_SZ = 0, 1536, 2048, 2560, 3072
PROJ_B_W = 2048
C_DQ, C_DK, C_DV, C_DZ = 0, 512, 1024, 1536
PS_DT, PS_BETA, PS_DECAY, PS_W = 0, 16, 20, 128

VMEM_LIMIT = 56 * 1024 * 1024
T_ATT = 512
TM = 1024
CL_SSD = 128
CL_GDN = 64
CPS_GDN = 2
CL_SAMPLE = 16
PAGES_PER_STEP = 32
TM_MOE = 512
TF_MOE = 896
TC_MOE = 256


def _cparams(sem):
    return pltpu.CompilerParams(dimension_semantics=sem, vmem_limit_bytes=VMEM_LIMIT)


def _dot(a, b):
    return jnp.dot(a, b, preferred_element_type=F32)


def _dot_nt(a, b):
    return lax.dot_general(a, b, (((1,), (1,)), ((), ())), preferred_element_type=F32)


def _dot_tn(a, b):
    return lax.dot_general(a, b, (((0,), (0,)), ((), ())), preferred_element_type=F32)


def _dot_hi(a, b):
    return jnp.dot(a, b, preferred_element_type=F32, precision=lax.Precision.HIGHEST)


def _split(a):
    hi = a.astype(BF16)
    lo = (a - hi.astype(F32)).astype(BF16)
    return hi, lo


def _dot3(a, b):
    ah, al = _split(a)
    bh, bl = _split(b)
    return _dot(ah, bh) + (_dot(ah, bl) + _dot(al, bh))


def _sigmoid(x):
    return 1.0 / (1.0 + jnp.exp(-x))


def _silu(x):
    return x * _sigmoid(x)


def _softplus(x):
    return jnp.maximum(x, 0.0) + jnp.log1p(jnp.exp(-jnp.abs(x)))


def _rmsnorm_body(x_ref, w_ref, o_ref):
    x = x_ref[...]
    y = x * lax.rsqrt(jnp.mean(x * x, axis=-1, keepdims=True) + EPS) * w_ref[...]
    o_ref[...] = y.astype(o_ref.dtype)


def _rmsnorm(x, w, tm):
    n, d = x.shape
    return pl.pallas_call(
        _rmsnorm_body, out_shape=S((n, d), BF16), grid=(n // tm,),
        in_specs=[BS((tm, d), lambda i: (i, 0)), BS((1, d), lambda i: (0, 0))],
        out_specs=BS((tm, d), lambda i: (i, 0)),
        compiler_params=_cparams(("parallel",)), name="rmsnorm")(x, w.reshape(1, d))


def _fused_matmul(acts, pairs, combine, extras, out_dtype, m, tm, tn, name, out_tile=None):
    n = acts[0].shape[0]
    na, npair, nex = len(acts), len(pairs), len(extras)
    act_idx = [a for a, _, _ in pairs]

    def body(*refs):
        a_vals = [r[...] for r in refs[:na]]
        dots = [_dot(a_vals[act_idx[j]], refs[na + j][...].astype(BF16)) for j in range(npair)]
        ex = [r[...] for r in refs[na + npair:na + npair + nex]]
        o_ref = refs[na + npair + nex]
        o_ref[...] = combine(dots, ex).astype(o_ref.dtype)

    def w_spec(w, c0):
        off = c0 // tn
        if isinstance(w, tuple):
            arr, layer = w
            return BS((None, arr.shape[1], tn), lambda i, j: (layer, 0, j + off))
        return BS((w.shape[0], tn), lambda i, j: (0, j + off))

    in_specs = ([BS((tm, a.shape[1]), lambda i, j: (i, 0)) for a in acts]
                + [w_spec(w, c0) for _, w, c0 in pairs]
                + [BS((tm, tn), lambda i, j: (i, j)) for _ in extras])
    out_tile = out_tile or (lambda j: j)
    return pl.pallas_call(
        body, out_shape=S((n, m), out_dtype), grid=(n // tm, m // tn),
        in_specs=in_specs, out_specs=BS((tm, tn), lambda i, j: (i, out_tile(j))),
        compiler_params=_cparams(("parallel", "arbitrary")), name=name,
    )(*acts, *[w[0] if isinstance(w, tuple) else w for _, w, _ in pairs], *extras)


def _qkv_body(aq_ref, ak_ref, av_ref, g_ref, qw_ref, kw_ref, q_ref, kb_ref, vb_ref, kf_ref, vf_ref):
    g = g_ref[...]

    def gnorm(x, w):
        hi, lo = _split(x * x)
        ss = _dot(hi, g) + _dot(lo, g)
        return x * lax.rsqrt(ss * (1.0 / ATT_DK) + EPS) * w

    q = gnorm(aq_ref[...], qw_ref[...]) * (ATT_DK ** -0.5 * LOG2E)
    k = gnorm(ak_ref[...], kw_ref[...])
    v = av_ref[...]
    q_ref[...] = q.astype(BF16)
    kb_ref[...] = k.astype(BF16)
    vb_ref[...] = v.astype(BF16)
    for h in range(ATT_HEADS):
        kf_ref[:, h, :] = k[:, h * ATT_DV:(h + 1) * ATT_DV]
        vf_ref[:, h, :] = v[:, h * ATT_DV:(h + 1) * ATT_DV]


def _qkv_prep(proj, q_norm, k_norm, tm):
    n = proj.shape[0]
    w = ATT_W
    gi = jnp.arange(w) // ATT_DK
    gmat = (gi[:, None] == gi[None, :]).astype(BF16)
    qw = jnp.tile(q_norm, w // ATT_DK).reshape(1, w)
    kw = jnp.tile(k_norm, w // ATT_DK).reshape(1, w)
    row = lambda c: BS((tm, w), lambda i: (i, c))
    const = lambda shp: BS(shp, lambda i: (0, 0))
    thd = BS((tm, ATT_HEADS, ATT_DV), lambda i: (i, 0, 0))
    return pl.pallas_call(
        _qkv_body,
        out_shape=(S((n, w), BF16), S((n, w), BF16), S((n, w), BF16),
                   S((n, ATT_HEADS, ATT_DV), F32), S((n, ATT_HEADS, ATT_DV), F32)),
        grid=(n // tm,),
        in_specs=[row(C_AQ // w), row(C_AK // w), row(C_AV // w), const((w, w)), const((1, w)), const((1, w))],
        out_specs=(row(0), row(0), row(0), thd, thd),
        compiler_params=_cparams(("parallel",)), name="qkv_prep")(proj, proj, proj, gmat, qw, kw)


def _t5_bucket(n):
    max_exact = NUM_BUCKETS // 2
    scaled = jnp.log(jnp.maximum(n, 1).astype(F32) / max_exact) / math.log(MAX_DISTANCE / max_exact)
    large = jnp.minimum(max_exact + (scaled * (NUM_BUCKETS - max_exact)).astype(jnp.int32), NUM_BUCKETS - 1)
    return jnp.where(n < max_exact, n, large)


def _bias_by_distance(rel_bias, n):
    return jnp.moveaxis(rel_bias[_t5_bucket(n)], -1, 0).astype(F32)


def _bias_table_body(rb_ref, o_ref, *, t, rows):
    h = pl.program_id(0)
    max_exact = NUM_BUCKETS // 2

    def block(i, carry):
        r0 = pl.multiple_of(i * rows, rows)
        c = lax.broadcasted_iota(jnp.int32, (rows, t), 0) + r0
        r = lax.broadcasted_iota(jnp.int32, (rows, t), 1)
        for idx in range(2):
            n = r - c + idx * t
            nn = jnp.maximum(n, 0)
            scaled = jnp.log(jnp.maximum(nn, 1).astype(F32) / max_exact) / math.log(MAX_DISTANCE / max_exact)
            large = jnp.minimum(max_exact + (scaled * (NUM_BUCKETS - max_exact)).astype(jnp.int32), NUM_BUCKETS - 1)
            bucket = jnp.where(nn < max_exact, nn, large)
            val = jnp.zeros((rows, t), F32)
            for b in range(NUM_BUCKETS):
                val = jnp.where(bucket == b, rb_ref[b, h] * LOG2E, val)
            if idx == 0:
                val = jnp.where(n >= 0, val, NEG)
            o_ref[idx, pl.ds(r0, rows), :] = val
        return carry

    lax.fori_loop(0, t // rows, block, 0)


def _bias_tables(rel_bias, t):
    return pl.pallas_call(
        functools.partial(_bias_table_body, t=t, rows=8),
        out_shape=S((ATT_HEADS, 2, t, t), F32), grid=(ATT_HEADS,),
        in_specs=[BS(memory_space=pltpu.SMEM)],
        out_specs=BS((None, 2, t, t), lambda h: (h, 0, 0, 0)),
        compiler_params=_cparams(("parallel",)), name="bias_tables")(rel_bias.astype(F32))


def _attn_body(lam_ref, cfar_ref, q_ref, k_ref, vt_ref, tab_ref, sw_ref, o_ref,
               q2_ref, s_ref, sb_ref, p_ref, m_ref, l_ref, acc_ref, *, t, rb, post_scale):
    h = pl.program_id(1)
    qi = pl.program_id(2)
    q = q_ref[...]
    lane = lax.broadcasted_iota(jnp.int32, q.shape, 1)
    zero = jnp.zeros_like(q)
    q2_ref[0:t, :] = jnp.where(lane < ATT_DK, q, zero)
    q2_ref[t:2 * t, :] = jnp.where(lane >= ATT_DK, q, zero)
    m_ref[...] = jnp.full(m_ref.shape, -jnp.inf, F32)
    l_ref[...] = jnp.zeros(l_ref.shape, F32)
    acc_ref[...] = jnp.zeros(acc_ref.shape, F32)
    cfar = cfar_ref[h] * LOG2E
    w2 = 2 * t

    def tile(ki, tab_idx):
        s_ref[...] = _dot_nt(k_ref[pl.ds(pl.multiple_of(ki * t, t), t), :], q2_ref[...])
        src_ref = s_ref if tab_idx is None else sb_ref

        def key_max(g, mx):
            rows = pl.ds(pl.multiple_of(g * rb, rb), rb)
            s = s_ref[rows, :]
            if tab_idx is not None:
                tab = tab_ref[tab_idx, rows, :]
                s = s + jnp.concatenate([tab, tab], axis=1)
                sb_ref[rows, :] = s
            for a in range(0, rb, 8):
                mx = jnp.maximum(mx, s[a:a + 8, :])
            return mx

        mx8 = lax.fori_loop(0, t // rb, key_max, jnp.full((8, w2), -jnp.inf, F32), unroll=4)
        tile_max = jnp.max(mx8, axis=0, keepdims=True)
        if tab_idx is None:
            tile_max = tile_max + cfar
        m_old = m_ref[...]
        m_new = jnp.maximum(m_old, tile_max)
        alpha = jnp.exp2(m_old - m_new)
        m_ref[...] = m_new
        shift = jnp.broadcast_to(m_new if tab_idx is not None else m_new - cfar, (rb, w2))

        def key_exp(g, tot):
            rows = pl.ds(pl.multiple_of(g * rb, rb), rb)
            p = jnp.exp2(src_ref[rows, :] - shift)
            p_ref[rows, :] = p.astype(BF16)
            for a in range(0, rb, 8):
                tot = tot + p[a:a + 8, :]
            return tot

        tot8 = lax.fori_loop(0, t // rb, key_exp, jnp.zeros((8, w2), F32), unroll=4)
        l_ref[...] = alpha * l_ref[...] + jnp.sum(tot8, axis=0, keepdims=True)
        acc_ref[...] = alpha * acc_ref[...] + _dot(vt_ref[ki], p_ref[...])

    def far(ki, carry):
        tile(ki, None)
        return carry

    lax.fori_loop(0, jnp.maximum(qi - 1, 0), far, 0)

    @pl.when(qi >= 1)
    def _():
        tile(qi - 1, 1)

    tile(qi, 0)
    ot = acc_ref[...] * (1.0 / l_ref[...])
    o = (ot[:, 0:t] - lam_ref[0] * ot[:, t:w2]).T
    o = o * lax.rsqrt(jnp.mean(o * o, axis=-1, keepdims=True) + EPS) * sw_ref[...] * post_scale
    o_ref[...] = o.astype(o_ref.dtype)


def _attn_prompt(q, k, v, lam, rel_bias, tabs, subln, post_scale, bsz, seq, t):
    nq = seq // t
    cfar = rel_bias[NUM_BUCKETS - 1].astype(F32)
    vt = v.reshape(bsz, nq, t, ATT_HEADS, ATT_DV).transpose(0, 3, 1, 4, 2).reshape(bsz * ATT_HEADS, nq, ATT_DV, t)
    smem = BS(memory_space=pltpu.SMEM)
    return pl.pallas_call(
        functools.partial(_attn_body, t=t, rb=16, post_scale=post_scale),
        out_shape=S((bsz * seq, ATT_W), BF16), grid=(bsz, ATT_HEADS, nq),
        in_specs=[smem, smem,
                  BS((t, ATT_DV), lambda b, h, qi: (b * nq + qi, h)),
                  BS((seq, ATT_DV), lambda b, h, qi: (b, h)),
                  BS((None, nq, ATT_DV, t), lambda b, h, qi: (b * ATT_HEADS + h, 0, 0, 0)),
                  BS((None, 2, t, t), lambda b, h, qi: (h, 0, 0, 0)),
                  BS((1, ATT_DV), lambda b, h, qi: (0, 0))],
        out_specs=BS((t, ATT_DV), lambda b, h, qi: (b * nq + qi, h)),
        scratch_shapes=[pltpu.VMEM((2 * t, ATT_DV), BF16), pltpu.VMEM((t, 2 * t), F32), pltpu.VMEM((t, 2 * t), F32),
                        pltpu.VMEM((t, 2 * t), BF16), pltpu.VMEM((1, 2 * t), F32), pltpu.VMEM((1, 2 * t), F32),
                        pltpu.VMEM((ATT_DV, 2 * t), F32)],
        compiler_params=_cparams(("parallel", "parallel", "arbitrary")), name="attn_prompt",
    )(lam.reshape(1), cfar, q, k, vt, tabs, subln.reshape(1, ATT_DV))


def _decode_body(pt_ref, lam_ref, q8_ref, kn_ref, vn_ref, btab_ref, sw_ref, *rest, pp, n_steps, post_scale):
    k_refs = rest[:pp]
    v_refs = rest[pp:2 * pp]
    o_ref = rest[2 * pp]
    m_ref, l_ref, acc_ref = rest[2 * pp + 1:]
    s_idx = pl.program_id(1)

    @pl.when(s_idx == 0)
    def _():
        m_ref[...] = jnp.full(m_ref.shape, -jnp.inf, F32)
        l_ref[...] = jnp.zeros(l_ref.shape, F32)
        acc_ref[...] = jnp.zeros(acc_ref.shape, F32)

    q8 = q8_ref[...]
    n_pages = n_steps * pp
    scores = []
    for j in range(pp):
        page = s_idx * pp + j
        bias = jnp.where(page == n_pages - 1, btab_ref[0], btab_ref[1])
        scores.append(_dot_nt(q8, k_refs[j][...].astype(BF16)) + bias)
    s_all = jnp.concatenate(scores, axis=1)
    m_old = m_ref[...]
    m_new = jnp.maximum(m_old, jnp.max(s_all, axis=1, keepdims=True))
    alpha = jnp.exp2(m_old - m_new)
    p_all = jnp.exp2(s_all - m_new)
    l_ref[...] = alpha * l_ref[...] + jnp.sum(p_all, axis=1, keepdims=True)
    ps = p_all.shape[1] // pp
    p16 = p_all.astype(BF16)
    pv = _dot(p16[:, :ps], v_refs[0][...].astype(BF16))
    for j in range(1, pp):
        pv = pv + _dot(p16[:, j * ps:(j + 1) * ps], v_refs[j][...].astype(BF16))
    acc_ref[...] = alpha * acc_ref[...] + pv
    m_ref[...] = m_new

    @pl.when(s_idx == n_steps - 1)
    def _():
        s_self = (jnp.sum(q8.astype(F32) * kn_ref[...].astype(F32), axis=1, keepdims=True)
                  + btab_ref[2][:, 0:1])
        m_o = m_ref[...]
        m_f = jnp.maximum(m_o, s_self)
        a_o = jnp.exp2(m_o - m_f)
        p_s = jnp.exp2(s_self - m_f)
        l_f = a_o * l_ref[...] + p_s
        acc = a_o * acc_ref[...] + p_s * vn_ref[...].astype(F32)
        o8 = acc * (1.0 / l_f)
        lam = lam_ref[0]
        outs = []
        for h in range(ATT_HEADS):
            o = o8[2 * h:2 * h + 1, :] - lam * o8[2 * h + 1:2 * h + 2, :]
            o = o * lax.rsqrt(jnp.mean(o * o, axis=-1, keepdims=True) + EPS) * sw_ref[...] * post_scale
            outs.append(o)
        o_ref[...] = jnp.concatenate(outs, axis=1).astype(o_ref.dtype)


def _attn_decode(layer, q, k_new, v_new, cache_k, cache_v, page_table, lam, rel_bias, subln, post_scale, pp):
    bsz = q.shape[0]
    depth, n_phys, page = cache_k.shape[0], cache_k.shape[1], cache_k.shape[2]
    n_pages = page_table.shape[1]
    n_steps = n_pages // pp
    prow = page * ATT_HEADS
    ck = cache_k.reshape(depth, n_phys, prow, ATT_DV)
    cv = cache_v.reshape(depth, n_phys, prow, ATT_DV)
    rows = jnp.arange(2 * ATT_HEADS)
    qh = jnp.repeat(q.reshape(bsz, ATT_HEADS, ATT_DV), 2, axis=1)
    q8 = jnp.where((jnp.arange(ATT_DV)[None, :] // ATT_DK == rows[:, None] % 2)[None], qh, jnp.zeros((), q.dtype))
    kn8 = jnp.repeat(k_new.reshape(bsz, ATT_HEADS, ATT_DV), 2, axis=1)
    vn8 = jnp.repeat(v_new.reshape(bsz, ATT_HEADS, ATT_DV), 2, axis=1)
    same = (jnp.arange(prow)[None, :] % ATT_HEADS) == (rows[:, None] // 2)
    d_last = page - jnp.arange(prow) // ATT_HEADS
    b_last = jnp.repeat(_bias_by_distance(rel_bias, d_last), 2, axis=0)
    b_far = jnp.broadcast_to(jnp.repeat(rel_bias[NUM_BUCKETS - 1].astype(F32), 2)[:, None], (2 * ATT_HEADS, prow))
    b_self = jnp.broadcast_to(jnp.repeat(rel_bias[0].astype(F32), 2)[:, None], (2 * ATT_HEADS, prow))
    btab = jnp.stack([jnp.where(same, b_last * LOG2E, NEG), jnp.where(same, b_far * LOG2E, NEG), b_self * LOG2E])

    def page_spec(j):
        return BS((None, None, prow, ATT_DV), lambda b, s, pt: (layer, pt[b, s * pp + j], 0, 0))

    const = lambda shp: BS(shp, lambda b, s, pt: tuple(0 for _ in shp))
    per_b = BS((None, 2 * ATT_HEADS, ATT_DV), lambda b, s, pt: (b, 0, 0))
    grid_spec = pltpu.PrefetchScalarGridSpec(
        num_scalar_prefetch=1, grid=(bsz, n_steps),
        in_specs=[BS(memory_space=pltpu.SMEM), per_b, per_b, per_b,
                  const((3, 2 * ATT_HEADS, prow)), const((1, ATT_DV))]
        + [page_spec(j) for j in range(pp)] + [page_spec(j) for j in range(pp)],
        out_specs=BS((None, 1, ATT_W), lambda b, s, pt: (b, 0, 0)),
        scratch_shapes=[pltpu.VMEM((2 * ATT_HEADS, 1), F32), pltpu.VMEM((2 * ATT_HEADS, 1), F32),
                        pltpu.VMEM((2 * ATT_HEADS, ATT_DV), F32)])
    out = pl.pallas_call(
        functools.partial(_decode_body, pp=pp, n_steps=n_steps, post_scale=post_scale),
        out_shape=S((bsz, 1, ATT_W), BF16), grid_spec=grid_spec,
        compiler_params=_cparams(("parallel", "arbitrary")), name="attn_decode",
    )(page_table, lam.reshape(1), q8, kn8, vn8, btab, subln.reshape(1, ATT_DV), *([ck] * pp), *([cv] * pp))
    return out.reshape(bsz, ATT_W)


def _causal_conv(x, prev8, cw_ref, c0, c1):
    acc = x * cw_ref[CONV_K - 1:CONV_K, c0:c1]
    row8 = lax.broadcasted_iota(jnp.int32, prev8.shape, 0)
    for s in range(1, CONV_K):
        r = pltpu.roll(x, s, axis=0)
        pr = pltpu.roll(prev8, s, axis=0)
        head = jnp.where(row8 < s, pr, r[:8])
        sh = jnp.concatenate([head, r[8:]], axis=0) if x.shape[0] > 8 else head
        acc = acc + sh * cw_ref[CONV_K - 1 - s:CONV_K - s, c0:c1]
    return acc


def _tri(cl):
    ii = lax.broadcasted_iota(jnp.int32, (cl, cl), 0)
    jj = lax.broadcasted_iota(jnp.int32, (cl, cl), 1)
    return ii, jj


def _ssd_body(z_ref, xbc_ref, ps_ref, cs0_ref, h0_ref, cw_ref, cb_ref, brow_ref, bcol_ref, arow_ref, acol_ref,
              dexp_ref, nw_ref, ex_ref, y_ref, hl_ref, st_ref, halo_ref, yacc_ref, *, cl, nc, valid_len):
    c = pl.program_id(1)

    @pl.when(c == 0)
    def _():
        st_ref[...] = h0_ref[...]
        halo_ref[...] = cs0_ref[...]

    n_pairs = SSM_HEADS // 2
    pw = 2 * SSM_HEAD_DIM
    states = [st_ref[j] for j in range(n_pairs)]
    x_raw = xbc_ref[...]
    conv = _causal_conv(x_raw, halo_ref[...], cw_ref, 0, SSM_CONV_DIM)
    xbc = _silu(conv + cb_ref[...])
    x = xbc[:, :SSM_D_INNER]
    ps = ps_ref[...]
    step_c = _softplus(ps + brow_ref[...])
    step_r = _softplus(ps.T + bcol_ref[...])
    if valid_len is not None:
        t_c = lax.broadcasted_iota(jnp.int32, step_c.shape, 0) + c * cl
        t_r = lax.broadcasted_iota(jnp.int32, step_r.shape, 1) + c * cl
        step_c = jnp.where(t_c < valid_len, step_c, 0.0)
        step_r = jnp.where(t_r < valid_len, step_r, 0.0)
    la_c = step_c * -jnp.exp(arow_ref[...])
    la_r = (step_r * -jnp.exp(acol_ref[...]))[PS_DT:PS_DT + SSM_HEADS, :]
    ii, jj = _tri(cl)
    incl = ii >= jj
    acs_c = _dot_hi(incl.astype(F32), la_c)
    acs_r = _dot_hi(la_r, (ii <= jj).astype(F32))
    last = acs_c[cl - 1:cl, :]
    e_last = jnp.exp(last)

    def expand(a):
        hi, lo = _split(a)
        return _dot(hi, ex_ref[...]) + _dot(lo, ex_ref[...])

    xs = x * expand(step_c)
    xe = (xs * expand(jnp.exp(last - acs_c))).astype(BF16)
    e_acs = expand(jnp.exp(acs_c))
    xs16 = xs.astype(BF16)
    lane = lax.broadcasted_iota(jnp.int32, (cl, pw), 1)
    srow = lax.broadcasted_iota(jnp.int32, (pw, 1), 0)
    ppg = n_pairs // SSM_GROUPS
    for g in range(SSM_GROUPS):
        b_g = xbc[:, SSM_D_INNER + g * SSM_STATE:SSM_D_INNER + (g + 1) * SSM_STATE].astype(BF16)
        c0 = SSM_D_INNER + SSM_GROUPS * SSM_STATE + g * SSM_STATE
        c_g = xbc[:, c0:c0 + SSM_STATE].astype(BF16)
        cb = _dot_nt(c_g, b_g)
        for r in range(ppg):
            j = g * ppg + r
            cols = slice(j * pw, (j + 1) * pw)
            y_heads = []
            for h in (2 * j, 2 * j + 1):
                dec = jnp.where(incl, jnp.exp(jnp.where(incl, acs_c[:, h:h + 1] - acs_r[h:h + 1, :], 0.0)), 0.0)
                y_heads.append(_dot((cb * dec).astype(BF16), xs16[:, cols]))
            y_diag = jnp.where(lane < SSM_HEAD_DIM, y_heads[0], y_heads[1])
            hst = states[j]
            y_off = _dot_nt(c_g, hst.astype(BF16)) * e_acs[:, cols]
            keep = jnp.where(srow < SSM_HEAD_DIM, e_last[:, 2 * j:2 * j + 1], e_last[:, 2 * j + 1:2 * j + 2])
            states[j] = hst * keep + _dot_tn(xe[:, cols], b_g)
            yacc_ref[:, cols] = y_diag + y_off
    for j in range(n_pairs):
        st_ref[j] = states[j]
    halo_ref[...] = x_raw[cl - 8:, :]
    y = (yacc_ref[...] + dexp_ref[...] * x) * _silu(z_ref[...])
    gw = SSM_D_INNER // SSM_GROUPS
    parts = []
    for g in range(SSM_GROUPS):
        yg = y[:, g * gw:(g + 1) * gw]
        parts.append(yg * lax.rsqrt(jnp.mean(yg * yg, axis=-1, keepdims=True) + EPS))
    y_ref[...] = (jnp.concatenate(parts, axis=1) * nw_ref[...]).astype(y_ref.dtype)

    @pl.when(c == nc - 1)
    def _():
        hl_ref[...] = st_ref[...]


def _ssd(proj, ps, conv0, h0, conv_w, conv_b, vec, ssm_d, ssm_norm, bsz, seq, cl, valid_len):
    nc = seq // cl
    n = bsz * seq
    brow, bcol, arow, acol = vec
    cs0 = jnp.pad(conv0, ((0, 0), (8 - (CONV_K - 1), 0), (0, 0)))
    dexp = jnp.repeat(ssm_d, SSM_HEAD_DIM).reshape(1, SSM_D_INNER)
    expand = (jnp.arange(SSM_D_INNER)[None, :] // SSM_HEAD_DIM == jnp.arange(PS_W)[:, None] - PS_DT).astype(BF16)
    n_pairs, pw = SSM_HEADS // 2, 2 * SSM_HEAD_DIM
    h0p = h0.reshape(bsz, n_pairs, pw, SSM_STATE)
    const = lambda shp: BS(shp, lambda b, c: tuple(0 for _ in shp))
    y, h_last = pl.pallas_call(
        functools.partial(_ssd_body, cl=cl, nc=nc, valid_len=valid_len),
        out_shape=(S((n, SSM_D_INNER), BF16), S(h0p.shape, F32)), grid=(bsz, nc),
        in_specs=[BS((cl, SSM_D_INNER), lambda b, c: (b * nc + c, C_SZ // SSM_D_INNER)),
                  BS((cl, SSM_CONV_DIM), lambda b, c: (b * nc + c, C_XBC // SSM_CONV_DIM)),
                  BS((cl, PS_W), lambda b, c: (b * nc + c, 0)),
                  BS((None, 8, SSM_CONV_DIM), lambda b, c: (b, 0, 0)),
                  BS((None, n_pairs, pw, SSM_STATE), lambda b, c: (b, 0, 0, 0)),
                  const((CONV_K, SSM_CONV_DIM)), const((1, SSM_CONV_DIM)),
                  const((1, PS_W)), const((PS_W, 1)), const((1, PS_W)), const((PS_W, 1)),
                  const((1, SSM_D_INNER)), const((1, SSM_D_INNER)), const((PS_W, SSM_D_INNER))],
        out_specs=(BS((cl, SSM_D_INNER), lambda b, c: (b * nc + c, 0)),
                   BS((None, n_pairs, pw, SSM_STATE), lambda b, c: (b, 0, 0, 0))),
        scratch_shapes=[pltpu.VMEM((n_pairs, pw, SSM_STATE), F32), pltpu.VMEM((8, SSM_CONV_DIM), F32),
                        pltpu.VMEM((cl, SSM_D_INNER), F32)],
        compiler_params=_cparams(("parallel", "arbitrary")), name="ssd",
    )(proj, proj, ps, cs0, h0p, conv_w, conv_b.reshape(1, -1), brow, bcol, arow, acol, dexp,
      ssm_norm.reshape(1, SSM_D_INNER), expand)
    return y, h_last.reshape(h0.shape)


def _unit_lower_inverses(mats, ii, jj, cl):
    eye = (ii == jj).astype(F32)
    pair = ((ii >> 1) == (jj >> 1)) & (ii > jj)
    invs = [eye - jnp.where(pair, a, 0.0) for a in mats]
    s = 2
    while s < cl:
        blk = ((ii // (2 * s)) == (jj // (2 * s))) & ((ii % (2 * s)) >= s) & ((jj % (2 * s)) < s)
        xs = [_dot3(inv, jnp.where(blk, a, 0.0)) for inv, a in zip(invs, mats)]
        invs = [inv - _dot3(x, inv) for inv, x in zip(invs, xs)]
        s *= 2
    return invs


def _gdn_body(q_ref, k_ref, v_ref, z_ref, ps_ref, cs0_ref, s0_ref, cw_ref, brow_ref, bcol_ref, arow_ref, acol_ref,
              nw_ref, o_ref, sl_ref, st_ref, halo_ref, *, cl, cps, nc, valid_len):
    c = pl.program_id(1)

    @pl.when(c == 0)
    def _():
        st_ref[...] = s0_ref[...]
        halo_ref[...] = cs0_ref[...]

    w = DN_W
    rows = cl * cps
    halo = halo_ref[...]
    states = [st_ref[h] for h in range(DN_HEADS)]
    raws = (q_ref[...], k_ref[...], v_ref[...])
    q, k, v = (_silu(_causal_conv(raws[p], halo[:, p * w:(p + 1) * w], cw_ref, p * w, (p + 1) * w)) for p in range(3))
    z = z_ref[...]
    ps = ps_ref[...]
    g_c = -jnp.exp(arow_ref[...]) * _softplus(ps + brow_ref[...])
    g_r = -jnp.exp(acol_ref[...]) * _softplus(ps.T + bcol_ref[...])
    beta = _sigmoid(ps)
    if valid_len is not None:
        t_c = lax.broadcasted_iota(jnp.int32, g_c.shape, 0) + c * rows
        t_r = lax.broadcasted_iota(jnp.int32, g_r.shape, 1) + c * rows
        g_c = jnp.where(t_c < valid_len, g_c, 0.0)
        beta = jnp.where(t_c < valid_len, beta, 0.0)
        g_r = jnp.where(t_r < valid_len, g_r, 0.0)
    ii, jj = _tri(cl)
    incl = ii >= jj
    strict = ii > jj
    lower_ones = incl.astype(F32)
    upper_ones = (ii <= jj).astype(F32)
    off = PS_DECAY - 16
    nw = nw_ref[...]
    pre = []
    for ci in range(cps):
        rs = slice(ci * cl, (ci + 1) * cl)
        gcs_c = _dot_hi(lower_ones, g_c[rs, 16:24])
        gcs_r = _dot_hi(g_r[16:24, rs], upper_ones)
        for h in range(DN_HEADS):
            sl = slice(h * DN_DK, (h + 1) * DN_DK)
            qh = q[rs, sl]
            kh = k[rs, sl]
            qh = qh * lax.rsqrt(jnp.sum(qh * qh, axis=-1, keepdims=True) + EPS) * (DN_DK ** -0.5)
            kh = kh * lax.rsqrt(jnp.sum(kh * kh, axis=-1, keepdims=True) + EPS)
            bh = beta[rs, PS_BETA + h:PS_BETA + h + 1]
            col = gcs_c[:, off + h:off + h + 1]
            row = gcs_r[off + h:off + h + 1, :]
            last = gcs_c[cl - 1:cl, off + h:off + h + 1]
            e_col = jnp.exp(col)
            kb = kh * bh
            pre.append(dict(
                dec=jnp.where(incl, jnp.exp(jnp.where(incl, col - row, 0.0)), 0.0),
                q16=qh.astype(BF16), k16=kh.astype(BF16), kb16=kb.astype(BF16),
                vb=v[rs, sl] * bh, kbe=kb * e_col,
                qd16=(qh * e_col).astype(BF16),
                kd16=(kh * jnp.exp(last - col)).astype(BF16),
                g_end=jnp.exp(last),
                gate=nw * _silu(z[rs, sl])))
    lowers = [jnp.where(strict, _dot_nt(t["kb16"], t["k16"]) * t["dec"], 0.0) for t in pre]
    qks = [(_dot_nt(t["q16"], t["k16"]) * t["dec"]).astype(BF16) for t in pre]
    tinvs = _unit_lower_inverses(lowers, ii, jj, cl)
    us = [_dot3(ti, t["vb"]) for ti, t in zip(tinvs, pre)]
    ws = [_dot3(ti, t["kbe"]).astype(BF16) for ti, t in zip(tinvs, pre)]
    outs = []
    for ci in range(cps):
        items = range(ci * DN_HEADS, (ci + 1) * DN_HEADS)
        s16 = [st.astype(BF16) for st in states]
        v16 = [(us[i] - _dot(ws[i], s16[h])).astype(BF16) for h, i in enumerate(items)]
        os_ = [_dot(pre[i]["qd16"], s16[h]) + _dot(qks[i], v16[h]) for h, i in enumerate(items)]
        states = [states[h] * pre[i]["g_end"] + _dot_tn(pre[i]["kd16"], v16[h]) for h, i in enumerate(items)]
        outs += [o * lax.rsqrt(jnp.mean(o * o, axis=-1, keepdims=True) + EPS) * pre[i]["gate"]
                 for o, i in zip(os_, items)]
    for ci in range(cps):
        for h in range(DN_HEADS):
            o_ref[ci * cl:(ci + 1) * cl, h * DN_DV:(h + 1) * DN_DV] = outs[ci * DN_HEADS + h].astype(o_ref.dtype)
    for h in range(DN_HEADS):
        st_ref[h] = states[h]
    for p in range(3):
        halo_ref[:, p * w:(p + 1) * w] = raws[p][rows - 8:, :]

    @pl.when(c == nc - 1)
    def _():
        sl_ref[...] = st_ref[...]


def _gdn(proj, ps, conv0, s0, conv_w, vec, dn_norm, bsz, seq, cl, cps, valid_len):
    rows = cl * cps
    nc = seq // rows
    n = bsz * seq
    brow, bcol, arow, acol = vec
    cs0 = jnp.pad(conv0, ((0, 0), (8 - (CONV_K - 1), 0), (0, 0)))
    w = DN_W
    const = lambda shp: BS(shp, lambda b, c: tuple(0 for _ in shp))
    blk = lambda col: BS((rows, w), lambda b, c: (b * nc + c, col // w))
    o, s_last = pl.pallas_call(
        functools.partial(_gdn_body, cl=cl, cps=cps, nc=nc, valid_len=valid_len),
        out_shape=(S((n, w), BF16), S(s0.shape, F32)), grid=(bsz, nc),
        in_specs=[blk(C_DQ), blk(C_DK), blk(C_DV), blk(C_DZ),
                  BS((rows, PS_W), lambda b, c: (b * nc + c, 0)),
                  BS((None, 8, DN_CONV_DIM), lambda b, c: (b, 0, 0)),
                  BS((None, DN_HEADS, DN_DK, DN_DV), lambda b, c: (b, 0, 0, 0)),
                  const((CONV_K, DN_CONV_DIM)),
                  const((1, PS_W)), const((PS_W, 1)), const((1, PS_W)), const((PS_W, 1)), const((1, DN_DV))],
        out_specs=(BS((rows, w), lambda b, c: (b * nc + c, 0)),
                   BS((None, DN_HEADS, DN_DK, DN_DV), lambda b, c: (b, 0, 0, 0))),
        scratch_shapes=[pltpu.VMEM((DN_HEADS, DN_DK, DN_DV), F32), pltpu.VMEM((8, DN_CONV_DIM), F32)],
        compiler_params=_cparams(("parallel", "arbitrary")), name="gdn",
    )(proj, proj, proj, proj, ps, cs0, s0, conv_w, brow, bcol, arow, acol, dn_norm.reshape(1, DN_DV))
    return o, s_last


def _router_body(h_ref, lw_ref, rw_ref, rb_ref, meta_ref):
    x = h_ref[...]
    xn = x * lax.rsqrt(jnp.mean(x * x, axis=-1, keepdims=True) + EPS) * lw_ref[...]
    logits = _dot_hi(xn, rw_ref[...]) + rb_ref[...]
    lane = lax.broadcasted_iota(jnp.int32, logits.shape, 1)
    big = jnp.int32(logits.shape[1])
    m1 = jnp.max(logits, axis=1, keepdims=True)
    i1 = jnp.min(jnp.where(logits == m1, lane, big), axis=1, keepdims=True)
    rest = jnp.where(lane == i1, NEG, logits)
    m2 = jnp.max(rest, axis=1, keepdims=True)
    i2 = jnp.min(jnp.where(rest == m2, lane, big), axis=1, keepdims=True)
    e2 = jnp.exp(m2 - m1)
    g1 = 1.0 / (1.0 + e2)
    g2 = e2 * g1
    meta = jnp.where(lane == 0, i1.astype(F32), jnp.where(lane == 1, i2.astype(F32),
                     jnp.where(lane == 2, g1, jnp.where(lane == 3, g2, 0.0))))
    meta_ref[...] = meta


def _router(h, ln_w, router_w, router_b, tm):
    n, d = h.shape
    rw = jnp.pad(router_w, ((0, 0), (0, 128 - N_EXPERTS)))
    rb = jnp.pad(router_b, (0, 128 - N_EXPERTS), constant_values=NEG).reshape(1, 128)
    const = lambda shp: BS(shp, lambda i: (0, 0))
    return pl.pallas_call(
        _router_body, out_shape=S((n, 128), F32), grid=(n // tm,),
        in_specs=[BS((tm, d), lambda i: (i, 0)), const((1, d)), const((d, 128)), const((1, 128))],
        out_specs=BS((tm, 128), lambda i: (i, 0)),
        compiler_params=_cparams(("parallel",)), name="router")(h, ln_w.reshape(1, d), rw, rb)


def _experts_body(be_ref, tok_ref, nused_ref, h_hbm, lw_ref, wg_ref, wu_ref, wd_ref, y_ref,
                  xg_ref, xb_ref, acc_ref, sem, *, tm, nf):
    i = pl.program_id(0)
    f = pl.program_id(1)
    used = i < nused_ref[0]
    slot = lax.rem(i, 2)

    def gather(blk, dst):
        def start(r, carry):
            pltpu.make_async_copy(h_hbm.at[pl.ds(tok_ref[blk * tm + r], 1), :],
                                  xg_ref.at[dst, pl.ds(r, 1), :], sem.at[dst]).start()
            return carry

        lax.fori_loop(0, tm, start, 0, unroll=8)

    @pl.when((i == 0) & (f == 0))
    def _():
        gather(0, 0)

    @pl.when(used & (f == 0))
    def _():
        pltpu.make_async_copy(h_hbm.at[pl.ds(0, tm), :], xg_ref.at[slot], sem.at[slot]).wait()
        x = xg_ref[slot]
        xn = x * lax.rsqrt(jnp.mean(x * x, axis=-1, keepdims=True) + EPS) * lw_ref[...]
        xb_ref[...] = xn.astype(BF16)
        acc_ref[...] = jnp.zeros(acc_ref.shape, F32)

    chunk = tm // nf

    @pl.when(used)
    def _():
        nxt = jnp.minimum(i + 1, nused_ref[0] - 1)
        for r in range(chunk):
            pltpu.make_async_copy(h_hbm.at[pl.ds(tok_ref[nxt * tm + f * chunk + r], 1), :],
                                  xg_ref.at[1 - slot, pl.ds(f * chunk + r, 1), :], sem.at[1 - slot]).start()
        xb = xb_ref[...]
        hid = _silu(_dot(xb, wg_ref[...].astype(BF16))) * _dot(xb, wu_ref[...].astype(BF16))
        acc_ref[...] += _dot(hid.astype(BF16), wd_ref[...].astype(BF16))

    @pl.when((i == nused_ref[0] - 1) & (f == nf - 1))
    def _():
        pltpu.make_async_copy(h_hbm.at[pl.ds(0, tm), :], xg_ref.at[1 - slot], sem.at[1 - slot]).wait()

    @pl.when(used & (f == nf - 1))
    def _():
        y_ref[...] = acc_ref[...]

    @pl.when(jnp.logical_not(used) & (f == nf - 1))
    def _():
        y_ref[...] = jnp.zeros(y_ref.shape, F32)


def _experts(h, ln_w, block_expert, row_token, n_used, wg, wu, wd, tm, tf):
    n, d = h.shape
    rows = row_token.shape[0]
    n_blocks = rows // tm
    (wg, layer), (wu, _), (wd, _) = wg, wu, wd
    ff = wg.shape[3]
    nf = ff // tf
    grid_spec = pltpu.PrefetchScalarGridSpec(
        num_scalar_prefetch=3, grid=(n_blocks, nf),
        in_specs=[BS(memory_space=pl.ANY),
                  BS((1, d), lambda i, f, be, tok, nu: (0, 0)),
                  BS((None, None, d, tf), lambda i, f, be, tok, nu: (layer, be[i], 0, f)),
                  BS((None, None, d, tf), lambda i, f, be, tok, nu: (layer, be[i], 0, f)),
                  BS((None, None, tf, d), lambda i, f, be, tok, nu: (layer, be[i], f, 0))],
        out_specs=BS((tm, d), lambda i, f, be, tok, nu: (i, 0)),
        scratch_shapes=[pltpu.VMEM((2, tm, d), F32), pltpu.VMEM((tm, d), BF16), pltpu.VMEM((tm, d), F32),
                        pltpu.SemaphoreType.DMA((2,))])
    return pl.pallas_call(
        functools.partial(_experts_body, tm=tm, nf=nf),
        out_shape=S((rows, d), F32), grid_spec=grid_spec,
        compiler_params=_cparams(("arbitrary", "arbitrary")), name="experts",
    )(block_expert, row_token, n_used, h, ln_w.reshape(1, d), wg, wu, wd)


def _combine_body(pos_ref, y_hbm, h_ref, meta_ref, o_ref, ya_ref, yb_ref, sem, *, tc, n_steps):
    i = pl.program_id(0)
    slot = lax.rem(i, 2)

    def gather(step, dst):
        def start(r, carry):
            t = step * tc + r
            pltpu.make_async_copy(y_hbm.at[pl.ds(pos_ref[2 * t], 1), :], ya_ref.at[dst, pl.ds(r, 1), :],
                                  sem.at[dst]).start()
            pltpu.make_async_copy(y_hbm.at[pl.ds(pos_ref[2 * t + 1], 1), :], yb_ref.at[dst, pl.ds(r, 1), :],
                                  sem.at[dst]).start()
            return carry

        lax.fori_loop(0, tc, start, 0, unroll=8)

    @pl.when(i == 0)
    def _():
        gather(0, 0)

    @pl.when(i + 1 < n_steps)
    def _():
        gather(i + 1, 1 - slot)

    pltpu.make_async_copy(y_hbm.at[pl.ds(0, tc), :], ya_ref.at[slot], sem.at[slot]).wait()
    pltpu.make_async_copy(y_hbm.at[pl.ds(0, tc), :], yb_ref.at[slot], sem.at[slot]).wait()
    meta = meta_ref[...]
    o_ref[...] = h_ref[...] + (meta[:, 2:3] * ya_ref[slot] + meta[:, 3:4] * yb_ref[slot])


def _combine(h, y_sorted, pos, meta, tc):
    n, d = h.shape
    n_steps = n // tc
    grid_spec = pltpu.PrefetchScalarGridSpec(
        num_scalar_prefetch=1, grid=(n_steps,),
        in_specs=[BS(memory_space=pl.ANY), BS((tc, d), lambda i, pos: (i, 0)), BS((tc, 128), lambda i, pos: (i, 0))],
        out_specs=BS((tc, d), lambda i, pos: (i, 0)),
        scratch_shapes=[pltpu.VMEM((2, tc, d), F32), pltpu.VMEM((2, tc, d), F32), pltpu.SemaphoreType.DMA((2,))])
    return pl.pallas_call(
        functools.partial(_combine_body, tc=tc, n_steps=n_steps), out_shape=S((n, d), F32), grid_spec=grid_spec,
        compiler_params=_cparams(("arbitrary",)), name="moe_combine")(pos, y_sorted, h, meta)


def _moe(h, ln_w, router_w, router_b, wg, wu, wd, tm_r, tm, tf, tc):
    n, d = h.shape
    meta = _router(h, ln_w, router_w, router_b, tm_r)
    idx = meta[:, 0:2].astype(jnp.int32)
    member = jnp.sum(jax.nn.one_hot(idx, N_EXPERTS, dtype=jnp.int32), axis=1)
    before = jnp.cumsum(member, axis=0) - member
    counts = jnp.sum(member, axis=0)
    padded = (counts + tm - 1) // tm * tm
    pad_ends = jnp.cumsum(padded)
    pad_starts = pad_ends - padded
    pos = (pad_starts[idx] + jnp.take_along_axis(before, idx, axis=1)).astype(jnp.int32)
    n_blocks = -(-(2 * n + N_EXPERTS * (tm - 1)) // tm)
    rows = n_blocks * tm
    tok = jnp.broadcast_to(jnp.arange(n, dtype=jnp.int32)[:, None], (n, 2))
    row_token = jnp.zeros((rows,), jnp.int32).at[pos.reshape(-1)].set(tok.reshape(-1))
    n_used = (pad_ends[-1] // tm).astype(jnp.int32).reshape(1)
    blk_start = jnp.minimum(jnp.arange(n_blocks, dtype=jnp.int32), n_used[0] - 1) * tm
    block_expert = jnp.minimum(jnp.searchsorted(pad_ends, blk_start, side="right"), N_EXPERTS - 1).astype(jnp.int32)
    y_sorted = _experts(h, ln_w, block_expert, row_token, n_used, wg, wu, wd, tm, tf)
    return _combine(h, y_sorted, pos.reshape(-1), meta, tc)


def _layer(i, h, p, lw, cfg, att_fn, ssm_conv0, ssm_h0, dn_conv0, dn_s0):
    bsz, seq, tm, cl_ssd, cl_gdn, pad_to, valid_len = (cfg[k] for k in
                                                      ("bsz", "seq", "tm", "cl_ssd", "cl_gdn", "pad_to", "valid_len"))
    n = bsz * seq
    xn = _rmsnorm(h, lw["ln_mix"], tm)
    ident = lambda d, e: d[0]
    tn = 512
    rot = PROJ_A_ROT // tn
    n_a = PROJ_A_W // tn
    tm_in = _pick(n, 2 * tm)
    proj_a = _fused_matmul([xn], [(0, lw["w_in"], 0)], ident, [], F32, PROJ_A_W, tm_in, tn, "in_proj_a",
                           out_tile=lambda j: lax.rem(j + (n_a - rot), n_a))
    proj_b = _fused_matmul([xn], [(0, lw["w_tail"], 0)], ident, [], F32, PROJ_B_W, tm_in, tn, "in_proj_b")
    ps = _fused_matmul([xn], [(0, lw["w_small"], 0)], ident, [], F32, PS_W, tm, PS_W, "in_proj_small")

    q16, k16, v16, k32, v32 = _qkv_prep(proj_a, lw["q_norm"], lw["k_norm"], tm)
    att = att_fn(q16, k16, v16)

    if pad_to is None:
        pa_r, pb_r, ps_r, seq_r = proj_a, proj_b, ps, seq
    else:
        seq_r = pad_to
        padded = lambda a: jnp.pad(a.reshape(bsz, seq, -1), ((0, 0), (0, pad_to - seq), (0, 0))).reshape(bsz * pad_to, -1)
        pa_r, pb_r, ps_r = padded(proj_a), padded(proj_b), padded(ps)
    y, ssm_h1 = _ssd(pa_r, ps_r, ssm_conv0, ssm_h0, lw["ssm_conv_w"], lw["ssm_conv_b"], lw["vec"], lw["ssm_d"],
                     lw["ssm_norm"], bsz, seq_r, cl_ssd, valid_len)
    o, dn_s1 = _gdn(pb_r, ps_r, dn_conv0, dn_s0, lw["dn_conv_w"], lw["vec"], lw["dn_norm"], bsz, seq_r, cl_gdn,
                    cfg["cps_gdn"], valid_len)
    if pad_to is not None:
        y = y.reshape(bsz, pad_to, -1)[:, :seq].reshape(n, -1)
        o = o.reshape(bsz, pad_to, -1)[:, :seq].reshape(n, -1)

    def merge(d, e):
        return _sigmoid(d[0]) * d[3] + _sigmoid(d[1]) * d[4] + _sigmoid(d[2]) * d[5]

    d_m = D_MODEL
    merged = _fused_matmul([xn, att, y, o],
                           [(0, lw["w_gate"], 0), (0, lw["w_gate"], d_m), (0, lw["w_gate"], 2 * d_m),
                            (1, lw["w_up_att"], 0), (2, lw["w_up_ssm"], 0), (3, lw["w_up_dn"], 0)],
                           merge, [], BF16, d_m, tm, 512, "merge")
    resid = lambda d, e: e[0] + d[0]
    h = _fused_matmul([merged], [(0, lw["w_out"], 0)], resid, [h], F32, d_m, tm, 512, "out_proj")

    if i % 2 == 0:
        hn = _rmsnorm(h, lw["ln_ffn"], tm)
        d_ff = lw["d_ff"]
        ff = _fused_matmul([hn], [(0, lw["ffn_w_gate"], 0), (0, lw["ffn_w_up"], 0)],
                           lambda d, e: _silu(d[0]) * d[1], [], BF16, d_ff, min(tm, 512), 1408, "ffn_up")
        h = _fused_matmul([ff], [(0, lw["ffn_w_down"], 0)], resid, [h], F32, d_m, min(tm, 512), 512, "ffn_down")
    else:
        h = _moe(h, lw["ln_ffn"], lw["router_w"], lw["router_b"], lw["exp_w_gate"], lw["exp_w_up"], lw["exp_w_down"],
                 min(tm, 512), cfg["tm_moe"], TF_MOE, cfg["tc"])

    hn = _rmsnorm(h, lw["ln_ple"], tm)
    h = _fused_matmul([hn, p.astype(BF16)], [(0, lw["ple_w_gate"], 0), (1, lw["ple_w_proj"], 0)],
                      lambda d, e: e[0] + _sigmoid(d[0]) * d[1], [h], F32, d_m, tm, 512, "ple")

    keep = CONV_K - 1
    tail = max(seq - keep, 0)
    xbc_raw = proj_a.reshape(bsz, seq, -1)[:, tail:, C_XBC:C_XBC + SSM_CONV_DIM]
    dn_raw = proj_b.reshape(bsz, seq, -1)[:, tail:, C_DQ:C_DQ + DN_CONV_DIM]
    if seq >= keep:
        ssm_conv1, dn_conv1 = xbc_raw, dn_raw
    else:
        ssm_conv1 = jnp.concatenate([ssm_conv0, xbc_raw], axis=1)[:, -keep:]
        dn_conv1 = jnp.concatenate([dn_conv0, dn_raw], axis=1)[:, -keep:]
    new_k = k32.reshape(bsz, seq, ATT_HEADS, 2 * ATT_DK)
    new_v = v32.reshape(bsz, seq, ATT_HEADS, ATT_DV)
    return h, (new_k, new_v, ssm_conv1, ssm_h1, dn_conv1, dn_s1)


def _prep_layer_weights(i, W):
    w_in = W["w_in"][i]
    o_dt = PROJ_A_W
    o_tail = o_dt + SSM_HEADS
    o_beta = o_tail + PROJ_B_W
    w_tail = w_in[:, o_tail:o_beta].astype(BF16)
    w_small = jnp.concatenate([w_in[:, o_dt:o_tail], w_in[:, o_beta:o_beta + 2 * DN_HEADS],
                               jnp.zeros((D_MODEL, PS_W - SSM_HEADS - 2 * DN_HEADS), w_in.dtype)], axis=1)
    bias_row = jnp.zeros((PS_W,), F32).at[PS_DT:PS_DT + SSM_HEADS].set(W["ssm_dt_bias"][i])
    bias_row = bias_row.at[PS_DECAY:PS_DECAY + DN_HEADS].set(W["dn_dt_bias"][i])
    alog_row = jnp.zeros((PS_W,), F32).at[PS_DT:PS_DT + SSM_HEADS].set(W["ssm_a_log"][i])
    alog_row = alog_row.at[PS_DECAY:PS_DECAY + DN_HEADS].set(W["dn_a_log"][i])
    vec = (bias_row.reshape(1, PS_W), bias_row.reshape(PS_W, 1), alog_row.reshape(1, PS_W), alog_row.reshape(PS_W, 1))
    lw = {
        "ln_mix": W["ln_mix"][i], "w_in": (W["w_in"], i), "w_tail": w_tail, "w_small": w_small.astype(BF16),
        "q_norm": W["q_norm"][i], "k_norm": W["k_norm"][i], "att_subln": W["att_subln"][i],
        "ssm_conv_w": W["ssm_conv_w"][i], "ssm_conv_b": W["ssm_conv_b"][i], "vec": vec,
        "ssm_d": W["ssm_d"][i], "ssm_norm": W["ssm_norm"][i],
        "dn_conv_w": W["dn_conv_w"][i], "dn_norm": W["dn_norm"][i],
        "w_gate": (W["w_gate"], i), "w_up_att": (W["w_up_att"], i), "w_up_ssm": (W["w_up_ssm"], i),
        "w_up_dn": (W["w_up_dn"], i), "w_out": (W["w_out"], i),
        "ln_ffn": W["ln_ffn"][i], "ln_ple": W["ln_ple"][i],
        "ple_w_gate": (W["ple_w_gate"], i), "ple_w_proj": (W["ple_w_proj"], i),
    }
    if i % 2 == 0:
        lw.update(ffn_w_gate=(W["ffn_w_gate"], i // 2), ffn_w_up=(W["ffn_w_up"], i // 2),
                  ffn_w_down=(W["ffn_w_down"], i // 2), d_ff=W["ffn_w_gate"].shape[2])
    else:
        lw.update(router_w=W["router_w"][i // 2], router_b=W["router_b"][i // 2],
                  exp_w_gate=(W["exp_w_gate"], i // 2), exp_w_up=(W["exp_w_up"], i // 2),
                  exp_w_down=(W["exp_w_down"], i // 2))
    lam_init = 0.8 - 0.6 * math.exp(-0.3 * i)
    lam = (jnp.exp(jnp.sum(W["lam_q1"][i] * W["lam_k1"][i])) - jnp.exp(jnp.sum(W["lam_q2"][i] * W["lam_k2"][i]))
           + lam_init).astype(F32)
    return lw, lam, lam_init


def _pick(n, pref):
    t = min(n, pref)
    while n % t:
        t //= 2
    return t


def kernel(x_prompt, x_sample, cache_k, cache_v, state_ssm_conv, state_ssm, state_dn_conv, state_dn, page_table, p_prompt, p_sample, ln_mix, w_in, w_gate, q_norm, k_norm, lam_q1, lam_k1, lam_q2, lam_k2, att_subln, rel_bias, ssm_conv_w, ssm_conv_b, ssm_dt_bias, ssm_a_log, ssm_d, ssm_norm, dn_conv_w, dn_dt_bias, dn_a_log, dn_norm, w_up_att, w_up_ssm, w_up_dn, w_out, ln_ffn, ffn_w_gate, ffn_w_up, ffn_w_down, router_w, router_b, exp_w_gate, exp_w_up, exp_w_down, ln_ple, ple_w_gate, ple_w_proj):
    W = dict(ln_mix=ln_mix, w_in=w_in, w_gate=w_gate, q_norm=q_norm, k_norm=k_norm, lam_q1=lam_q1, lam_k1=lam_k1,
             lam_q2=lam_q2, lam_k2=lam_k2, att_subln=att_subln, ssm_conv_w=ssm_conv_w, ssm_conv_b=ssm_conv_b,
             ssm_dt_bias=ssm_dt_bias, ssm_a_log=ssm_a_log, ssm_d=ssm_d, ssm_norm=ssm_norm, dn_conv_w=dn_conv_w,
             dn_dt_bias=dn_dt_bias, dn_a_log=dn_a_log, dn_norm=dn_norm, w_up_att=w_up_att, w_up_ssm=w_up_ssm,
             w_up_dn=w_up_dn, w_out=w_out, ln_ffn=ln_ffn, ffn_w_gate=ffn_w_gate, ffn_w_up=ffn_w_up,
             ffn_w_down=ffn_w_down, router_w=router_w, router_b=router_b, exp_w_gate=exp_w_gate, exp_w_up=exp_w_up,
             exp_w_down=exp_w_down, ln_ple=ln_ple, ple_w_gate=ple_w_gate, ple_w_proj=ple_w_proj)
    depth = ln_mix.shape[0]
    bp, lp, d = x_prompt.shape
    bs, ls, _ = x_sample.shape
    n_pages = page_table.shape[1]
    t_att = _pick(lp, T_ATT)
    cfg_p = dict(bsz=bp, seq=lp, tm=_pick(bp * lp, TM), cl_ssd=_pick(lp, CL_SSD), cl_gdn=_pick(lp, CL_GDN), pad_to=None,
                 valid_len=None, tm_moe=_pick(bp * lp, TM_MOE), tc=_pick(bp * lp, TC_MOE))
    cfg_p["cps_gdn"] = _pick(lp // cfg_p["cl_gdn"], CPS_GDN)
    cfg_s = dict(bsz=bs, seq=ls, tm=bs * ls, cl_ssd=CL_SAMPLE, cl_gdn=CL_SAMPLE, cps_gdn=1, pad_to=CL_SAMPLE,
                 valid_len=ls,
                 tm_moe=64, tc=bs * ls)
    hp = x_prompt.reshape(bp * lp, d)
    hs = x_sample.reshape(bs * ls, d)
    st_p, st_s = [], []
    tabs = _bias_tables(rel_bias, t_att)
    for i in range(depth):
        lw, lam, lam_init = _prep_layer_weights(i, W)
        post = 1.0 - lam_init
        att_p = lambda q, k, v: _attn_prompt(q, k, v, lam, rel_bias, tabs, lw["att_subln"], post, bp, lp, t_att)
        hp, st = _layer(i, hp, p_prompt[i].reshape(bp * lp, -1), lw, cfg_p, att_p,
                        jnp.zeros((bp, CONV_K - 1, SSM_CONV_DIM), F32),
                        jnp.zeros((bp, SSM_HEADS, SSM_HEAD_DIM, SSM_STATE), F32),
                        jnp.zeros((bp, CONV_K - 1, DN_CONV_DIM), F32),
                        jnp.zeros((bp, DN_HEADS, DN_DK, DN_DV), F32))
        st_p.append(st)
        att_s = lambda q, k, v: _attn_decode(i, q, k, v, cache_k, cache_v, page_table, lam, rel_bias,
                                             lw["att_subln"], post, _pick(n_pages, PAGES_PER_STEP))
        hs, st = _layer(i, hs, p_sample[i].reshape(bs * ls, -1), lw, cfg_s, att_s,
                        state_ssm_conv[i], state_ssm[i], state_dn_conv[i], state_dn[i])
        st_s.append(st)
    stk = lambda sts, j: jnp.stack([s[j] for s in sts])
    return (hp.reshape(bp, lp, d), hs.reshape(bs, ls, d),
            stk(st_p, 0), stk(st_p, 1), stk(st_p, 2), stk(st_p, 3), stk(st_p, 4), stk(st_p, 5),
            stk(st_s, 0), stk(st_s, 1), stk(st_s, 2), stk(st_s, 3), stk(st_s, 4), stk(st_s, 5))
```

```python
import functools
import math

import jax
import jax.numpy as jnp
from jax import lax
from jax.experimental import pallas as pl
from jax.experimental.pallas import tpu as pltpu

F32 = jnp.float32
BF16 = jnp.bfloat16
S = jax.ShapeDtypeStruct
BS = pl.BlockSpec

D_MODEL = 1024
ATT_HEADS = 4
ATT_DK = 64
ATT_DV = 128
ATT_W = ATT_HEADS * ATT_DV
NUM_BUCKETS = 32
MAX_DISTANCE = 128
SSM_HEADS = 16
SSM_HEAD_DIM = 64
SSM_GROUPS = 2
SSM_STATE = 128
SSM_D_INNER = 1024
SSM_CONV_DIM = 1536
DN_HEADS = 4
DN_DK = 128
DN_DV = 128
DN_W = 512
DN_CONV_DIM = 1536
CONV_K = 4
N_EXPERTS = 8
EPS = 1e-6
NEG = -1e30
LOG2E = 1.4426950408889634

PROJ_A_W, PROJ_A_ROT = 4096, 2560
C_XBC, C_AQ, C_AK, C_AV, C_SZ = 0, 1536, 2048, 2560, 3072
PROJ_B_W = 2048
C_DQ, C_DK, C_DV, C_DZ = 0, 512, 1024, 1536
PS_DT, PS_BETA, PS_DECAY, PS_W = 0, 16, 20, 128

VMEM_LIMIT = 56 * 1024 * 1024
T_ATT = 512
TM = 1024
CL_SSD = 128
CL_GDN = 64
CPS_GDN = 4
CL_SAMPLE = 16
PAGES_PER_STEP = 32
TM_MOE = 512
TF_MOE = 896
TC_MOE = 256


def _cparams(sem):
    return pltpu.CompilerParams(dimension_semantics=sem, vmem_limit_bytes=VMEM_LIMIT)


def _dot(a, b):
    return jnp.dot(a, b, preferred_element_type=F32)


def _dot_nt(a, b):
    return lax.dot_general(a, b, (((1,), (1,)), ((), ())), preferred_element_type=F32)


def _dot_tn(a, b):
    return lax.dot_general(a, b, (((0,), (0,)), ((), ())), preferred_element_type=F32)


def _dot_hi(a, b):
    return jnp.dot(a, b, preferred_element_type=F32, precision=lax.Precision.HIGHEST)


def _split(a):
    hi = a.astype(BF16)
    lo = (a - hi.astype(F32)).astype(BF16)
    return hi, lo


def _dot3(a, b):
    ah, al = _split(a)
    bh, bl = _split(b)
    return _dot(ah, bh) + (_dot(ah, bl) + _dot(al, bh))


def _sigmoid(x):
    return 1.0 / (1.0 + jnp.exp(-x))


def _silu(x):
    return x * _sigmoid(x)


def _softplus(x):
    return jnp.maximum(x, 0.0) + jnp.log1p(jnp.exp(-jnp.abs(x)))


def _rmsnorm_body(x_ref, w_ref, o_ref):
    x = x_ref[...]
    y = x * lax.rsqrt(jnp.mean(x * x, axis=-1, keepdims=True) + EPS) * w_ref[...]
    o_ref[...] = y.astype(o_ref.dtype)


def _rmsnorm(x, w, tm):
    n, d = x.shape
    return pl.pallas_call(
        _rmsnorm_body, out_shape=S((n, d), BF16), grid=(n // tm,),
        in_specs=[BS((tm, d), lambda i: (i, 0)), BS((1, d), lambda i: (0, 0))],
        out_specs=BS((tm, d), lambda i: (i, 0)),
        compiler_params=_cparams(("parallel",)), name="rmsnorm")(x, w.reshape(1, d))


def _fused_matmul(acts, pairs, combine, extras, out_dtype, m, tm, tn, name, out_tile=None):
    n = acts[0].shape[0]
    na, npair, nex = len(acts), len(pairs), len(extras)
    act_idx = [a for a, _, _ in pairs]

    def body(*refs):
        a_vals = [r[...] for r in refs[:na]]
        dots = [_dot(a_vals[act_idx[j]], refs[na + j][...].astype(BF16)) for j in range(npair)]
        ex = [r[...] for r in refs[na + npair:na + npair + nex]]
        o_ref = refs[na + npair + nex]
        o_ref[...] = combine(dots, ex).astype(o_ref.dtype)

    def w_spec(w, c0):
        off = c0 // tn
        if isinstance(w, tuple):
            arr, layer = w
            return BS((None, arr.shape[1], tn), lambda i, j: (layer, 0, j + off))
        return BS((w.shape[0], tn), lambda i, j: (0, j + off))

    in_specs = ([BS((tm, a.shape[1]), lambda i, j: (i, 0)) for a in acts]
                + [w_spec(w, c0) for _, w, c0 in pairs]
                + [BS((tm, tn), lambda i, j: (i, j)) for _ in extras])
    out_tile = out_tile or (lambda j: j)
    return pl.pallas_call(
        body, out_shape=S((n, m), out_dtype), grid=(n // tm, m // tn),
        in_specs=in_specs, out_specs=BS((tm, tn), lambda i, j: (i, out_tile(j))),
        compiler_params=_cparams(("parallel", "arbitrary")), name=name,
    )(*acts, *[w[0] if isinstance(w, tuple) else w for _, w, _ in pairs], *extras)


def _qkv_body(aq_ref, ak_ref, av_ref, g_ref, qw_ref, kw_ref, q_ref, kb_ref, vb_ref, kf_ref, vf_ref):
    g = g_ref[...]

    def gnorm(x, w):
        hi, lo = _split(x * x)
        ss = _dot(hi, g) + _dot(lo, g)
        return x * lax.rsqrt(ss * (1.0 / ATT_DK) + EPS) * w

    q = gnorm(aq_ref[...], qw_ref[...]) * (ATT_DK ** -0.5 * LOG2E)
    k = gnorm(ak_ref[...], kw_ref[...])
    v = av_ref[...]
    q_ref[...] = q.astype(BF16)
    kb_ref[...] = k.astype(BF16)
    vb_ref[...] = v.astype(BF16)
    for h in range(ATT_HEADS):
        kf_ref[:, h, :] = k[:, h * ATT_DV:(h + 1) * ATT_DV]
        vf_ref[:, h, :] = v[:, h * ATT_DV:(h + 1) * ATT_DV]


def _qkv_prep(proj, q_norm, k_norm, tm):
    n = proj.shape[0]
    w = ATT_W
    gi = jnp.arange(w) // ATT_DK
    gmat = (gi[:, None] == gi[None, :]).astype(BF16)
    qw = jnp.tile(q_norm, w // ATT_DK).reshape(1, w)
    kw = jnp.tile(k_norm, w // ATT_DK).reshape(1, w)
    row = lambda c: BS((tm, w), lambda i: (i, c))
    const = lambda shp: BS(shp, lambda i: (0, 0))
    thd = BS((tm, ATT_HEADS, ATT_DV), lambda i: (i, 0, 0))
    return pl.pallas_call(
        _qkv_body,
        out_shape=(S((n, w), BF16), S((n, w), BF16), S((n, w), BF16),
                   S((n, ATT_HEADS, ATT_DV), F32), S((n, ATT_HEADS, ATT_DV), F32)),
        grid=(n // tm,),
        in_specs=[row(C_AQ // w), row(C_AK // w), row(C_AV // w), const((w, w)), const((1, w)), const((1, w))],
        out_specs=(row(0), row(0), row(0), thd, thd),
        compiler_params=_cparams(("parallel",)), name="qkv_prep")(proj, proj, proj, gmat, qw, kw)


def _t5_bucket(n):
    max_exact = NUM_BUCKETS // 2
    scaled = jnp.log(jnp.maximum(n, 1).astype(F32) / max_exact) / math.log(MAX_DISTANCE / max_exact)
    large = jnp.minimum(max_exact + (scaled * (NUM_BUCKETS - max_exact)).astype(jnp.int32), NUM_BUCKETS - 1)
    return jnp.where(n < max_exact, n, large)


def _bias_by_distance(rel_bias, n):
    return jnp.moveaxis(rel_bias[_t5_bucket(n)], -1, 0).astype(F32)


def _bias_table_body(rb_ref, o_ref, *, t, rows):
    h = pl.program_id(0)
    max_exact = NUM_BUCKETS // 2

    def block(i, carry):
        r0 = pl.multiple_of(i * rows, rows)
        c = lax.broadcasted_iota(jnp.int32, (rows, t), 0) + r0
        r = lax.broadcasted_iota(jnp.int32, (rows, t), 1)
        for idx in range(2):
            n = r - c + idx * t
            nn = jnp.maximum(n, 0)
            scaled = jnp.log(jnp.maximum(nn, 1).astype(F32) / max_exact) / math.log(MAX_DISTANCE / max_exact)
            large = jnp.minimum(max_exact + (scaled * (NUM_BUCKETS - max_exact)).astype(jnp.int32), NUM_BUCKETS - 1)
            bucket = jnp.where(nn < max_exact, nn, large)
            val = jnp.zeros((rows, t), F32)
            for b in range(NUM_BUCKETS):
                val = jnp.where(bucket == b, rb_ref[b, h] * LOG2E, val)
            if idx == 0:
                val = jnp.where(n >= 0, val, NEG)
            o_ref[idx, pl.ds(r0, rows), :] = val
        return carry

    lax.fori_loop(0, t // rows, block, 0)


def _bias_tables(rel_bias, t):
    return pl.pallas_call(
        functools.partial(_bias_table_body, t=t, rows=8),
        out_shape=S((ATT_HEADS, 2, t, t), F32), grid=(ATT_HEADS,),
        in_specs=[BS(memory_space=pltpu.SMEM)],
        out_specs=BS((None, 2, t, t), lambda h: (h, 0, 0, 0)),
        compiler_params=_cparams(("parallel",)), name="bias_tables")(rel_bias.astype(F32))


def _attn_body(lam_ref, cfar_ref, q_ref, k_ref, vt_ref, tab_ref, sw_ref, o_ref,
               q2_ref, s_ref, sb_ref, p_ref, m_ref, l_ref, acc_ref, *, t, rb, post_scale):
    h = pl.program_id(1)
    qi = pl.program_id(2)
    q = q_ref[...]
    lane = lax.broadcasted_iota(jnp.int32, q.shape, 1)
    zero = jnp.zeros_like(q)
    q2_ref[0:t, :] = jnp.where(lane < ATT_DK, q, zero)
    q2_ref[t:2 * t, :] = jnp.where(lane >= ATT_DK, q, zero)
    m_ref[...] = jnp.full(m_ref.shape, -jnp.inf, F32)
    l_ref[...] = jnp.zeros(l_ref.shape, F32)
    acc_ref[...] = jnp.zeros(acc_ref.shape, F32)
    cfar = cfar_ref[h] * LOG2E
    w2 = 2 * t

    def tile(ki, tab_idx):
        s_ref[...] = _dot_nt(k_ref[pl.ds(pl.multiple_of(ki * t, t), t), :], q2_ref[...])
        src_ref = s_ref if tab_idx is None else sb_ref

        def key_max(g, mx):
            rows = pl.ds(pl.multiple_of(g * rb, rb), rb)
            s = s_ref[rows, :]
            if tab_idx is not None:
                tab = tab_ref[tab_idx, rows, :]
                s = s + jnp.concatenate([tab, tab], axis=1)
                sb_ref[rows, :] = s
            for a in range(0, rb, 8):
                mx = jnp.maximum(mx, s[a:a + 8, :])
            return mx

        mx8 = lax.fori_loop(0, t // rb, key_max, jnp.full((8, w2), -jnp.inf, F32), unroll=4)
        tile_max = jnp.max(mx8, axis=0, keepdims=True)
        if tab_idx is None:
            tile_max = tile_max + cfar
        m_old = m_ref[...]
        m_new = jnp.maximum(m_old, tile_max)
        alpha = jnp.exp2(m_old - m_new)
        m_ref[...] = m_new
        shift = jnp.broadcast_to(m_new if tab_idx is not None else m_new - cfar, (rb, w2))

        def key_exp(g, tot):
            rows = pl.ds(pl.multiple_of(g * rb, rb), rb)
            p = jnp.exp2(src_ref[rows, :] - shift)
            p_ref[rows, :] = p.astype(BF16)
            for a in range(0, rb, 8):
                tot = tot + p[a:a + 8, :]
            return tot

        tot8 = lax.fori_loop(0, t // rb, key_exp, jnp.zeros((8, w2), F32), unroll=4)
        l_ref[...] = alpha * l_ref[...] + jnp.sum(tot8, axis=0, keepdims=True)
        acc_ref[...] = alpha * acc_ref[...] + _dot(vt_ref[ki], p_ref[...])

    def far(ki, carry):
        tile(ki, None)
        return carry

    lax.fori_loop(0, jnp.maximum(qi - 1, 0), far, 0)

    @pl.when(qi >= 1)
    def _():
        tile(qi - 1, 1)

    tile(qi, 0)
    ot = acc_ref[...] * (1.0 / l_ref[...])
    o = (ot[:, 0:t] - lam_ref[0] * ot[:, t:w2]).T
    o = o * lax.rsqrt(jnp.mean(o * o, axis=-1, keepdims=True) + EPS) * sw_ref[...] * post_scale
    o_ref[...] = o.astype(o_ref.dtype)


def _attn_prompt(q, k, v, lam, rel_bias, tabs, subln, post_scale, bsz, seq, t):
    nq = seq // t
    cfar = rel_bias[NUM_BUCKETS - 1].astype(F32)
    vt = v.reshape(bsz, nq, t, ATT_HEADS, ATT_DV).transpose(0, 3, 1, 4, 2).reshape(bsz * ATT_HEADS, nq, ATT_DV, t)
    smem = BS(memory_space=pltpu.SMEM)
    return pl.pallas_call(
        functools.partial(_attn_body, t=t, rb=16, post_scale=post_scale),
        out_shape=S((bsz * seq, ATT_W), BF16), grid=(bsz, ATT_HEADS, nq),
        in_specs=[smem, smem,
                  BS((t, ATT_DV), lambda b, h, qi: (b * nq + qi, h)),
                  BS((seq, ATT_DV), lambda b, h, qi: (b, h)),
                  BS((None, nq, ATT_DV, t), lambda b, h, qi: (b * ATT_HEADS + h, 0, 0, 0)),
                  BS((None, 2, t, t), lambda b, h, qi: (h, 0, 0, 0)),
                  BS((1, ATT_DV), lambda b, h, qi: (0, 0))],
        out_specs=BS((t, ATT_DV), lambda b, h, qi: (b * nq + qi, h)),
        scratch_shapes=[pltpu.VMEM((2 * t, ATT_DV), BF16), pltpu.VMEM((t, 2 * t), F32), pltpu.VMEM((t, 2 * t), F32),
                        pltpu.VMEM((t, 2 * t), BF16), pltpu.VMEM((1, 2 * t), F32), pltpu.VMEM((1, 2 * t), F32),
                        pltpu.VMEM((ATT_DV, 2 * t), F32)],
        compiler_params=_cparams(("parallel", "parallel", "arbitrary")), name="attn_prompt",
    )(lam.reshape(1), cfar, q, k, vt, tabs, subln.reshape(1, ATT_DV))


def _decode_body(pt_ref, lam_ref, q8_ref, kn_ref, vn_ref, btab_ref, sw_ref, *rest, pp, n_steps, post_scale):
    k_refs = rest[:pp]
    v_refs = rest[pp:2 * pp]
    o_ref = rest[2 * pp]
    m_ref, l_ref, acc_ref = rest[2 * pp + 1:]
    s_idx = pl.program_id(1)

    @pl.when(s_idx == 0)
    def _():
        m_ref[...] = jnp.full(m_ref.shape, -jnp.inf, F32)
        l_ref[...] = jnp.zeros(l_ref.shape, F32)
        acc_ref[...] = jnp.zeros(acc_ref.shape, F32)

    q8 = q8_ref[...]
    n_pages = n_steps * pp
    scores = []
    for j in range(pp):
        page = s_idx * pp + j
        bias = jnp.where(page == n_pages - 1, btab_ref[0], btab_ref[1])
        scores.append(_dot_nt(q8, k_refs[j][...].astype(BF16)) + bias)
    s_all = jnp.concatenate(scores, axis=1)
    m_old = m_ref[...]
    m_new = jnp.maximum(m_old, jnp.max(s_all, axis=1, keepdims=True))
    alpha = jnp.exp2(m_old - m_new)
    p_all = jnp.exp2(s_all - m_new)
    l_ref[...] = alpha * l_ref[...] + jnp.sum(p_all, axis=1, keepdims=True)
    ps = p_all.shape[1] // pp
    p16 = p_all.astype(BF16)
    pv = _dot(p16[:, :ps], v_refs[0][...].astype(BF16))
    for j in range(1, pp):
        pv = pv + _dot(p16[:, j * ps:(j + 1) * ps], v_refs[j][...].astype(BF16))
    acc_ref[...] = alpha * acc_ref[...] + pv
    m_ref[...] = m_new

    @pl.when(s_idx == n_steps - 1)
    def _():
        s_self = (jnp.sum(q8.astype(F32) * kn_ref[...].astype(F32), axis=1, keepdims=True)
                  + btab_ref[2][:, 0:1])
        m_o = m_ref[...]
        m_f = jnp.maximum(m_o, s_self)
        a_o = jnp.exp2(m_o - m_f)
        p_s = jnp.exp2(s_self - m_f)
        l_f = a_o * l_ref[...] + p_s
        acc = a_o * acc_ref[...] + p_s * vn_ref[...].astype(F32)
        o8 = acc * (1.0 / l_f)
        lam = lam_ref[0]
        outs = []
        for h in range(ATT_HEADS):
            o = o8[2 * h:2 * h + 1, :] - lam * o8[2 * h + 1:2 * h + 2, :]
            o = o * lax.rsqrt(jnp.mean(o * o, axis=-1, keepdims=True) + EPS) * sw_ref[...] * post_scale
            outs.append(o)
        o_ref[...] = jnp.concatenate(outs, axis=1).astype(o_ref.dtype)


def _attn_decode(layer, q, k_new, v_new, cache_k, cache_v, page_table, lam, rel_bias, subln, post_scale, pp):
    bsz = q.shape[0]
    depth, n_phys, page = cache_k.shape[0], cache_k.shape[1], cache_k.shape[2]
    n_pages = page_table.shape[1]
    n_steps = n_pages // pp
    prow = page * ATT_HEADS
    ck = cache_k.reshape(depth, n_phys, prow, ATT_DV)
    cv = cache_v.reshape(depth, n_phys, prow, ATT_DV)
    rows = jnp.arange(2 * ATT_HEADS)
    qh = jnp.repeat(q.reshape(bsz, ATT_HEADS, ATT_DV), 2, axis=1)
    q8 = jnp.where((jnp.arange(ATT_DV)[None, :] // ATT_DK == rows[:, None] % 2)[None], qh, jnp.zeros((), q.dtype))
    kn8 = jnp.repeat(k_new.reshape(bsz, ATT_HEADS, ATT_DV), 2, axis=1)
    vn8 = jnp.repeat(v_new.reshape(bsz, ATT_HEADS, ATT_DV), 2, axis=1)
    same = (jnp.arange(prow)[None, :] % ATT_HEADS) == (rows[:, None] // 2)
    d_last = page - jnp.arange(prow) // ATT_HEADS
    b_last = jnp.repeat(_bias_by_distance(rel_bias, d_last), 2, axis=0)
    b_far = jnp.broadcast_to(jnp.repeat(rel_bias[NUM_BUCKETS - 1].astype(F32), 2)[:, None], (2 * ATT_HEADS, prow))
    b_self = jnp.broadcast_to(jnp.repeat(rel_bias[0].astype(F32), 2)[:, None], (2 * ATT_HEADS, prow))
    btab = jnp.stack([jnp.where(same, b_last * LOG2E, NEG), jnp.where(same, b_far * LOG2E, NEG), b_self * LOG2E])

    def page_spec(j):
        return BS((None, None, prow, ATT_DV), lambda b, s, pt: (layer, pt[b, s * pp + j], 0, 0))

    const = lambda shp: BS(shp, lambda b, s, pt: tuple(0 for _ in shp))
    per_b = BS((None, 2 * ATT_HEADS, ATT_DV), lambda b, s, pt: (b, 0, 0))
    grid_spec = pltpu.PrefetchScalarGridSpec(
        num_scalar_prefetch=1, grid=(bsz, n_steps),
        in_specs=[BS(memory_space=pltpu.SMEM), per_b, per_b, per_b,
                  const((3, 2 * ATT_HEADS, prow)), const((1, ATT_DV))]
        + [page_spec(j) for j in range(pp)] + [page_spec(j) for j in range(pp)],
        out_specs=BS((None, 1, ATT_W), lambda b, s, pt: (b, 0, 0)),
        scratch_shapes=[pltpu.VMEM((2 * ATT_HEADS, 1), F32), pltpu.VMEM((2 * ATT_HEADS, 1), F32),
                        pltpu.VMEM((2 * ATT_HEADS, ATT_DV), F32)])
    out = pl.pallas_call(
        functools.partial(_decode_body, pp=pp, n_steps=n_steps, post_scale=post_scale),
        out_shape=S((bsz, 1, ATT_W), BF16), grid_spec=grid_spec,
        compiler_params=_cparams(("parallel", "arbitrary")), name="attn_decode",
    )(page_table, lam.reshape(1), q8, kn8, vn8, btab, subln.reshape(1, ATT_DV), *([ck] * pp), *([cv] * pp))
    return out.reshape(bsz, ATT_W)


def _causal_conv(x, prev8, cw_ref, c0, c1):
    acc = x * cw_ref[CONV_K - 1:CONV_K, c0:c1]
    row8 = lax.broadcasted_iota(jnp.int32, prev8.shape, 0)
    for s in range(1, CONV_K):
        r = pltpu.roll(x, s, axis=0)
        pr = pltpu.roll(prev8, s, axis=0)
        head = jnp.where(row8 < s, pr, r[:8])
        sh = jnp.concatenate([head, r[8:]], axis=0) if x.shape[0] > 8 else head
        acc = acc + sh * cw_ref[CONV_K - 1 - s:CONV_K - s, c0:c1]
    return acc


def _tri(cl):
    ii = lax.broadcasted_iota(jnp.int32, (cl, cl), 0)
    jj = lax.broadcasted_iota(jnp.int32, (cl, cl), 1)
    return ii, jj


def _ssd_body(z_ref, xbc_ref, ps_ref, cs0_ref, h0_ref, cw_ref, cb_ref, brow_ref, bcol_ref, arow_ref, acol_ref,
              dexp_ref, nw_ref, ex_ref, y_ref, hl_ref, st_ref, halo_ref, yacc_ref, *, cl, nc, valid_len):
    c = pl.program_id(1)

    @pl.when(c == 0)
    def _():
        st_ref[...] = h0_ref[...]
        halo_ref[...] = cs0_ref[...]

    n_pairs = SSM_HEADS // 2
    pw = 2 * SSM_HEAD_DIM
    states = [st_ref[j] for j in range(n_pairs)]
    x_raw = xbc_ref[...]
    conv = _causal_conv(x_raw, halo_ref[...], cw_ref, 0, SSM_CONV_DIM)
    xbc = _silu(conv + cb_ref[...])
    x = xbc[:, :SSM_D_INNER]
    ps = ps_ref[...]
    step_c = _softplus(ps + brow_ref[...])
    step_r = _softplus(ps.T + bcol_ref[...])
    if valid_len is not None:
        t_c = lax.broadcasted_iota(jnp.int32, step_c.shape, 0) + c * cl
        t_r = lax.broadcasted_iota(jnp.int32, step_r.shape, 1) + c * cl
        step_c = jnp.where(t_c < valid_len, step_c, 0.0)
        step_r = jnp.where(t_r < valid_len, step_r, 0.0)
    la_c = step_c * -jnp.exp(arow_ref[...])
    la_r = (step_r * -jnp.exp(acol_ref[...]))[PS_DT:PS_DT + SSM_HEADS, :]
    ii, jj = _tri(cl)
    incl = ii >= jj
    acs_c = _dot_hi(incl.astype(F32), la_c)
    acs_r = _dot_hi(la_r, (ii <= jj).astype(F32))
    last = acs_c[cl - 1:cl, :]
    e_last = jnp.exp(last)

    def expand(a):
        hi, lo = _split(a)
        return _dot(hi, ex_ref[...]) + _dot(lo, ex_ref[...])

    xs = x * expand(step_c)
    xe = (xs * expand(jnp.exp(last - acs_c))).astype(BF16)
    e_acs = expand(jnp.exp(acs_c))
    xs16 = xs.astype(BF16)
    lane = lax.broadcasted_iota(jnp.int32, (cl, pw), 1)
    srow = lax.broadcasted_iota(jnp.int32, (pw, 1), 0)
    ppg = n_pairs // SSM_GROUPS
    for g in range(SSM_GROUPS):
        b_g = xbc[:, SSM_D_INNER + g * SSM_STATE:SSM_D_INNER + (g + 1) * SSM_STATE].astype(BF16)
        c0 = SSM_D_INNER + SSM_GROUPS * SSM_STATE + g * SSM_STATE
        c_g = xbc[:, c0:c0 + SSM_STATE].astype(BF16)
        cb = _dot_nt(c_g, b_g)
        for r in range(ppg):
            j = g * ppg + r
            cols = slice(j * pw, (j + 1) * pw)
            y_heads = []
            for h in (2 * j, 2 * j + 1):
                dec = jnp.where(incl, jnp.exp(jnp.where(incl, acs_c[:, h:h + 1] - acs_r[h:h + 1, :], 0.0)), 0.0)
                y_heads.append(_dot((cb * dec).astype(BF16), xs16[:, cols]))
            y_diag = jnp.where(lane < SSM_HEAD_DIM, y_heads[0], y_heads[1])
            hst = states[j]
            y_off = _dot_nt(c_g, hst.astype(BF16)) * e_acs[:, cols]
            keep = jnp.where(srow < SSM_HEAD_DIM, e_last[:, 2 * j:2 * j + 1], e_last[:, 2 * j + 1:2 * j + 2])
            states[j] = hst * keep + _dot_tn(xe[:, cols], b_g)
            yacc_ref[:, cols] = y_diag + y_off
    for j in range(n_pairs):
        st_ref[j] = states[j]
    halo_ref[...] = x_raw[cl - 8:, :]
    y = (yacc_ref[...] + dexp_ref[...] * x) * _silu(z_ref[...])
    gw = SSM_D_INNER // SSM_GROUPS
    parts = []
    for g in range(SSM_GROUPS):
        yg = y[:, g * gw:(g + 1) * gw]
        parts.append(yg * lax.rsqrt(jnp.mean(yg * yg, axis=-1, keepdims=True) + EPS))
    y_ref[...] = (jnp.concatenate(parts, axis=1) * nw_ref[...]).astype(y_ref.dtype)

    @pl.when(c == nc - 1)
    def _():
        hl_ref[...] = st_ref[...]


def _ssd(proj, ps, conv0, h0, conv_w, conv_b, vec, ssm_d, ssm_norm, bsz, seq, cl, valid_len):
    nc = seq // cl
    n = bsz * seq
    brow, bcol, arow, acol = vec
    cs0 = jnp.pad(conv0, ((0, 0), (8 - (CONV_K - 1), 0), (0, 0)))
    dexp = jnp.repeat(ssm_d, SSM_HEAD_DIM).reshape(1, SSM_D_INNER)
    expand = (jnp.arange(SSM_D_INNER)[None, :] // SSM_HEAD_DIM == jnp.arange(PS_W)[:, None] - PS_DT).astype(BF16)
    n_pairs, pw = SSM_HEADS // 2, 2 * SSM_HEAD_DIM
    st_shape = (bsz, n_pairs, pw, SSM_STATE)
    if isinstance(h0, tuple):
        h0p = h0[0].reshape((h0[0].shape[0],) + st_shape)
        layer = h0[1]
        h0_spec = BS((None, None, n_pairs, pw, SSM_STATE), lambda b, c: (layer, b, 0, 0, 0))
    else:
        h0p = h0.reshape(st_shape)
        h0_spec = BS((None, n_pairs, pw, SSM_STATE), lambda b, c: (b, 0, 0, 0))
    const = lambda shp: BS(shp, lambda b, c: tuple(0 for _ in shp))
    y, h_last = pl.pallas_call(
        functools.partial(_ssd_body, cl=cl, nc=nc, valid_len=valid_len),
        out_shape=(S((n, SSM_D_INNER), BF16), S(st_shape, F32)), grid=(bsz, nc),
        in_specs=[BS((cl, SSM_D_INNER), lambda b, c: (b * nc + c, C_SZ // SSM_D_INNER)),
                  BS((cl, SSM_CONV_DIM), lambda b, c: (b * nc + c, C_XBC // SSM_CONV_DIM)),
                  BS((cl, PS_W), lambda b, c: (b * nc + c, 0)),
                  BS((None, 8, SSM_CONV_DIM), lambda b, c: (b, 0, 0)),
                  h0_spec,
                  const((CONV_K, SSM_CONV_DIM)), const((1, SSM_CONV_DIM)),
                  const((1, PS_W)), const((PS_W, 1)), const((1, PS_W)), const((PS_W, 1)),
                  const((1, SSM_D_INNER)), const((1, SSM_D_INNER)), const((PS_W, SSM_D_INNER))],
        out_specs=(BS((cl, SSM_D_INNER), lambda b, c: (b * nc + c, 0)),
                   BS((None, n_pairs, pw, SSM_STATE), lambda b, c: (b, 0, 0, 0))),
        scratch_shapes=[pltpu.VMEM((n_pairs, pw, SSM_STATE), F32), pltpu.VMEM((8, SSM_CONV_DIM), F32),
                        pltpu.VMEM((cl, SSM_D_INNER), F32)],
        compiler_params=_cparams(("parallel", "arbitrary")), name="ssd",
    )(proj, proj, ps, cs0, h0p, conv_w, conv_b.reshape(1, -1), brow, bcol, arow, acol, dexp,
      ssm_norm.reshape(1, SSM_D_INNER), expand)
    return y, h_last.reshape(bsz, SSM_HEADS, SSM_HEAD_DIM, SSM_STATE)


def _unit_lower_inverses(mats, ii, jj, cl):
    eye = (ii == jj).astype(F32)
    pair = ((ii >> 1) == (jj >> 1)) & (ii > jj)
    invs = [eye - jnp.where(pair, a, 0.0) for a in mats]
    s = 2
    while s < cl:
        blk = ((ii // (2 * s)) == (jj // (2 * s))) & ((ii % (2 * s)) >= s) & ((jj % (2 * s)) < s)
        xs = [_dot3(inv, jnp.where(blk, a, 0.0)) for inv, a in zip(invs, mats)]
        invs = [inv - _dot3(x, inv) for inv, x in zip(invs, xs)]
        s *= 2
    return invs


def _gdn_body(q_ref, k_ref, v_ref, z_ref, ps_ref, cs0_ref, s0_ref, cw_ref, brow_ref, bcol_ref, arow_ref, acol_ref,
              nw_ref, o_ref, sl_ref, st_ref, halo_ref, *, cl, cps, nc, valid_len):
    c = pl.program_id(1)

    @pl.when(c == 0)
    def _():
        st_ref[...] = s0_ref[...]
        halo_ref[...] = cs0_ref[...]

    w = DN_W
    rows = cl * cps
    halo = halo_ref[...]
    states = [st_ref[h] for h in range(DN_HEADS)]
    raws = (q_ref[...], k_ref[...], v_ref[...])
    q, k, v = (_silu(_causal_conv(raws[p], halo[:, p * w:(p + 1) * w], cw_ref, p * w, (p + 1) * w)) for p in range(3))
    z = z_ref[...]
    ps = ps_ref[...]
    g_c = -jnp.exp(arow_ref[...]) * _softplus(ps + brow_ref[...])
    g_r = -jnp.exp(acol_ref[...]) * _softplus(ps.T + bcol_ref[...])
    beta = _sigmoid(ps)
    if valid_len is not None:
        t_c = lax.broadcasted_iota(jnp.int32, g_c.shape, 0) + c * rows
        t_r = lax.broadcasted_iota(jnp.int32, g_r.shape, 1) + c * rows
        g_c = jnp.where(t_c < valid_len, g_c, 0.0)
        beta = jnp.where(t_c < valid_len, beta, 0.0)
        g_r = jnp.where(t_r < valid_len, g_r, 0.0)
    ii, jj = _tri(cl)
    incl = ii >= jj
    strict = ii > jj
    lower_ones = incl.astype(F32)
    upper_ones = (ii <= jj).astype(F32)
    off = PS_DECAY - 16
    nw = nw_ref[...]
    pre = []
    for ci in range(cps):
        rs = slice(ci * cl, (ci + 1) * cl)
        gcs_c = _dot_hi(lower_ones, g_c[rs, 16:24])
        gcs_r = _dot_hi(g_r[16:24, rs], upper_ones)
        for h in range(DN_HEADS):
            sl = slice(h * DN_DK, (h + 1) * DN_DK)
            qh = q[rs, sl]
            kh = k[rs, sl]
            qh = qh * lax.rsqrt(jnp.sum(qh * qh, axis=-1, keepdims=True) + EPS) * (DN_DK ** -0.5)
            kh = kh * lax.rsqrt(jnp.sum(kh * kh, axis=-1, keepdims=True) + EPS)
            bh = beta[rs, PS_BETA + h:PS_BETA + h + 1]
            col = gcs_c[:, off + h:off + h + 1]
            row = gcs_r[off + h:off + h + 1, :]
            last = gcs_c[cl - 1:cl, off + h:off + h + 1]
            e_col = jnp.exp(col)
            kb = kh * bh
            pre.append(dict(
                dec=jnp.where(incl, jnp.exp(jnp.where(incl, col - row, 0.0)), 0.0),
                q16=qh.astype(BF16), k16=kh.astype(BF16), kb16=kb.astype(BF16),
                vb=v[rs, sl] * bh, kbe=kb * e_col,
                qd16=(qh * e_col).astype(BF16),
                kd16=(kh * jnp.exp(last - col)).astype(BF16),
                g_end=jnp.exp(last),
                gate=nw * _silu(z[rs, sl])))
    lowers = [jnp.where(strict, _dot_nt(t["kb16"], t["k16"]) * t["dec"], 0.0) for t in pre]
    qks = [(_dot_nt(t["q16"], t["k16"]) * t["dec"]).astype(BF16) for t in pre]
    tinvs = _unit_lower_inverses(lowers, ii, jj, cl)
    us = [_dot3(ti, t["vb"]) for ti, t in zip(tinvs, pre)]
    ws = [_dot3(ti, t["kbe"]).astype(BF16) for ti, t in zip(tinvs, pre)]
    outs = []
    for ci in range(cps):
        items = range(ci * DN_HEADS, (ci + 1) * DN_HEADS)
        s16 = [st.astype(BF16) for st in states]
        v16 = [(us[i] - _dot(ws[i], s16[h])).astype(BF16) for h, i in enumerate(items)]
        os_ = [_dot(pre[i]["qd16"], s16[h]) + _dot(qks[i], v16[h]) for h, i in enumerate(items)]
        states = [states[h] * pre[i]["g_end"] + _dot_tn(pre[i]["kd16"], v16[h]) for h, i in enumerate(items)]
        outs += [o * lax.rsqrt(jnp.mean(o * o, axis=-1, keepdims=True) + EPS) * pre[i]["gate"]
                 for o, i in zip(os_, items)]
    for ci in range(cps):
        for h in range(DN_HEADS):
            o_ref[ci * cl:(ci + 1) * cl, h * DN_DV:(h + 1) * DN_DV] = outs[ci * DN_HEADS + h].astype(o_ref.dtype)
    for h in range(DN_HEADS):
        st_ref[h] = states[h]
    for p in range(3):
        halo_ref[:, p * w:(p + 1) * w] = raws[p][rows - 8:, :]

    @pl.when(c == nc - 1)
    def _():
        sl_ref[...] = st_ref[...]


def _gdn(proj, ps, conv0, s0, conv_w, vec, dn_norm, bsz, seq, cl, cps, valid_len):
    rows = cl * cps
    nc = seq // rows
    n = bsz * seq
    brow, bcol, arow, acol = vec
    cs0 = jnp.pad(conv0, ((0, 0), (8 - (CONV_K - 1), 0), (0, 0)))
    w = DN_W
    if isinstance(s0, tuple):
        s0, layer = s0
        s0_spec = BS((None, None, DN_HEADS, DN_DK, DN_DV), lambda b, c: (layer, b, 0, 0, 0))
    else:
        s0_spec = BS((None, DN_HEADS, DN_DK, DN_DV), lambda b, c: (b, 0, 0, 0))
    const = lambda shp: BS(shp, lambda b, c: tuple(0 for _ in shp))
    blk = lambda col: BS((rows, w), lambda b, c: (b * nc + c, col // w))
    o, s_last = pl.pallas_call(
        functools.partial(_gdn_body, cl=cl, cps=cps, nc=nc, valid_len=valid_len),
        out_shape=(S((n, w), BF16), S((bsz, DN_HEADS, DN_DK, DN_DV), F32)), grid=(bsz, nc),
        in_specs=[blk(C_DQ), blk(C_DK), blk(C_DV), blk(C_DZ),
                  BS((rows, PS_W), lambda b, c: (b * nc + c, 0)),
                  BS((None, 8, DN_CONV_DIM), lambda b, c: (b, 0, 0)),
                  s0_spec,
                  const((CONV_K, DN_CONV_DIM)),
                  const((1, PS_W)), const((PS_W, 1)), const((1, PS_W)), const((PS_W, 1)), const((1, DN_DV))],
        out_specs=(BS((rows, w), lambda b, c: (b * nc + c, 0)),
                   BS((None, DN_HEADS, DN_DK, DN_DV), lambda b, c: (b, 0, 0, 0))),
        scratch_shapes=[pltpu.VMEM((DN_HEADS, DN_DK, DN_DV), F32), pltpu.VMEM((8, DN_CONV_DIM), F32)],
        compiler_params=_cparams(("parallel", "arbitrary")), name="gdn",
    )(proj, proj, proj, proj, ps, cs0, s0, conv_w, brow, bcol, arow, acol, dn_norm.reshape(1, DN_DV))
    return o, s_last


def _router_body(h_ref, lw_ref, rw_ref, rb_ref, meta_ref):
    x = h_ref[...]
    xn = x * lax.rsqrt(jnp.mean(x * x, axis=-1, keepdims=True) + EPS) * lw_ref[...]
    logits = _dot_hi(xn, rw_ref[...]) + rb_ref[...]
    lane = lax.broadcasted_iota(jnp.int32, logits.shape, 1)
    big = jnp.int32(logits.shape[1])
    m1 = jnp.max(logits, axis=1, keepdims=True)
    i1 = jnp.min(jnp.where(logits == m1, lane, big), axis=1, keepdims=True)
    rest = jnp.where(lane == i1, NEG, logits)
    m2 = jnp.max(rest, axis=1, keepdims=True)
    i2 = jnp.min(jnp.where(rest == m2, lane, big), axis=1, keepdims=True)
    e2 = jnp.exp(m2 - m1)
    g1 = 1.0 / (1.0 + e2)
    g2 = e2 * g1
    meta = jnp.where(lane == 0, i1.astype(F32), jnp.where(lane == 1, i2.astype(F32),
                     jnp.where(lane == 2, g1, jnp.where(lane == 3, g2, 0.0))))
    meta_ref[...] = meta


def _router(h, ln_w, router_w, router_b, tm):
    n, d = h.shape
    rw = jnp.pad(router_w, ((0, 0), (0, 128 - N_EXPERTS)))
    rb = jnp.pad(router_b, (0, 128 - N_EXPERTS), constant_values=NEG).reshape(1, 128)
    const = lambda shp: BS(shp, lambda i: (0, 0))
    return pl.pallas_call(
        _router_body, out_shape=S((n, 128), F32), grid=(n // tm,),
        in_specs=[BS((tm, d), lambda i: (i, 0)), const((1, d)), const((d, 128)), const((1, 128))],
        out_specs=BS((tm, 128), lambda i: (i, 0)),
        compiler_params=_cparams(("parallel",)), name="router")(h, ln_w.reshape(1, d), rw, rb)


def _experts_body(be_ref, tok_ref, nused_ref, h_hbm, lw_ref, wg_ref, wu_ref, wd_ref, y_ref,
                  xg_ref, xb_ref, acc_ref, sem, *, tm, nf):
    i = pl.program_id(0)
    f = pl.program_id(1)
    used = i < nused_ref[0]
    slot = lax.rem(i, 2)

    def gather(blk, dst):
        def start(r, carry):
            pltpu.make_async_copy(h_hbm.at[pl.ds(tok_ref[blk * tm + r], 1), :],
                                  xg_ref.at[dst, pl.ds(r, 1), :], sem.at[dst]).start()
            return carry

        lax.fori_loop(0, tm, start, 0, unroll=8)

    @pl.when((i == 0) & (f == 0))
    def _():
        gather(0, 0)

    @pl.when(used & (f == 0))
    def _():
        pltpu.make_async_copy(h_hbm.at[pl.ds(0, tm), :], xg_ref.at[slot], sem.at[slot]).wait()
        x = xg_ref[slot]
        xn = x * lax.rsqrt(jnp.mean(x * x, axis=-1, keepdims=True) + EPS) * lw_ref[...]
        xb_ref[...] = xn.astype(BF16)
        acc_ref[...] = jnp.zeros(acc_ref.shape, F32)

    chunk = tm // nf

    @pl.when(used)
    def _():
        nxt = jnp.minimum(i + 1, nused_ref[0] - 1)
        for r in range(chunk):
            pltpu.make_async_copy(h_hbm.at[pl.ds(tok_ref[nxt * tm + f * chunk + r], 1), :],
                                  xg_ref.at[1 - slot, pl.ds(f * chunk + r, 1), :], sem.at[1 - slot]).start()
        xb = xb_ref[...]
        hid = _silu(_dot(xb, wg_ref[...].astype(BF16))) * _dot(xb, wu_ref[...].astype(BF16))
        acc_ref[...] += _dot(hid.astype(BF16), wd_ref[...].astype(BF16))

    @pl.when((i == nused_ref[0] - 1) & (f == nf - 1))
    def _():
        pltpu.make_async_copy(h_hbm.at[pl.ds(0, tm), :], xg_ref.at[1 - slot], sem.at[1 - slot]).wait()

    @pl.when(used & (f == nf - 1))
    def _():
        y_ref[...] = acc_ref[...]

    @pl.when(jnp.logical_not(used) & (f == nf - 1))
    def _():
        y_ref[...] = jnp.zeros(y_ref.shape, F32)


def _experts(h, ln_w, block_expert, row_token, n_used, wg, wu, wd, tm, tf):
    n, d = h.shape
    rows = row_token.shape[0]
    n_blocks = rows // tm
    (wg, layer), (wu, _), (wd, _) = wg, wu, wd
    ff = wg.shape[3]
    nf = ff // tf
    grid_spec = pltpu.PrefetchScalarGridSpec(
        num_scalar_prefetch=3, grid=(n_blocks, nf),
        in_specs=[BS(memory_space=pl.ANY),
                  BS((1, d), lambda i, f, be, tok, nu: (0, 0)),
                  BS((None, None, d, tf), lambda i, f, be, tok, nu: (layer, be[i], 0, f)),
                  BS((None, None, d, tf), lambda i, f, be, tok, nu: (layer, be[i], 0, f)),
                  BS((None, None, tf, d), lambda i, f, be, tok, nu: (layer, be[i], f, 0))],
        out_specs=BS((tm, d), lambda i, f, be, tok, nu: (i, 0)),
        scratch_shapes=[pltpu.VMEM((2, tm, d), F32), pltpu.VMEM((tm, d), BF16), pltpu.VMEM((tm, d), F32),
                        pltpu.SemaphoreType.DMA((2,))])
    return pl.pallas_call(
        functools.partial(_experts_body, tm=tm, nf=nf),
        out_shape=S((rows, d), F32), grid_spec=grid_spec,
        compiler_params=_cparams(("arbitrary", "arbitrary")), name="experts",
    )(block_expert, row_token, n_used, h, ln_w.reshape(1, d), wg, wu, wd)


def _combine_body(pos_ref, y_hbm, h_ref, meta_ref, o_ref, ya_ref, yb_ref, sem, *, tc, n_steps):
    i = pl.program_id(0)
    slot = lax.rem(i, 2)

    def gather(step, dst):
        def start(r, carry):
            t = step * tc + r
            pltpu.make_async_copy(y_hbm.at[pl.ds(pos_ref[2 * t], 1), :], ya_ref.at[dst, pl.ds(r, 1), :],
                                  sem.at[dst]).start()
            pltpu.make_async_copy(y_hbm.at[pl.ds(pos_ref[2 * t + 1], 1), :], yb_ref.at[dst, pl.ds(r, 1), :],
                                  sem.at[dst]).start()
            return carry

        lax.fori_loop(0, tc, start, 0, unroll=8)

    @pl.when(i == 0)
    def _():
        gather(0, 0)

    @pl.when(i + 1 < n_steps)
    def _():
        gather(i + 1, 1 - slot)

    pltpu.make_async_copy(y_hbm.at[pl.ds(0, tc), :], ya_ref.at[slot], sem.at[slot]).wait()
    pltpu.make_async_copy(y_hbm.at[pl.ds(0, tc), :], yb_ref.at[slot], sem.at[slot]).wait()
    meta = meta_ref[...]
    o_ref[...] = h_ref[...] + (meta[:, 2:3] * ya_ref[slot] + meta[:, 3:4] * yb_ref[slot])


def _combine(h, y_sorted, pos, meta, tc):
    n, d = h.shape
    n_steps = n // tc
    grid_spec = pltpu.PrefetchScalarGridSpec(
        num_scalar_prefetch=1, grid=(n_steps,),
        in_specs=[BS(memory_space=pl.ANY), BS((tc, d), lambda i, pos: (i, 0)), BS((tc, 128), lambda i, pos: (i, 0))],
        out_specs=BS((tc, d), lambda i, pos: (i, 0)),
        scratch_shapes=[pltpu.VMEM((2, tc, d), F32), pltpu.VMEM((2, tc, d), F32), pltpu.SemaphoreType.DMA((2,))])
    return pl.pallas_call(
        functools.partial(_combine_body, tc=tc, n_steps=n_steps), out_shape=S((n, d), F32), grid_spec=grid_spec,
        compiler_params=_cparams(("arbitrary",)), name="moe_combine")(pos, y_sorted, h, meta)


def _moe(h, ln_w, router_w, router_b, wg, wu, wd, tm_r, tm, tf, tc):
    n, d = h.shape
    meta = _router(h, ln_w, router_w, router_b, tm_r)
    idx = meta[:, 0:2].astype(jnp.int32)
    member = jnp.sum(jax.nn.one_hot(idx, N_EXPERTS, dtype=jnp.int32), axis=1)
    before = jnp.cumsum(member, axis=0) - member
    counts = jnp.sum(member, axis=0)
    padded = (counts + tm - 1) // tm * tm
    pad_ends = jnp.cumsum(padded)
    pad_starts = pad_ends - padded
    pos = (pad_starts[idx] + jnp.take_along_axis(before, idx, axis=1)).astype(jnp.int32)
    n_blocks = -(-(2 * n + N_EXPERTS * (tm - 1)) // tm)
    rows = n_blocks * tm
    tok = jnp.broadcast_to(jnp.arange(n, dtype=jnp.int32)[:, None], (n, 2))
    row_token = jnp.zeros((rows,), jnp.int32).at[pos.reshape(-1)].set(tok.reshape(-1))
    n_used = (pad_ends[-1] // tm).astype(jnp.int32).reshape(1)
    blk_start = jnp.minimum(jnp.arange(n_blocks, dtype=jnp.int32), n_used[0] - 1) * tm
    block_expert = jnp.minimum(jnp.searchsorted(pad_ends, blk_start, side="right"), N_EXPERTS - 1).astype(jnp.int32)
    y_sorted = _experts(h, ln_w, block_expert, row_token, n_used, wg, wu, wd, tm, tf)
    return _combine(h, y_sorted, pos.reshape(-1), meta, tc)


def _layer(i, h, p, lw, cfg, att_fn, ssm_conv0, ssm_h0, dn_conv0, dn_s0):
    bsz, seq, tm, cl_ssd, cl_gdn, pad_to, valid_len = (cfg[k] for k in
                                                      ("bsz", "seq", "tm", "cl_ssd", "cl_gdn", "pad_to", "valid_len"))
    n = bsz * seq
    xn = _rmsnorm(h, lw["ln_mix"], tm)
    ident = lambda d, e: d[0]
    tn = 512
    rot = PROJ_A_ROT // tn
    n_a = PROJ_A_W // tn
    tm_in = _pick(n, 2 * tm)
    proj_a = _fused_matmul([xn], [(0, lw["w_in"], 0)], ident, [], F32, PROJ_A_W, tm_in, tn, "in_proj_a",
                           out_tile=lambda j: lax.rem(j + (n_a - rot), n_a))
    proj_b = _fused_matmul([xn], [(0, lw["w_tail"], 0)], ident, [], F32, PROJ_B_W, tm_in, tn, "in_proj_b")
    ps = _fused_matmul([xn], [(0, lw["w_small"], 0)], ident, [], F32, PS_W, tm, PS_W, "in_proj_small")

    q16, k16, v16, k32, v32 = _qkv_prep(proj_a, lw["q_norm"], lw["k_norm"], tm)
    att = att_fn(q16, k16, v16)

    if pad_to is None:
        pa_r, pb_r, ps_r, seq_r = proj_a, proj_b, ps, seq
    else:
        seq_r = pad_to
        padded = lambda a: jnp.pad(a.reshape(bsz, seq, -1), ((0, 0), (0, pad_to - seq), (0, 0))).reshape(bsz * pad_to, -1)
        pa_r, pb_r, ps_r = padded(proj_a), padded(proj_b), padded(ps)
    y, ssm_h1 = _ssd(pa_r, ps_r, ssm_conv0, ssm_h0, lw["ssm_conv_w"], lw["ssm_conv_b"], lw["vec"], lw["ssm_d"],
                     lw["ssm_norm"], bsz, seq_r, cl_ssd, valid_len)
    o, dn_s1 = _gdn(pb_r, ps_r, dn_conv0, dn_s0, lw["dn_conv_w"], lw["vec"], lw["dn_norm"], bsz, seq_r, cl_gdn,
                    cfg["cps_gdn"], valid_len)
    if pad_to is not None:
        y = y.reshape(bsz, pad_to, -1)[:, :seq].reshape(n, -1)
        o = o.reshape(bsz, pad_to, -1)[:, :seq].reshape(n, -1)

    def merge(d, e):
        return _sigmoid(d[0]) * d[3] + _sigmoid(d[1]) * d[4] + _sigmoid(d[2]) * d[5]

    d_m = D_MODEL
    merged = _fused_matmul([xn, att, y, o],
                           [(0, lw["w_gate"], 0), (0, lw["w_gate"], d_m), (0, lw["w_gate"], 2 * d_m),
                            (1, lw["w_up_att"], 0), (2, lw["w_up_ssm"], 0), (3, lw["w_up_dn"], 0)],
                           merge, [], BF16, d_m, tm, 512, "merge")
    resid = lambda d, e: e[0] + d[0]
    h = _fused_matmul([merged], [(0, lw["w_out"], 0)], resid, [h], F32, d_m, tm_in, 512, "out_proj")

    if i % 2 == 0:
        hn = _rmsnorm(h, lw["ln_ffn"], tm)
        d_ff = lw["d_ff"]
        ff = _fused_matmul([hn], [(0, lw["ffn_w_gate"], 0), (0, lw["ffn_w_up"], 0)],
                           lambda d, e: _silu(d[0]) * d[1], [], BF16, d_ff, tm, 1408, "ffn_up")
        h = _fused_matmul([ff], [(0, lw["ffn_w_down"], 0)], resid, [h], F32, d_m, tm, 512, "ffn_down")
    else:
        h = _moe(h, lw["ln_ffn"], lw["router_w"], lw["router_b"], lw["exp_w_gate"], lw["exp_w_up"], lw["exp_w_down"],
                 min(tm, 512), cfg["tm_moe"], TF_MOE, cfg["tc"])

    hn = _rmsnorm(h, lw["ln_ple"], tm)
    h = _fused_matmul([hn, p.astype(BF16)], [(0, lw["ple_w_gate"], 0), (1, lw["ple_w_proj"], 0)],
                      lambda d, e: e[0] + _sigmoid(d[0]) * d[1], [h], F32, d_m, tm_in, 512, "ple")

    keep = CONV_K - 1
    tail = max(seq - keep, 0)
    xbc_raw = proj_a.reshape(bsz, seq, -1)[:, tail:, C_XBC:C_XBC + SSM_CONV_DIM]
    dn_raw = proj_b.reshape(bsz, seq, -1)[:, tail:, C_DQ:C_DQ + DN_CONV_DIM]
    if seq >= keep:
        ssm_conv1, dn_conv1 = xbc_raw, dn_raw
    else:
        ssm_conv1 = jnp.concatenate([ssm_conv0, xbc_raw], axis=1)[:, -keep:]
        dn_conv1 = jnp.concatenate([dn_conv0, dn_raw], axis=1)[:, -keep:]
    new_k = k32.reshape(bsz, seq, ATT_HEADS, 2 * ATT_DK)
    new_v = v32.reshape(bsz, seq, ATT_HEADS, ATT_DV)
    return h, (new_k, new_v, ssm_conv1, ssm_h1, dn_conv1, dn_s1)


def _prep_layer_weights(i, W):
    w_in = W["w_in"][i]
    o_dt = PROJ_A_W
    o_tail = o_dt + SSM_HEADS
    o_beta = o_tail + PROJ_B_W
    w_tail = w_in[:, o_tail:o_beta].astype(BF16)
    w_small = jnp.concatenate([w_in[:, o_dt:o_tail], w_in[:, o_beta:o_beta + 2 * DN_HEADS],
                               jnp.zeros((D_MODEL, PS_W - SSM_HEADS - 2 * DN_HEADS), w_in.dtype)], axis=1)
    bias_row = jnp.zeros((PS_W,), F32).at[PS_DT:PS_DT + SSM_HEADS].set(W["ssm_dt_bias"][i])
    bias_row = bias_row.at[PS_DECAY:PS_DECAY + DN_HEADS].set(W["dn_dt_bias"][i])
    alog_row = jnp.zeros((PS_W,), F32).at[PS_DT:PS_DT + SSM_HEADS].set(W["ssm_a_log"][i])
    alog_row = alog_row.at[PS_DECAY:PS_DECAY + DN_HEADS].set(W["dn_a_log"][i])
    vec = (bias_row.reshape(1, PS_W), bias_row.reshape(PS_W, 1), alog_row.reshape(1, PS_W), alog_row.reshape(PS_W, 1))
    lw = {
        "ln_mix": W["ln_mix"][i], "w_in": (W["w_in"], i), "w_tail": w_tail, "w_small": w_small.astype(BF16),
        "q_norm": W["q_norm"][i], "k_norm": W["k_norm"][i], "att_subln": W["att_subln"][i],
        "ssm_conv_w": W["ssm_conv_w"][i], "ssm_conv_b": W["ssm_conv_b"][i], "vec": vec,
        "ssm_d": W["ssm_d"][i], "ssm_norm": W["ssm_norm"][i],
        "dn_conv_w": W["dn_conv_w"][i], "dn_norm": W["dn_norm"][i],
        "w_gate": (W["w_gate"], i), "w_up_att": (W["w_up_att"], i), "w_up_ssm": (W["w_up_ssm"], i),
        "w_up_dn": (W["w_up_dn"], i), "w_out": (W["w_out"], i),
        "ln_ffn": W["ln_ffn"][i], "ln_ple": W["ln_ple"][i],
        "ple_w_gate": (W["ple_w_gate"], i), "ple_w_proj": (W["ple_w_proj"], i),
    }
    if i % 2 == 0:
        lw.update(ffn_w_gate=(W["ffn_w_gate"], i // 2), ffn_w_up=(W["ffn_w_up"], i // 2),
                  ffn_w_down=(W["ffn_w_down"], i // 2), d_ff=W["ffn_w_gate"].shape[2])
    else:
        lw.update(router_w=W["router_w"][i // 2], router_b=W["router_b"][i // 2],
                  exp_w_gate=(W["exp_w_gate"], i // 2), exp_w_up=(W["exp_w_up"], i // 2),
                  exp_w_down=(W["exp_w_down"], i // 2))
    lam_init = 0.8 - 0.6 * math.exp(-0.3 * i)
    lam = (jnp.exp(jnp.sum(W["lam_q1"][i] * W["lam_k1"][i])) - jnp.exp(jnp.sum(W["lam_q2"][i] * W["lam_k2"][i]))
           + lam_init).astype(F32)
    return lw, lam, lam_init


def _pick(n, pref):
    t = min(n, pref)
    while n % t:
        t //= 2
    return t


def kernel(x_prompt, x_sample, cache_k, cache_v, state_ssm_conv, state_ssm, state_dn_conv, state_dn, page_table, p_prompt, p_sample, ln_mix, w_in, w_gate, q_norm, k_norm, lam_q1, lam_k1, lam_q2, lam_k2, att_subln, rel_bias, ssm_conv_w, ssm_conv_b, ssm_dt_bias, ssm_a_log, ssm_d, ssm_norm, dn_conv_w, dn_dt_bias, dn_a_log, dn_norm, w_up_att, w_up_ssm, w_up_dn, w_out, ln_ffn, ffn_w_gate, ffn_w_up, ffn_w_down, router_w, router_b, exp_w_gate, exp_w_up, exp_w_down, ln_ple, ple_w_gate, ple_w_proj):
    W = dict(ln_mix=ln_mix, w_in=w_in, w_gate=w_gate, q_norm=q_norm, k_norm=k_norm, lam_q1=lam_q1, lam_k1=lam_k1,
             lam_q2=lam_q2, lam_k2=lam_k2, att_subln=att_subln, ssm_conv_w=ssm_conv_w, ssm_conv_b=ssm_conv_b,
             ssm_dt_bias=ssm_dt_bias, ssm_a_log=ssm_a_log, ssm_d=ssm_d, ssm_norm=ssm_norm, dn_conv_w=dn_conv_w,
             dn_dt_bias=dn_dt_bias, dn_a_log=dn_a_log, dn_norm=dn_norm, w_up_att=w_up_att, w_up_ssm=w_up_ssm,
             w_up_dn=w_up_dn, w_out=w_out, ln_ffn=ln_ffn, ffn_w_gate=ffn_w_gate, ffn_w_up=ffn_w_up,
             ffn_w_down=ffn_w_down, router_w=router_w, router_b=router_b, exp_w_gate=exp_w_gate, exp_w_up=exp_w_up,
             exp_w_down=exp_w_down, ln_ple=ln_ple, ple_w_gate=ple_w_gate, ple_w_proj=ple_w_proj)
    depth = ln_mix.shape[0]
    bp, lp, d = x_prompt.shape
    bs, ls, _ = x_sample.shape
    n_pages = page_table.shape[1]
    t_att = _pick(lp, T_ATT)
    cfg_p = dict(bsz=bp, seq=lp, tm=_pick(bp * lp, TM), cl_ssd=_pick(lp, CL_SSD), cl_gdn=_pick(lp, CL_GDN), pad_to=None,
                 valid_len=None, tm_moe=_pick(bp * lp, TM_MOE), tc=_pick(bp * lp, TC_MOE))
    cfg_p["cps_gdn"] = _pick(lp // cfg_p["cl_gdn"], CPS_GDN)
    cfg_s = dict(bsz=bs, seq=ls, tm=bs * ls, cl_ssd=CL_SAMPLE, cl_gdn=CL_SAMPLE, cps_gdn=1, pad_to=CL_SAMPLE,
                 valid_len=ls,
                 tm_moe=64, tc=bs * ls)
    hp = x_prompt.reshape(bp * lp, d)
    hs = x_sample.reshape(bs * ls, d)
    st_p, st_s = [], []
    tabs = _bias_tables(rel_bias, t_att)
    for i in range(depth):
        lw, lam, lam_init = _prep_layer_weights(i, W)
        post = 1.0 - lam_init
        att_p = lambda q, k, v: _attn_prompt(q, k, v, lam, rel_bias, tabs, lw["att_subln"], post, bp, lp, t_att)
        hp, st = _layer(i, hp, p_prompt[i].reshape(bp * lp, -1), lw, cfg_p, att_p,
                        jnp.zeros((bp, CONV_K - 1, SSM_CONV_DIM), F32),
                        jnp.zeros((bp, SSM_HEADS, SSM_HEAD_DIM, SSM_STATE), F32),
                        jnp.zeros((bp, CONV_K - 1, DN_CONV_DIM), F32),
                        jnp.zeros((bp, DN_HEADS, DN_DK, DN_DV), F32))
        st_p.append(st)
        att_s = lambda q, k, v: _attn_decode(i, q, k, v, cache_k, cache_v, page_table, lam, rel_bias,
                                             lw["att_subln"], post, _pick(n_pages, PAGES_PER_STEP))
        hs, st = _layer(i, hs, p_sample[i].reshape(bs * ls, -1), lw, cfg_s, att_s,
                        state_ssm_conv[i], (state_ssm, i), state_dn_conv[i], (state_dn, i))
        st_s.append(st)
    stk = lambda sts, j: jnp.stack([s[j] for s in sts])
    return (hp.reshape(bp, lp, d), hs.reshape(bs, ls, d),
            stk(st_p, 0), stk(st_p, 1), stk(st_p, 2), stk(st_p, 3), stk(st_p, 4), stk(st_p, 5),
            stk(st_s, 0), stk(st_s, 1), stk(st_s, 2), stk(st_s, 3), stk(st_s, 4), stk(st_s, 5))
```

```python
import functools
import math

import jax
import jax.numpy as jnp
from jax import lax
from jax.experimental import pallas as pl
from jax.experimental.pallas import tpu as pltpu

F32 = jnp.float32
BF16 = jnp.bfloat16
S = jax.ShapeDtypeStruct
BS = pl.BlockSpec

D_MODEL = 1024
ATT_HEADS = 4
ATT_DK = 64
ATT_DV = 128
ATT_W = ATT_HEADS * ATT_DV
NUM_BUCKETS = 32
MAX_DISTANCE = 128
SSM_HEADS = 16
SSM_HEAD_DIM = 64
SSM_GROUPS = 2
SSM_STATE = 128
SSM_D_INNER = 1024
SSM_CONV_DIM = 1536
DN_HEADS = 4
DN_DK = 128
DN_DV = 128
DN_W = 512
DN_CONV_DIM = 1536
CONV_K = 4
N_EXPERTS = 8
EPS = 1e-6
NEG = -1e30
LOG2E = 1.4426950408889634

PROJ_A_W, PROJ_A_ROT = 4096, 2560
C_XBC, C_AQ, C_AK, C_AV, C_SZ = 0, 1536, 2048, 2560, 3072
PROJ_B_W = 2048
C_DQ, C_DK, C_DV, C_DZ = 0, 512, 1024, 1536
PS_DT, PS_BETA, PS_DECAY, PS_W = 0, 16, 20, 128

VMEM_LIMIT = 56 * 1024 * 1024
T_ATT = 512
TM = 1024
CL_SSD = 128
CL_GDN = 64
CPS_GDN = 4
CL_SAMPLE = 16
PAGES_PER_STEP = 32
TM_MOE = 512
TF_MOE = 896
TC_MOE = 256


def _cparams(sem):
    return pltpu.CompilerParams(dimension_semantics=sem, vmem_limit_bytes=VMEM_LIMIT)


def _dot(a, b):
    return jnp.dot(a, b, preferred_element_type=F32)


def _dot_nt(a, b):
    return lax.dot_general(a, b, (((1,), (1,)), ((), ())), preferred_element_type=F32)


def _dot_tn(a, b):
    return lax.dot_general(a, b, (((0,), (0,)), ((), ())), preferred_element_type=F32)


def _dot_hi(a, b):
    return jnp.dot(a, b, preferred_element_type=F32, precision=lax.Precision.HIGHEST)


def _split(a):
    hi = a.astype(BF16)
    lo = (a - hi.astype(F32)).astype(BF16)
    return hi, lo


def _dot3(a, b):
    ah, al = _split(a)
    bh, bl = _split(b)
    return _dot(ah, bh) + (_dot(ah, bl) + _dot(al, bh))


def _sigmoid(x):
    return 1.0 / (1.0 + jnp.exp(-x))


def _silu(x):
    return x * _sigmoid(x)


def _softplus(x):
    return jnp.maximum(x, 0.0) + jnp.log1p(jnp.exp(-jnp.abs(x)))


def _rmsnorm_body(x_ref, w_ref, o_ref):
    x = x_ref[...]
    y = x * lax.rsqrt(jnp.mean(x * x, axis=-1, keepdims=True) + EPS) * w_ref[...]
    o_ref[...] = y.astype(o_ref.dtype)


def _rmsnorm(x, w, tm):
    n, d = x.shape
    return pl.pallas_call(
        _rmsnorm_body, out_shape=S((n, d), BF16), grid=(n // tm,),
        in_specs=[BS((tm, d), lambda i: (i, 0)), BS((1, d), lambda i: (0, 0))],
        out_specs=BS((tm, d), lambda i: (i, 0)),
        compiler_params=_cparams(("parallel",)), name="rmsnorm")(x, w.reshape(1, d))


def _fused_matmul(acts, pairs, combine, extras, out_dtype, m, tm, tn, name, out_tile=None):
    n = acts[0].shape[0]
    na, npair, nex = len(acts), len(pairs), len(extras)
    act_idx = [a for a, _, _ in pairs]

    def body(*refs):
        a_vals = [r[...] for r in refs[:na]]
        dots = [_dot(a_vals[act_idx[j]], refs[na + j][...].astype(BF16)) for j in range(npair)]
        ex = [r[...] for r in refs[na + npair:na + npair + nex]]
        o_ref = refs[na + npair + nex]
        o_ref[...] = combine(dots, ex).astype(o_ref.dtype)

    def w_spec(w, c0):
        off = c0 // tn
        if isinstance(w, tuple):
            arr, layer = w
            return BS((None, arr.shape[1], tn), lambda i, j: (layer, 0, j + off))
        return BS((w.shape[0], tn), lambda i, j: (0, j + off))

    in_specs = ([BS((tm, a.shape[1]), lambda i, j: (i, 0)) for a in acts]
                + [w_spec(w, c0) for _, w, c0 in pairs]
                + [BS((tm, tn), lambda i, j: (i, j)) for _ in extras])
    out_tile = out_tile or (lambda j: j)
    return pl.pallas_call(
        body, out_shape=S((n, m), out_dtype), grid=(n // tm, m // tn),
        in_specs=in_specs, out_specs=BS((tm, tn), lambda i, j: (i, out_tile(j))),
        compiler_params=_cparams(("parallel", "arbitrary")), name=name,
    )(*acts, *[w[0] if isinstance(w, tuple) else w for _, w, _ in pairs], *extras)


def _qkv_body(aq_ref, ak_ref, av_ref, g_ref, qw_ref, kw_ref, q_ref, kb_ref, vb_ref, kf_ref, vf_ref):
    g = g_ref[...]

    def gnorm(x, w):
        hi, lo = _split(x * x)
        ss = _dot(hi, g) + _dot(lo, g)
        return x * lax.rsqrt(ss * (1.0 / ATT_DK) + EPS) * w

    q = gnorm(aq_ref[...], qw_ref[...]) * (ATT_DK ** -0.5 * LOG2E)
    k = gnorm(ak_ref[...], kw_ref[...])
    v = av_ref[...]
    q_ref[...] = q.astype(BF16)
    kb_ref[...] = k.astype(BF16)
    vb_ref[...] = v.astype(BF16)
    for h in range(ATT_HEADS):
        kf_ref[:, h, :] = k[:, h * ATT_DV:(h + 1) * ATT_DV]
        vf_ref[:, h, :] = v[:, h * ATT_DV:(h + 1) * ATT_DV]


def _qkv_prep(proj, q_norm, k_norm, tm):
    n = proj.shape[0]
    w = ATT_W
    gi = jnp.arange(w) // ATT_DK
    gmat = (gi[:, None] == gi[None, :]).astype(BF16)
    qw = jnp.tile(q_norm, w // ATT_DK).reshape(1, w)
    kw = jnp.tile(k_norm, w // ATT_DK).reshape(1, w)
    row = lambda c: BS((tm, w), lambda i: (i, c))
    const = lambda shp: BS(shp, lambda i: (0, 0))
    thd = BS((tm, ATT_HEADS, ATT_DV), lambda i: (i, 0, 0))
    return pl.pallas_call(
        _qkv_body,
        out_shape=(S((n, w), BF16), S((n, w), BF16), S((n, w), BF16),
                   S((n, ATT_HEADS, ATT_DV), F32), S((n, ATT_HEADS, ATT_DV), F32)),
        grid=(n // tm,),
        in_specs=[row(C_AQ // w), row(C_AK // w), row(C_AV // w), const((w, w)), const((1, w)), const((1, w))],
        out_specs=(row(0), row(0), row(0), thd, thd),
        compiler_params=_cparams(("parallel",)), name="qkv_prep")(proj, proj, proj, gmat, qw, kw)


def _t5_bucket(n):
    max_exact = NUM_BUCKETS // 2
    scaled = jnp.log(jnp.maximum(n, 1).astype(F32) / max_exact) / math.log(MAX_DISTANCE / max_exact)
    large = jnp.minimum(max_exact + (scaled * (NUM_BUCKETS - max_exact)).astype(jnp.int32), NUM_BUCKETS - 1)
    return jnp.where(n < max_exact, n, large)


def _bias_by_distance(rel_bias, n):
    return jnp.moveaxis(rel_bias[_t5_bucket(n)], -1, 0).astype(F32)


def _bias_table_body(rb_ref, o_ref, *, t, rows):
    h = pl.program_id(0)
    max_exact = NUM_BUCKETS // 2

    def block(i, carry):
        r0 = pl.multiple_of(i * rows, rows)
        c = lax.broadcasted_iota(jnp.int32, (rows, t), 0) + r0
        r = lax.broadcasted_iota(jnp.int32, (rows, t), 1)
        for idx in range(2):
            n = r - c + idx * t
            nn = jnp.maximum(n, 0)
            scaled = jnp.log(jnp.maximum(nn, 1).astype(F32) / max_exact) / math.log(MAX_DISTANCE / max_exact)
            large = jnp.minimum(max_exact + (scaled * (NUM_BUCKETS - max_exact)).astype(jnp.int32), NUM_BUCKETS - 1)
            bucket = jnp.where(nn < max_exact, nn, large)
            val = jnp.zeros((rows, t), F32)
            for b in range(NUM_BUCKETS):
                val = jnp.where(bucket == b, rb_ref[b, h] * LOG2E, val)
            if idx == 0:
                val = jnp.where(n >= 0, val, NEG)
            o_ref[idx, pl.ds(r0, rows), :] = val
        return carry

    lax.fori_loop(0, t // rows, block, 0)


def _bias_tables(rel_bias, t):
    return pl.pallas_call(
        functools.partial(_bias_table_body, t=t, rows=8),
        out_shape=S((ATT_HEADS, 2, t, t), F32), grid=(ATT_HEADS,),
        in_specs=[BS(memory_space=pltpu.SMEM)],
        out_specs=BS((None, 2, t, t), lambda h: (h, 0, 0, 0)),
        compiler_params=_cparams(("parallel",)), name="bias_tables")(rel_bias.astype(F32))


def _attn_body(lam_ref, cfar_ref, q_ref, k_ref, vt_ref, tab_ref, sw_ref, o_ref,
               q2_ref, s_ref, sb_ref, p0_ref, p1_ref, m_ref, l_ref, acc0_ref, acc1_ref, *, t, rb, post_scale):
    h = pl.program_id(1)
    qi = pl.program_id(2)
    q = q_ref[...]
    lane = lax.broadcasted_iota(jnp.int32, q.shape, 1)
    zero = jnp.zeros_like(q)
    q2_ref[0:t, :] = jnp.where(lane < ATT_DK, q, zero)
    q2_ref[t:2 * t, :] = jnp.where(lane >= ATT_DK, q, zero)
    m_ref[...] = jnp.full(m_ref.shape, -jnp.inf, F32)
    l_ref[...] = jnp.zeros(l_ref.shape, F32)
    acc0_ref[...] = jnp.zeros(acc0_ref.shape, F32)
    acc1_ref[...] = jnp.zeros(acc1_ref.shape, F32)
    cfar = cfar_ref[h] * LOG2E
    w2 = 2 * t

    def tile(ki, tab_idx):
        s_ref[...] = _dot_nt(k_ref[pl.ds(pl.multiple_of(ki * t, t), t), :], q2_ref[...])
        src_ref = s_ref if tab_idx is None else sb_ref

        def key_max(g, mx):
            rows = pl.ds(pl.multiple_of(g * rb, rb), rb)
            s = s_ref[rows, :]
            if tab_idx is not None:
                tab = tab_ref[tab_idx, rows, :]
                s = s + jnp.concatenate([tab, tab], axis=1)
                sb_ref[rows, :] = s
            for a in range(0, rb, 8):
                mx = jnp.maximum(mx, s[a:a + 8, :])
            return mx

        mx8 = lax.fori_loop(0, t // rb, key_max, jnp.full((8, w2), -jnp.inf, F32), unroll=4)
        tile_max = jnp.max(mx8, axis=0, keepdims=True)
        if tab_idx is None:
            tile_max = tile_max + cfar
        m_old = m_ref[...]
        m_new = jnp.maximum(m_old, tile_max)
        alpha = jnp.exp2(m_old - m_new)
        m_ref[...] = m_new
        shift = jnp.broadcast_to(m_new if tab_idx is not None else m_new - cfar, (rb, w2))

        vt = vt_ref[ki]
        tots = []
        for half, (ph_ref, ah_ref) in enumerate(((p0_ref, acc0_ref), (p1_ref, acc1_ref))):
            cols = slice(half * t, (half + 1) * t)
            sh = shift[:, cols]
            tot8 = jnp.zeros((8, t), F32)
            for g in range(t // rb):
                p = jnp.exp2(src_ref[g * rb:(g + 1) * rb, cols] - sh)
                ph_ref[g * rb:(g + 1) * rb, :] = p.astype(BF16)
                for a in range(0, rb, 8):
                    tot8 = tot8 + p[a:a + 8, :]
            tots.append(jnp.sum(tot8, axis=0, keepdims=True))
            ah_ref[...] = alpha[:, cols] * ah_ref[...] + _dot(vt, ph_ref[...])
        l_ref[...] = alpha * l_ref[...] + jnp.concatenate(tots, axis=1)

    def far(ki, carry):
        tile(ki, None)
        return carry

    lax.fori_loop(0, jnp.maximum(qi - 1, 0), far, 0)

    @pl.when(qi >= 1)
    def _():
        tile(qi - 1, 1)

    tile(qi, 0)
    inv_l = 1.0 / l_ref[...]
    o = (acc0_ref[...] * inv_l[:, 0:t] - lam_ref[0] * (acc1_ref[...] * inv_l[:, t:w2])).T
    o = o * lax.rsqrt(jnp.mean(o * o, axis=-1, keepdims=True) + EPS) * sw_ref[...] * post_scale
    o_ref[...] = o.astype(o_ref.dtype)


def _attn_prompt(q, k, v, lam, rel_bias, tabs, subln, post_scale, bsz, seq, t):
    nq = seq // t
    cfar = rel_bias[NUM_BUCKETS - 1].astype(F32)
    vt = v.reshape(bsz, nq, t, ATT_HEADS, ATT_DV).transpose(0, 3, 1, 4, 2).reshape(bsz * ATT_HEADS, nq, ATT_DV, t)
    smem = BS(memory_space=pltpu.SMEM)
    return pl.pallas_call(
        functools.partial(_attn_body, t=t, rb=16, post_scale=post_scale),
        out_shape=S((bsz * seq, ATT_W), BF16), grid=(bsz, ATT_HEADS, nq),
        in_specs=[smem, smem,
                  BS((t, ATT_DV), lambda b, h, qi: (b * nq + qi, h)),
                  BS((seq, ATT_DV), lambda b, h, qi: (b, h)),
                  BS((None, nq, ATT_DV, t), lambda b, h, qi: (b * ATT_HEADS + h, 0, 0, 0)),
                  BS((None, 2, t, t), lambda b, h, qi: (h, 0, 0, 0)),
                  BS((1, ATT_DV), lambda b, h, qi: (0, 0))],
        out_specs=BS((t, ATT_DV), lambda b, h, qi: (b * nq + qi, h)),
        scratch_shapes=[pltpu.VMEM((2 * t, ATT_DV), BF16), pltpu.VMEM((t, 2 * t), F32), pltpu.VMEM((t, 2 * t), F32),
                        pltpu.VMEM((t, t), BF16), pltpu.VMEM((t, t), BF16),
                        pltpu.VMEM((1, 2 * t), F32), pltpu.VMEM((1, 2 * t), F32),
                        pltpu.VMEM((ATT_DV, t), F32), pltpu.VMEM((ATT_DV, t), F32)],
        compiler_params=_cparams(("parallel", "parallel", "arbitrary")), name="attn_prompt",
    )(lam.reshape(1), cfar, q, k, vt, tabs, subln.reshape(1, ATT_DV))


def _decode_body(pt_ref, lam_ref, q8_ref, kn_ref, vn_ref, btab_ref, sw_ref, *rest, pp, n_steps, post_scale):
    k_refs = rest[:pp]
    v_refs = rest[pp:2 * pp]
    o_ref = rest[2 * pp]
    m_ref, l_ref, acc_ref = rest[2 * pp + 1:]
    s_idx = pl.program_id(1)

    @pl.when(s_idx == 0)
    def _():
        m_ref[...] = jnp.full(m_ref.shape, -jnp.inf, F32)
        l_ref[...] = jnp.zeros(l_ref.shape, F32)
        acc_ref[...] = jnp.zeros(acc_ref.shape, F32)

    q8 = q8_ref[...]
    n_pages = n_steps * pp
    scores = []
    for j in range(pp):
        page = s_idx * pp + j
        bias = jnp.where(page == n_pages - 1, btab_ref[0], btab_ref[1])
        scores.append(_dot_nt(q8, k_refs[j][...].astype(BF16)) + bias)
    s_all = jnp.concatenate(scores, axis=1)
    m_old = m_ref[...]
    m_new = jnp.maximum(m_old, jnp.max(s_all, axis=1, keepdims=True))
    alpha = jnp.exp2(m_old - m_new)
    p_all = jnp.exp2(s_all - m_new)
    l_ref[...] = alpha * l_ref[...] + jnp.sum(p_all, axis=1, keepdims=True)
    ps = p_all.shape[1] // pp
    p16 = p_all.astype(BF16)
    pv = _dot(p16[:, :ps], v_refs[0][...].astype(BF16))
    for j in range(1, pp):
        pv = pv + _dot(p16[:, j * ps:(j + 1) * ps], v_refs[j][...].astype(BF16))
    acc_ref[...] = alpha * acc_ref[...] + pv
    m_ref[...] = m_new

    @pl.when(s_idx == n_steps - 1)
    def _():
        s_self = (jnp.sum(q8.astype(F32) * kn_ref[...].astype(F32), axis=1, keepdims=True)
                  + btab_ref[2][:, 0:1])
        m_o = m_ref[...]
        m_f = jnp.maximum(m_o, s_self)
        a_o = jnp.exp2(m_o - m_f)
        p_s = jnp.exp2(s_self - m_f)
        l_f = a_o * l_ref[...] + p_s
        acc = a_o * acc_ref[...] + p_s * vn_ref[...].astype(F32)
        o8 = acc * (1.0 / l_f)
        lam = lam_ref[0]
        outs = []
        for h in range(ATT_HEADS):
            o = o8[2 * h:2 * h + 1, :] - lam * o8[2 * h + 1:2 * h + 2, :]
            o = o * lax.rsqrt(jnp.mean(o * o, axis=-1, keepdims=True) + EPS) * sw_ref[...] * post_scale
            outs.append(o)
        o_ref[...] = jnp.concatenate(outs, axis=1).astype(o_ref.dtype)


def _attn_decode(layer, q, k_new, v_new, cache_k, cache_v, page_table, lam, rel_bias, subln, post_scale, pp):
    bsz = q.shape[0]
    depth, n_phys, page = cache_k.shape[0], cache_k.shape[1], cache_k.shape[2]
    n_pages = page_table.shape[1]
    n_steps = n_pages // pp
    prow = page * ATT_HEADS
    ck = cache_k.reshape(depth, n_phys, prow, ATT_DV)
    cv = cache_v.reshape(depth, n_phys, prow, ATT_DV)
    rows = jnp.arange(2 * ATT_HEADS)
    qh = jnp.repeat(q.reshape(bsz, ATT_HEADS, ATT_DV), 2, axis=1)
    q8 = jnp.where((jnp.arange(ATT_DV)[None, :] // ATT_DK == rows[:, None] % 2)[None], qh, jnp.zeros((), q.dtype))
    kn8 = jnp.repeat(k_new.reshape(bsz, ATT_HEADS, ATT_DV), 2, axis=1)
    vn8 = jnp.repeat(v_new.reshape(bsz, ATT_HEADS, ATT_DV), 2, axis=1)
    same = (jnp.arange(prow)[None, :] % ATT_HEADS) == (rows[:, None] // 2)
    d_last = page - jnp.arange(prow) // ATT_HEADS
    b_last = jnp.repeat(_bias_by_distance(rel_bias, d_last), 2, axis=0)
    b_far = jnp.broadcast_to(jnp.repeat(rel_bias[NUM_BUCKETS - 1].astype(F32), 2)[:, None], (2 * ATT_HEADS, prow))
    b_self = jnp.broadcast_to(jnp.repeat(rel_bias[0].astype(F32), 2)[:, None], (2 * ATT_HEADS, prow))
    btab = jnp.stack([jnp.where(same, b_last * LOG2E, NEG), jnp.where(same, b_far * LOG2E, NEG), b_self * LOG2E])

    def page_spec(j):
        return BS((None, None, prow, ATT_DV), lambda b, s, pt: (layer, pt[b, s * pp + j], 0, 0))

    const = lambda shp: BS(shp, lambda b, s, pt: tuple(0 for _ in shp))
    per_b = BS((None, 2 * ATT_HEADS, ATT_DV), lambda b, s, pt: (b, 0, 0))
    grid_spec = pltpu.PrefetchScalarGridSpec(
        num_scalar_prefetch=1, grid=(bsz, n_steps),
        in_specs=[BS(memory_space=pltpu.SMEM), per_b, per_b, per_b,
                  const((3, 2 * ATT_HEADS, prow)), const((1, ATT_DV))]
        + [page_spec(j) for j in range(pp)] + [page_spec(j) for j in range(pp)],
        out_specs=BS((None, 1, ATT_W), lambda b, s, pt: (b, 0, 0)),
        scratch_shapes=[pltpu.VMEM((2 * ATT_HEADS, 1), F32), pltpu.VMEM((2 * ATT_HEADS, 1), F32),
                        pltpu.VMEM((2 * ATT_HEADS, ATT_DV), F32)])
    out = pl.pallas_call(
        functools.partial(_decode_body, pp=pp, n_steps=n_steps, post_scale=post_scale),
        out_shape=S((bsz, 1, ATT_W), BF16), grid_spec=grid_spec,
        compiler_params=_cparams(("parallel", "arbitrary")), name="attn_decode",
    )(page_table, lam.reshape(1), q8, kn8, vn8, btab, subln.reshape(1, ATT_DV), *([ck] * pp), *([cv] * pp))
    return out.reshape(bsz, ATT_W)


def _causal_conv(x, prev8, cw_ref, c0, c1):
    acc = x * cw_ref[CONV_K - 1:CONV_K, c0:c1]
    row8 = lax.broadcasted_iota(jnp.int32, prev8.shape, 0)
    for s in range(1, CONV_K):
        r = pltpu.roll(x, s, axis=0)
        pr = pltpu.roll(prev8, s, axis=0)
        head = jnp.where(row8 < s, pr, r[:8])
        sh = jnp.concatenate([head, r[8:]], axis=0) if x.shape[0] > 8 else head
        acc = acc + sh * cw_ref[CONV_K - 1 - s:CONV_K - s, c0:c1]
    return acc


def _tri(cl):
    ii = lax.broadcasted_iota(jnp.int32, (cl, cl), 0)
    jj = lax.broadcasted_iota(jnp.int32, (cl, cl), 1)
    return ii, jj


def _ssd_body(z_ref, xbc_ref, ps_ref, cs0_ref, h0_ref, cw_ref, cb_ref, brow_ref, bcol_ref, arow_ref, acol_ref,
              dexp_ref, nw_ref, ex_ref, y_ref, hl_ref, st_ref, halo_ref, yacc_ref, *, cl, nc, valid_len):
    c = pl.program_id(1)

    @pl.when(c == 0)
    def _():
        st_ref[...] = h0_ref[...]
        halo_ref[...] = cs0_ref[...]

    n_pairs = SSM_HEADS // 2
    pw = 2 * SSM_HEAD_DIM
    states = [st_ref[j] for j in range(n_pairs)]
    x_raw = xbc_ref[...]
    conv = _causal_conv(x_raw, halo_ref[...], cw_ref, 0, SSM_CONV_DIM)
    xbc = _silu(conv + cb_ref[...])
    x = xbc[:, :SSM_D_INNER]
    ps = ps_ref[...]
    step_c = _softplus(ps + brow_ref[...])
    step_r = _softplus(ps.T + bcol_ref[...])
    if valid_len is not None:
        t_c = lax.broadcasted_iota(jnp.int32, step_c.shape, 0) + c * cl
        t_r = lax.broadcasted_iota(jnp.int32, step_r.shape, 1) + c * cl
        step_c = jnp.where(t_c < valid_len, step_c, 0.0)
        step_r = jnp.where(t_r < valid_len, step_r, 0.0)
    la_c = step_c * -jnp.exp(arow_ref[...])
    la_r = (step_r * -jnp.exp(acol_ref[...]))[PS_DT:PS_DT + SSM_HEADS, :]
    ii, jj = _tri(cl)
    incl = ii >= jj
    acs_c = _dot_hi(incl.astype(F32), la_c)
    acs_r = _dot_hi(la_r, (ii <= jj).astype(F32))
    last = acs_c[cl - 1:cl, :]
    e_last = jnp.exp(last)

    def expand(a):
        hi, lo = _split(a)
        return _dot(hi, ex_ref[...]) + _dot(lo, ex_ref[...])

    xs = x * expand(step_c)
    xe = (xs * expand(jnp.exp(last - acs_c))).astype(BF16)
    e_acs = expand(jnp.exp(acs_c))
    xs16 = xs.astype(BF16)
    lane = lax.broadcasted_iota(jnp.int32, (cl, pw), 1)
    srow = lax.broadcasted_iota(jnp.int32, (pw, 1), 0)
    ppg = n_pairs // SSM_GROUPS
    for g in range(SSM_GROUPS):
        b_g = xbc[:, SSM_D_INNER + g * SSM_STATE:SSM_D_INNER + (g + 1) * SSM_STATE].astype(BF16)
        c0 = SSM_D_INNER + SSM_GROUPS * SSM_STATE + g * SSM_STATE
        c_g = xbc[:, c0:c0 + SSM_STATE].astype(BF16)
        cb = _dot_nt(c_g, b_g)
        for r in range(ppg):
            j = g * ppg + r
            cols = slice(j * pw, (j + 1) * pw)
            y_heads = []
            for h in (2 * j, 2 * j + 1):
                dec = jnp.where(incl, jnp.exp(jnp.where(incl, acs_c[:, h:h + 1] - acs_r[h:h + 1, :], 0.0)), 0.0)
                y_heads.append(_dot((cb * dec).astype(BF16), xs16[:, cols]))
            y_diag = jnp.where(lane < SSM_HEAD_DIM, y_heads[0], y_heads[1])
            hst = states[j]
            y_off = _dot_nt(c_g, hst.astype(BF16)) * e_acs[:, cols]
            keep = jnp.where(srow < SSM_HEAD_DIM, e_last[:, 2 * j:2 * j + 1], e_last[:, 2 * j + 1:2 * j + 2])
            states[j] = hst * keep + _dot_tn(xe[:, cols], b_g)
            yacc_ref[:, cols] = y_diag + y_off
    for j in range(n_pairs):
        st_ref[j] = states[j]
    halo_ref[...] = x_raw[cl - 8:, :]
    y = (yacc_ref[...] + dexp_ref[...] * x) * _silu(z_ref[...])
    gw = SSM_D_INNER // SSM_GROUPS
    parts = []
    for g in range(SSM_GROUPS):
        yg = y[:, g * gw:(g + 1) * gw]
        parts.append(yg * lax.rsqrt(jnp.mean(yg * yg, axis=-1, keepdims=True) + EPS))
    y_ref[...] = (jnp.concatenate(parts, axis=1) * nw_ref[...]).astype(y_ref.dtype)

    @pl.when(c == nc - 1)
    def _():
        hl_ref[...] = st_ref[...]


def _ssd(proj, ps, conv0, h0, conv_w, conv_b, vec, ssm_d, ssm_norm, bsz, seq, cl, valid_len):
    nc = seq // cl
    n = bsz * seq
    brow, bcol, arow, acol = vec
    cs0 = jnp.pad(conv0, ((0, 0), (8 - (CONV_K - 1), 0), (0, 0)))
    dexp = jnp.repeat(ssm_d, SSM_HEAD_DIM).reshape(1, SSM_D_INNER)
    expand = (jnp.arange(SSM_D_INNER)[None, :] // SSM_HEAD_DIM == jnp.arange(PS_W)[:, None] - PS_DT).astype(BF16)
    n_pairs, pw = SSM_HEADS // 2, 2 * SSM_HEAD_DIM
    st_shape = (bsz, n_pairs, pw, SSM_STATE)
    if isinstance(h0, tuple):
        h0p = h0[0].reshape((h0[0].shape[0],) + st_shape)
        layer = h0[1]
        h0_spec = BS((None, None, n_pairs, pw, SSM_STATE), lambda b, c: (layer, b, 0, 0, 0))
    else:
        h0p = h0.reshape(st_shape)
        h0_spec = BS((None, n_pairs, pw, SSM_STATE), lambda b, c: (b, 0, 0, 0))
    const = lambda shp: BS(shp, lambda b, c: tuple(0 for _ in shp))
    y, h_last = pl.pallas_call(
        functools.partial(_ssd_body, cl=cl, nc=nc, valid_len=valid_len),
        out_shape=(S((n, SSM_D_INNER), BF16), S(st_shape, F32)), grid=(bsz, nc),
        in_specs=[BS((cl, SSM_D_INNER), lambda b, c: (b * nc + c, C_SZ // SSM_D_INNER)),
                  BS((cl, SSM_CONV_DIM), lambda b, c: (b * nc + c, C_XBC // SSM_CONV_DIM)),
                  BS((cl, PS_W), lambda b, c: (b * nc + c, 0)),
                  BS((None, 8, SSM_CONV_DIM), lambda b, c: (b, 0, 0)),
                  h0_spec,
                  const((CONV_K, SSM_CONV_DIM)), const((1, SSM_CONV_DIM)),
                  const((1, PS_W)), const((PS_W, 1)), const((1, PS_W)), const((PS_W, 1)),
                  const((1, SSM_D_INNER)), const((1, SSM_D_INNER)), const((PS_W, SSM_D_INNER))],
        out_specs=(BS((cl, SSM_D_INNER), lambda b, c: (b * nc + c, 0)),
                   BS((None, n_pairs, pw, SSM_STATE), lambda b, c: (b, 0, 0, 0))),
        scratch_shapes=[pltpu.VMEM((n_pairs, pw, SSM_STATE), F32), pltpu.VMEM((8, SSM_CONV_DIM), F32),
                        pltpu.VMEM((cl, SSM_D_INNER), F32)],
        compiler_params=_cparams(("parallel", "arbitrary")), name="ssd",
    )(proj, proj, ps, cs0, h0p, conv_w, conv_b.reshape(1, -1), brow, bcol, arow, acol, dexp,
      ssm_norm.reshape(1, SSM_D_INNER), expand)
    return y, h_last.reshape(bsz, SSM_HEADS, SSM_HEAD_DIM, SSM_STATE)


def _unit_lower_inverses(mats, ii, jj, cl):
    eye = (ii == jj).astype(F32)
    pair = ((ii >> 1) == (jj >> 1)) & (ii > jj)
    invs = [eye - jnp.where(pair, a, 0.0) for a in mats]
    s = 2
    while s < cl:
        blk = ((ii // (2 * s)) == (jj // (2 * s))) & ((ii % (2 * s)) >= s) & ((jj % (2 * s)) < s)
        xs = [_dot3(inv, jnp.where(blk, a, 0.0)) for inv, a in zip(invs, mats)]
        invs = [inv - _dot3(x, inv) for inv, x in zip(invs, xs)]
        s *= 2
    return invs


def _gdn_body(q_ref, k_ref, v_ref, z_ref, ps_ref, cs0_ref, s0_ref, cw_ref, brow_ref, bcol_ref, arow_ref, acol_ref,
              nw_ref, o_ref, sl_ref, st_ref, halo_ref, *, cl, cps, nc, valid_len):
    c = pl.program_id(1)

    @pl.when(c == 0)
    def _():
        st_ref[...] = s0_ref[...]
        halo_ref[...] = cs0_ref[...]

    w = DN_W
    rows = cl * cps
    halo = halo_ref[...]
    states = [st_ref[h] for h in range(DN_HEADS)]
    raws = (q_ref[...], k_ref[...], v_ref[...])
    q, k, v = (_silu(_causal_conv(raws[p], halo[:, p * w:(p + 1) * w], cw_ref, p * w, (p + 1) * w)) for p in range(3))
    z = z_ref[...]
    ps = ps_ref[...]
    g_c = -jnp.exp(arow_ref[...]) * _softplus(ps + brow_ref[...])
    g_r = -jnp.exp(acol_ref[...]) * _softplus(ps.T + bcol_ref[...])
    beta = _sigmoid(ps)
    if valid_len is not None:
        t_c = lax.broadcasted_iota(jnp.int32, g_c.shape, 0) + c * rows
        t_r = lax.broadcasted_iota(jnp.int32, g_r.shape, 1) + c * rows
        g_c = jnp.where(t_c < valid_len, g_c, 0.0)
        beta = jnp.where(t_c < valid_len, beta, 0.0)
        g_r = jnp.where(t_r < valid_len, g_r, 0.0)
    ii, jj = _tri(cl)
    incl = ii >= jj
    strict = ii > jj
    lower_ones = incl.astype(F32)
    upper_ones = (ii <= jj).astype(F32)
    off = PS_DECAY - 16
    nw = nw_ref[...]
    pre = []
    for ci in range(cps):
        rs = slice(ci * cl, (ci + 1) * cl)
        gcs_c = _dot_hi(lower_ones, g_c[rs, 16:24])
        gcs_r = _dot_hi(g_r[16:24, rs], upper_ones)
        for h in range(DN_HEADS):
            sl = slice(h * DN_DK, (h + 1) * DN_DK)
            qh = q[rs, sl]
            kh = k[rs, sl]
            qh = qh * lax.rsqrt(jnp.sum(qh * qh, axis=-1, keepdims=True) + EPS) * (DN_DK ** -0.5)
            kh = kh * lax.rsqrt(jnp.sum(kh * kh, axis=-1, keepdims=True) + EPS)
            bh = beta[rs, PS_BETA + h:PS_BETA + h + 1]
            col = gcs_c[:, off + h:off + h + 1]
            row = gcs_r[off + h:off + h + 1, :]
            last = gcs_c[cl - 1:cl, off + h:off + h + 1]
            e_col = jnp.exp(col)
            kb = kh * bh
            pre.append(dict(
                dec=jnp.where(incl, jnp.exp(jnp.where(incl, col - row, 0.0)), 0.0),
                q16=qh.astype(BF16), k16=kh.astype(BF16), kb16=kb.astype(BF16),
                vb=v[rs, sl] * bh, kbe=kb * e_col,
                qd16=(qh * e_col).astype(BF16),
                kd16=(kh * jnp.exp(last - col)).astype(BF16),
                g_end=jnp.exp(last),
                gate=nw * _silu(z[rs, sl])))
    lowers = [jnp.where(strict, _dot_nt(t["kb16"], t["k16"]) * t["dec"], 0.0) for t in pre]
    qks = [(_dot_nt(t["q16"], t["k16"]) * t["dec"]).astype(BF16) for t in pre]
    tinvs = _unit_lower_inverses(lowers, ii, jj, cl)
    us = [_dot3(ti, t["vb"]) for ti, t in zip(tinvs, pre)]
    ws = [_dot3(ti, t["kbe"]).astype(BF16) for ti, t in zip(tinvs, pre)]
    outs = []
    for ci in range(cps):
        items = range(ci * DN_HEADS, (ci + 1) * DN_HEADS)
        s16 = [st.astype(BF16) for st in states]
        v16 = [(us[i] - _dot(ws[i], s16[h])).astype(BF16) for h, i in enumerate(items)]
        os_ = [_dot(pre[i]["qd16"], s16[h]) + _dot(qks[i], v16[h]) for h, i in enumerate(items)]
        states = [states[h] * pre[i]["g_end"] + _dot_tn(pre[i]["kd16"], v16[h]) for h, i in enumerate(items)]
        outs += [o * lax.rsqrt(jnp.mean(o * o, axis=-1, keepdims=True) + EPS) * pre[i]["gate"]
                 for o, i in zip(os_, items)]
    for ci in range(cps):
        for h in range(DN_HEADS):
            o_ref[ci * cl:(ci + 1) * cl, h * DN_DV:(h + 1) * DN_DV] = outs[ci * DN_HEADS + h].astype(o_ref.dtype)
    for h in range(DN_HEADS):
        st_ref[h] = states[h]
    for p in range(3):
        halo_ref[:, p * w:(p + 1) * w] = raws[p][rows - 8:, :]

    @pl.when(c == nc - 1)
    def _():
        sl_ref[...] = st_ref[...]


def _gdn(proj, ps, conv0, s0, conv_w, vec, dn_norm, bsz, seq, cl, cps, valid_len):
    rows = cl * cps
    nc = seq // rows
    n = bsz * seq
    brow, bcol, arow, acol = vec
    cs0 = jnp.pad(conv0, ((0, 0), (8 - (CONV_K - 1), 0), (0, 0)))
    w = DN_W
    if isinstance(s0, tuple):
        s0, layer = s0
        s0_spec = BS((None, None, DN_HEADS, DN_DK, DN_DV), lambda b, c: (layer, b, 0, 0, 0))
    else:
        s0_spec = BS((None, DN_HEADS, DN_DK, DN_DV), lambda b, c: (b, 0, 0, 0))
    const = lambda shp: BS(shp, lambda b, c: tuple(0 for _ in shp))
    blk = lambda col: BS((rows, w), lambda b, c: (b * nc + c, col // w))
    o, s_last = pl.pallas_call(
        functools.partial(_gdn_body, cl=cl, cps=cps, nc=nc, valid_len=valid_len),
        out_shape=(S((n, w), BF16), S((bsz, DN_HEADS, DN_DK, DN_DV), F32)), grid=(bsz, nc),
        in_specs=[blk(C_DQ), blk(C_DK), blk(C_DV), blk(C_DZ),
                  BS((rows, PS_W), lambda b, c: (b * nc + c, 0)),
                  BS((None, 8, DN_CONV_DIM), lambda b, c: (b, 0, 0)),
                  s0_spec,
                  const((CONV_K, DN_CONV_DIM)),
                  const((1, PS_W)), const((PS_W, 1)), const((1, PS_W)), const((PS_W, 1)), const((1, DN_DV))],
        out_specs=(BS((rows, w), lambda b, c: (b * nc + c, 0)),
                   BS((None, DN_HEADS, DN_DK, DN_DV), lambda b, c: (b, 0, 0, 0))),
        scratch_shapes=[pltpu.VMEM((DN_HEADS, DN_DK, DN_DV), F32), pltpu.VMEM((8, DN_CONV_DIM), F32)],
        compiler_params=_cparams(("parallel", "arbitrary")), name="gdn",
    )(proj, proj, proj, proj, ps, cs0, s0, conv_w, brow, bcol, arow, acol, dn_norm.reshape(1, DN_DV))
    return o, s_last


def _router_body(h_ref, lw_ref, rw_ref, rb_ref, meta_ref):
    x = h_ref[...]
    xn = x * lax.rsqrt(jnp.mean(x * x, axis=-1, keepdims=True) + EPS) * lw_ref[...]
    logits = _dot3(xn, rw_ref[...]) + rb_ref[...]
    lane = lax.broadcasted_iota(jnp.int32, logits.shape, 1)
    big = jnp.int32(logits.shape[1])
    m1 = jnp.max(logits, axis=1, keepdims=True)
    i1 = jnp.min(jnp.where(logits == m1, lane, big), axis=1, keepdims=True)
    rest = jnp.where(lane == i1, NEG, logits)
    m2 = jnp.max(rest, axis=1, keepdims=True)
    i2 = jnp.min(jnp.where(rest == m2, lane, big), axis=1, keepdims=True)
    e2 = jnp.exp(m2 - m1)
    g1 = 1.0 / (1.0 + e2)
    g2 = e2 * g1
    meta = jnp.where(lane == 0, i1.astype(F32), jnp.where(lane == 1, i2.astype(F32),
                     jnp.where(lane == 2, g1, jnp.where(lane == 3, g2, 0.0))))
    meta_ref[...] = meta


def _router(h, ln_w, router_w, router_b, tm):
    n, d = h.shape
    rw = jnp.pad(router_w, ((0, 0), (0, 128 - N_EXPERTS)))
    rb = jnp.pad(router_b, (0, 128 - N_EXPERTS), constant_values=NEG).reshape(1, 128)
    const = lambda shp: BS(shp, lambda i: (0, 0))
    return pl.pallas_call(
        _router_body, out_shape=S((n, 128), F32), grid=(n // tm,),
        in_specs=[BS((tm, d), lambda i: (i, 0)), const((1, d)), const((d, 128)), const((1, 128))],
        out_specs=BS((tm, 128), lambda i: (i, 0)),
        compiler_params=_cparams(("parallel",)), name="router")(h, ln_w.reshape(1, d), rw, rb)


def _experts_body(be_ref, tok_ref, nused_ref, h_hbm, lw_ref, wg_ref, wu_ref, wd_ref, y_ref,
                  xg_ref, xb_ref, acc_ref, sem, *, tm, nf):
    i = pl.program_id(0)
    f = pl.program_id(1)
    used = i < nused_ref[0]
    slot = lax.rem(i, 2)

    def gather(blk, dst):
        def start(r, carry):
            pltpu.make_async_copy(h_hbm.at[pl.ds(tok_ref[blk * tm + r], 1), :],
                                  xg_ref.at[dst, pl.ds(r, 1), :], sem.at[dst]).start()
            return carry

        lax.fori_loop(0, tm, start, 0, unroll=8)

    @pl.when((i == 0) & (f == 0))
    def _():
        gather(0, 0)

    @pl.when(used & (f == 0))
    def _():
        pltpu.make_async_copy(h_hbm.at[pl.ds(0, tm), :], xg_ref.at[slot], sem.at[slot]).wait()
        x = xg_ref[slot]
        xn = x * lax.rsqrt(jnp.mean(x * x, axis=-1, keepdims=True) + EPS) * lw_ref[...]
        xb_ref[...] = xn.astype(BF16)
        acc_ref[...] = jnp.zeros(acc_ref.shape, F32)

    chunk = tm // nf

    @pl.when(used)
    def _():
        nxt = jnp.minimum(i + 1, nused_ref[0] - 1)
        for r in range(chunk):
            pltpu.make_async_copy(h_hbm.at[pl.ds(tok_ref[nxt * tm + f * chunk + r], 1), :],
                                  xg_ref.at[1 - slot, pl.ds(f * chunk + r, 1), :], sem.at[1 - slot]).start()
        xb = xb_ref[...]
        hid = _silu(_dot(xb, wg_ref[...].astype(BF16))) * _dot(xb, wu_ref[...].astype(BF16))
        acc_ref[...] += _dot(hid.astype(BF16), wd_ref[...].astype(BF16))

    @pl.when((i == nused_ref[0] - 1) & (f == nf - 1))
    def _():
        pltpu.make_async_copy(h_hbm.at[pl.ds(0, tm), :], xg_ref.at[1 - slot], sem.at[1 - slot]).wait()

    @pl.when(used & (f == nf - 1))
    def _():
        y_ref[...] = acc_ref[...]

    @pl.when(jnp.logical_not(used) & (f == nf - 1))
    def _():
        y_ref[...] = jnp.zeros(y_ref.shape, F32)


def _experts(h, ln_w, block_expert, row_token, n_used, wg, wu, wd, tm, tf):
    n, d = h.shape
    rows = row_token.shape[0]
    n_blocks = rows // tm
    (wg, layer), (wu, _), (wd, _) = wg, wu, wd
    ff = wg.shape[3]
    nf = ff // tf
    grid_spec = pltpu.PrefetchScalarGridSpec(
        num_scalar_prefetch=3, grid=(n_blocks, nf),
        in_specs=[BS(memory_space=pl.ANY),
                  BS((1, d), lambda i, f, be, tok, nu: (0, 0)),
                  BS((None, None, d, tf), lambda i, f, be, tok, nu: (layer, be[i], 0, f)),
                  BS((None, None, d, tf), lambda i, f, be, tok, nu: (layer, be[i], 0, f)),
                  BS((None, None, tf, d), lambda i, f, be, tok, nu: (layer, be[i], f, 0))],
        out_specs=BS((tm, d), lambda i, f, be, tok, nu: (i, 0)),
        scratch_shapes=[pltpu.VMEM((2, tm, d), F32), pltpu.VMEM((tm, d), BF16), pltpu.VMEM((tm, d), F32),
                        pltpu.SemaphoreType.DMA((2,))])
    return pl.pallas_call(
        functools.partial(_experts_body, tm=tm, nf=nf),
        out_shape=S((rows, d), F32), grid_spec=grid_spec,
        compiler_params=_cparams(("arbitrary", "arbitrary")), name="experts",
    )(block_expert, row_token, n_used, h, ln_w.reshape(1, d), wg, wu, wd)


def _combine_body(pos_ref, y_hbm, h_ref, meta_ref, o_ref, ya_ref, yb_ref, sem, *, tc, n_steps):
    i = pl.program_id(0)
    slot = lax.rem(i, 2)

    def gather(step, dst):
        def start(r, carry):
            t = step * tc + r
            pltpu.make_async_copy(y_hbm.at[pl.ds(pos_ref[2 * t], 1), :], ya_ref.at[dst, pl.ds(r, 1), :],
                                  sem.at[dst]).start()
            pltpu.make_async_copy(y_hbm.at[pl.ds(pos_ref[2 * t + 1], 1), :], yb_ref.at[dst, pl.ds(r, 1), :],
                                  sem.at[dst]).start()
            return carry

        lax.fori_loop(0, tc, start, 0, unroll=8)

    @pl.when(i == 0)
    def _():
        gather(0, 0)

    @pl.when(i + 1 < n_steps)
    def _():
        gather(i + 1, 1 - slot)

    pltpu.make_async_copy(y_hbm.at[pl.ds(0, tc), :], ya_ref.at[slot], sem.at[slot]).wait()
    pltpu.make_async_copy(y_hbm.at[pl.ds(0, tc), :], yb_ref.at[slot], sem.at[slot]).wait()
    meta = meta_ref[...]
    o_ref[...] = h_ref[...] + (meta[:, 2:3] * ya_ref[slot] + meta[:, 3:4] * yb_ref[slot])


def _combine(h, y_sorted, pos, meta, tc):
    n, d = h.shape
    n_steps = n // tc
    grid_spec = pltpu.PrefetchScalarGridSpec(
        num_scalar_prefetch=1, grid=(n_steps,),
        in_specs=[BS(memory_space=pl.ANY), BS((tc, d), lambda i, pos: (i, 0)), BS((tc, 128), lambda i, pos: (i, 0))],
        out_specs=BS((tc, d), lambda i, pos: (i, 0)),
        scratch_shapes=[pltpu.VMEM((2, tc, d), F32), pltpu.VMEM((2, tc, d), F32), pltpu.SemaphoreType.DMA((2,))])
    return pl.pallas_call(
        functools.partial(_combine_body, tc=tc, n_steps=n_steps), out_shape=S((n, d), F32), grid_spec=grid_spec,
        compiler_params=_cparams(("arbitrary",)), name="moe_combine")(pos, y_sorted, h, meta)


def _moe(h, ln_w, router_w, router_b, wg, wu, wd, tm_r, tm, tf, tc):
    n, d = h.shape
    meta = _router(h, ln_w, router_w, router_b, tm_r)
    idx = meta[:, 0:2].astype(jnp.int32)
    member = jnp.sum(jax.nn.one_hot(idx, N_EXPERTS, dtype=jnp.int32), axis=1)
    before = jnp.cumsum(member, axis=0) - member
    counts = jnp.sum(member, axis=0)
    padded = (counts + tm - 1) // tm * tm
    pad_ends = jnp.cumsum(padded)
    pad_starts = pad_ends - padded
    pos = (pad_starts[idx] + jnp.take_along_axis(before, idx, axis=1)).astype(jnp.int32)
    n_blocks = -(-(2 * n + N_EXPERTS * (tm - 1)) // tm)
    rows = n_blocks * tm
    tok = jnp.broadcast_to(jnp.arange(n, dtype=jnp.int32)[:, None], (n, 2))
    row_token = jnp.zeros((rows,), jnp.int32).at[pos.reshape(-1)].set(tok.reshape(-1))
    n_used = (pad_ends[-1] // tm).astype(jnp.int32).reshape(1)
    blk_start = jnp.minimum(jnp.arange(n_blocks, dtype=jnp.int32), n_used[0] - 1) * tm
    block_expert = jnp.minimum(jnp.searchsorted(pad_ends, blk_start, side="right"), N_EXPERTS - 1).astype(jnp.int32)
    y_sorted = _experts(h, ln_w, block_expert, row_token, n_used, wg, wu, wd, tm, tf)
    return _combine(h, y_sorted, pos.reshape(-1), meta, tc)


def _layer(i, h, p, lw, cfg, att_fn, ssm_conv0, ssm_h0, dn_conv0, dn_s0):
    bsz, seq, tm, cl_ssd, cl_gdn, pad_to, valid_len = (cfg[k] for k in
                                                      ("bsz", "seq", "tm", "cl_ssd", "cl_gdn", "pad_to", "valid_len"))
    n = bsz * seq
    xn = _rmsnorm(h, lw["ln_mix"], tm)
    ident = lambda d, e: d[0]
    tn = 512
    rot = PROJ_A_ROT // tn
    n_a = PROJ_A_W // tn
    tm_in = _pick(n, 2 * tm)
    proj_a = _fused_matmul([xn], [(0, lw["w_in"], 0)], ident, [], F32, PROJ_A_W, tm_in, tn, "in_proj_a",
                           out_tile=lambda j: lax.rem(j + (n_a - rot), n_a))
    proj_b = _fused_matmul([xn], [(0, lw["w_tail"], 0)], ident, [], F32, PROJ_B_W, tm_in, tn, "in_proj_b")
    ps = _fused_matmul([xn], [(0, lw["w_small"], 0)], ident, [], F32, PS_W, tm, PS_W, "in_proj_small")

    q16, k16, v16, k32, v32 = _qkv_prep(proj_a, lw["q_norm"], lw["k_norm"], tm)
    att = att_fn(q16, k16, v16)

    if pad_to is None:
        pa_r, pb_r, ps_r, seq_r = proj_a, proj_b, ps, seq
    else:
        seq_r = pad_to
        padded = lambda a: jnp.pad(a.reshape(bsz, seq, -1), ((0, 0), (0, pad_to - seq), (0, 0))).reshape(bsz * pad_to, -1)
        pa_r, pb_r, ps_r = padded(proj_a), padded(proj_b), padded(ps)
    y, ssm_h1 = _ssd(pa_r, ps_r, ssm_conv0, ssm_h0, lw["ssm_conv_w"], lw["ssm_conv_b"], lw["vec"], lw["ssm_d"],
                     lw["ssm_norm"], bsz, seq_r, cl_ssd, valid_len)
    o, dn_s1 = _gdn(pb_r, ps_r, dn_conv0, dn_s0, lw["dn_conv_w"], lw["vec"], lw["dn_norm"], bsz, seq_r, cl_gdn,
                    cfg["cps_gdn"], valid_len)
    if pad_to is not None:
        y = y.reshape(bsz, pad_to, -1)[:, :seq].reshape(n, -1)
        o = o.reshape(bsz, pad_to, -1)[:, :seq].reshape(n, -1)

    def merge(d, e):
        return _sigmoid(d[0]) * d[3] + _sigmoid(d[1]) * d[4] + _sigmoid(d[2]) * d[5]

    d_m = D_MODEL
    merged = _fused_matmul([xn, att, y, o],
                           [(0, lw["w_gate"], 0), (0, lw["w_gate"], d_m), (0, lw["w_gate"], 2 * d_m),
                            (1, lw["w_up_att"], 0), (2, lw["w_up_ssm"], 0), (3, lw["w_up_dn"], 0)],
                           merge, [], BF16, d_m, tm, 512, "merge")
    resid = lambda d, e: e[0] + d[0]
    h = _fused_matmul([merged], [(0, lw["w_out"], 0)], resid, [h], F32, d_m, tm_in, 512, "out_proj")

    if i % 2 == 0:
        hn = _rmsnorm(h, lw["ln_ffn"], tm)
        d_ff = lw["d_ff"]
        ff = _fused_matmul([hn], [(0, lw["ffn_w_gate"], 0), (0, lw["ffn_w_up"], 0)],
                           lambda d, e: _silu(d[0]) * d[1], [], BF16, d_ff, tm, 1408, "ffn_up")
        h = _fused_matmul([ff], [(0, lw["ffn_w_down"], 0)], resid, [h], F32, d_m, tm, 512, "ffn_down")
    else:
        h = _moe(h, lw["ln_ffn"], lw["router_w"], lw["router_b"], lw["exp_w_gate"], lw["exp_w_up"], lw["exp_w_down"],
                 min(tm, 512), cfg["tm_moe"], TF_MOE, cfg["tc"])

    hn = _rmsnorm(h, lw["ln_ple"], tm)
    h = _fused_matmul([hn, p.astype(BF16)], [(0, lw["ple_w_gate"], 0), (1, lw["ple_w_proj"], 0)],
                      lambda d, e: e[0] + _sigmoid(d[0]) * d[1], [h], F32, d_m, tm_in, 512, "ple")

    keep = CONV_K - 1
    tail = max(seq - keep, 0)
    xbc_raw = proj_a.reshape(bsz, seq, -1)[:, tail:, C_XBC:C_XBC + SSM_CONV_DIM]
    dn_raw = proj_b.reshape(bsz, seq, -1)[:, tail:, C_DQ:C_DQ + DN_CONV_DIM]
    if seq >= keep:
        ssm_conv1, dn_conv1 = xbc_raw, dn_raw
    else:
        ssm_conv1 = jnp.concatenate([ssm_conv0, xbc_raw], axis=1)[:, -keep:]
        dn_conv1 = jnp.concatenate([dn_conv0, dn_raw], axis=1)[:, -keep:]
    new_k = k32.reshape(bsz, seq, ATT_HEADS, 2 * ATT_DK)
    new_v = v32.reshape(bsz, seq, ATT_HEADS, ATT_DV)
    return h, (new_k, new_v, ssm_conv1, ssm_h1, dn_conv1, dn_s1)


def _prep_layer_weights(i, W):
    w_in = W["w_in"][i]
    o_dt = PROJ_A_W
    o_tail = o_dt + SSM_HEADS
    o_beta = o_tail + PROJ_B_W
    w_tail = w_in[:, o_tail:o_beta].astype(BF16)
    w_small = jnp.concatenate([w_in[:, o_dt:o_tail], w_in[:, o_beta:o_beta + 2 * DN_HEADS],
                               jnp.zeros((D_MODEL, PS_W - SSM_HEADS - 2 * DN_HEADS), w_in.dtype)], axis=1)
    bias_row = jnp.zeros((PS_W,), F32).at[PS_DT:PS_DT + SSM_HEADS].set(W["ssm_dt_bias"][i])
    bias_row = bias_row.at[PS_DECAY:PS_DECAY + DN_HEADS].set(W["dn_dt_bias"][i])
    alog_row = jnp.zeros((PS_W,), F32).at[PS_DT:PS_DT + SSM_HEADS].set(W["ssm_a_log"][i])
    alog_row = alog_row.at[PS_DECAY:PS_DECAY + DN_HEADS].set(W["dn_a_log"][i])
    vec = (bias_row.reshape(1, PS_W), bias_row.reshape(PS_W, 1), alog_row.reshape(1, PS_W), alog_row.reshape(PS_W, 1))
    lw = {
        "ln_mix": W["ln_mix"][i], "w_in": (W["w_in"], i), "w_tail": w_tail, "w_small": w_small.astype(BF16),
        "q_norm": W["q_norm"][i], "k_norm": W["k_norm"][i], "att_subln": W["att_subln"][i],
        "ssm_conv_w": W["ssm_conv_w"][i], "ssm_conv_b": W["ssm_conv_b"][i], "vec": vec,
        "ssm_d": W["ssm_d"][i], "ssm_norm": W["ssm_norm"][i],
        "dn_conv_w": W["dn_conv_w"][i], "dn_norm": W["dn_norm"][i],
        "w_gate": (W["w_gate"], i), "w_up_att": (W["w_up_att"], i), "w_up_ssm": (W["w_up_ssm"], i),
        "w_up_dn": (W["w_up_dn"], i), "w_out": (W["w_out"], i),
        "ln_ffn": W["ln_ffn"][i], "ln_ple": W["ln_ple"][i],
        "ple_w_gate": (W["ple_w_gate"], i), "ple_w_proj": (W["ple_w_proj"], i),
    }
    if i % 2 == 0:
        lw.update(ffn_w_gate=(W["ffn_w_gate"], i // 2), ffn_w_up=(W["ffn_w_up"], i // 2),
                  ffn_w_down=(W["ffn_w_down"], i // 2), d_ff=W["ffn_w_gate"].shape[2])
    else:
        lw.update(router_w=W["router_w"][i // 2], router_b=W["router_b"][i // 2],
                  exp_w_gate=(W["exp_w_gate"], i // 2), exp_w_up=(W["exp_w_up"], i // 2),
                  exp_w_down=(W["exp_w_down"], i // 2))
    lam_init = 0.8 - 0.6 * math.exp(-0.3 * i)
    lam = (jnp.exp(jnp.sum(W["lam_q1"][i] * W["lam_k1"][i])) - jnp.exp(jnp.sum(W["lam_q2"][i] * W["lam_k2"][i]))
           + lam_init).astype(F32)
    return lw, lam, lam_init


def _pick(n, pref):
    t = min(n, pref)
    while n % t:
        t //= 2
    return t


def kernel(x_prompt, x_sample, cache_k, cache_v, state_ssm_conv, state_ssm, state_dn_conv, state_dn, page_table, p_prompt, p_sample, ln_mix, w_in, w_gate, q_norm, k_norm, lam_q1, lam_k1, lam_q2, lam_k2, att_subln, rel_bias, ssm_conv_w, ssm_conv_b, ssm_dt_bias, ssm_a_log, ssm_d, ssm_norm, dn_conv_w, dn_dt_bias, dn_a_log, dn_norm, w_up_att, w_up_ssm, w_up_dn, w_out, ln_ffn, ffn_w_gate, ffn_w_up, ffn_w_down, router_w, router_b, exp_w_gate, exp_w_up, exp_w_down, ln_ple, ple_w_gate, ple_w_proj):
    W = dict(ln_mix=ln_mix, w_in=w_in, w_gate=w_gate, q_norm=q_norm, k_norm=k_norm, lam_q1=lam_q1, lam_k1=lam_k1,
             lam_q2=lam_q2, lam_k2=lam_k2, att_subln=att_subln, ssm_conv_w=ssm_conv_w, ssm_conv_b=ssm_conv_b,
             ssm_dt_bias=ssm_dt_bias, ssm_a_log=ssm_a_log, ssm_d=ssm_d, ssm_norm=ssm_norm, dn_conv_w=dn_conv_w,
             dn_dt_bias=dn_dt_bias, dn_a_log=dn_a_log, dn_norm=dn_norm, w_up_att=w_up_att, w_up_ssm=w_up_ssm,
             w_up_dn=w_up_dn, w_out=w_out, ln_ffn=ln_ffn, ffn_w_gate=ffn_w_gate, ffn_w_up=ffn_w_up,
             ffn_w_down=ffn_w_down, router_w=router_w, router_b=router_b, exp_w_gate=exp_w_gate, exp_w_up=exp_w_up,
             exp_w_down=exp_w_down, ln_ple=ln_ple, ple_w_gate=ple_w_gate, ple_w_proj=ple_w_proj)
    depth = ln_mix.shape[0]
    bp, lp, d = x_prompt.shape
    bs, ls, _ = x_sample.shape
    n_pages = page_table.shape[1]
    t_att = _pick(lp, T_ATT)
    cfg_p = dict(bsz=bp, seq=lp, tm=_pick(bp * lp, TM), cl_ssd=_pick(lp, CL_SSD), cl_gdn=_pick(lp, CL_GDN), pad_to=None,
                 valid_len=None, tm_moe=_pick(bp * lp, TM_MOE), tc=_pick(bp * lp, TC_MOE))
    cfg_p["cps_gdn"] = _pick(lp // cfg_p["cl_gdn"], CPS_GDN)
    cfg_s = dict(bsz=bs, seq=ls, tm=bs * ls, cl_ssd=CL_SAMPLE, cl_gdn=CL_SAMPLE, cps_gdn=1, pad_to=CL_SAMPLE,
                 valid_len=ls,
                 tm_moe=64, tc=bs * ls)
    hp = x_prompt.reshape(bp * lp, d)
    hs = x_sample.reshape(bs * ls, d)
    st_p, st_s = [], []
    tabs = _bias_tables(rel_bias, t_att)
    for i in range(depth):
        lw, lam, lam_init = _prep_layer_weights(i, W)
        post = 1.0 - lam_init
        att_p = lambda q, k, v: _attn_prompt(q, k, v, lam, rel_bias, tabs, lw["att_subln"], post, bp, lp, t_att)
        hp, st = _layer(i, hp, p_prompt[i].reshape(bp * lp, -1), lw, cfg_p, att_p,
                        jnp.zeros((bp, CONV_K - 1, SSM_CONV_DIM), F32),
                        jnp.zeros((bp, SSM_HEADS, SSM_HEAD_DIM, SSM_STATE), F32),
                        jnp.zeros((bp, CONV_K - 1, DN_CONV_DIM), F32),
                        jnp.zeros((bp, DN_HEADS, DN_DK, DN_DV), F32))
        st_p.append(st)
        att_s = lambda q, k, v: _attn_decode(i, q, k, v, cache_k, cache_v, page_table, lam, rel_bias,
                                             lw["att_subln"], post, _pick(n_pages, PAGES_PER_STEP))
        hs, st = _layer(i, hs, p_sample[i].reshape(bs * ls, -1), lw, cfg_s, att_s,
                        state_ssm_conv[i], (state_ssm, i), state_dn_conv[i], (state_dn, i))
        st_s.append(st)
    stk = lambda sts, j: jnp.stack([s[j] for s in sts])
    return (hp.reshape(bp, lp, d), hs.reshape(bs, ls, d),
            stk(st_p, 0), stk(st_p, 1), stk(st_p, 2), stk(st_p, 3), stk(st_p, 4), stk(st_p, 5),
            stk(st_s, 0), stk(st_s, 1), stk(st_s, 2), stk(st_s, 3), stk(st_s, 4), stk(st_s, 5))
```

```python
import functools
import math

import jax
import jax.numpy as jnp
from jax import lax
from jax.experimental import pallas as pl
from jax.experimental.pallas import tpu as pltpu

F32 = jnp.float32
BF16 = jnp.bfloat16
S = jax.ShapeDtypeStruct
BS = pl.BlockSpec

D_MODEL = 1024
ATT_HEADS = 4
ATT_DK = 64
ATT_DV = 128
ATT_W = ATT_HEADS * ATT_DV
NUM_BUCKETS = 32
MAX_DISTANCE = 128
SSM_HEADS = 16
SSM_HEAD_DIM = 64
SSM_GROUPS = 2
SSM_STATE = 128
SSM_D_INNER = 1024
SSM_CONV_DIM = 1536
DN_HEADS = 4
DN_DK = 128
DN_DV = 128
DN_W = 512
DN_CONV_DIM = 1536
CONV_K = 4
N_EXPERTS = 8
EPS = 1e-6
NEG = -1e30
LOG2E = 1.4426950408889634

PROJ_A_W, PROJ_A_ROT = 4096, 2560
C_XBC, C_AQ, C_AK, C_AV, C_SZ = 0, 1536, 2048, 2560, 3072
PROJ_B_W = 2048
C_DQ, C_DK, C_DV, C_DZ = 0, 512, 1024, 1536
PS_DT, PS_BETA, PS_DECAY, PS_W = 0, 16, 20, 128

VMEM_LIMIT = 56 * 1024 * 1024
T_ATT = 512
TM = 1024
CL_SSD = 128
CL_GDN = 64
CPS_GDN = 4
CL_SAMPLE = 16
PAGES_PER_STEP = 32
TM_MOE = 512
TF_MOE = 896
TC_MOE = 256


def _cparams(sem):
    return pltpu.CompilerParams(dimension_semantics=sem, vmem_limit_bytes=VMEM_LIMIT)


def _dot(a, b):
    return jnp.dot(a, b, preferred_element_type=F32)


def _dot_nt(a, b):
    return lax.dot_general(a, b, (((1,), (1,)), ((), ())), preferred_element_type=F32)


def _dot_tn(a, b):
    return lax.dot_general(a, b, (((0,), (0,)), ((), ())), preferred_element_type=F32)


def _dot_hi(a, b):
    return jnp.dot(a, b, preferred_element_type=F32, precision=lax.Precision.HIGHEST)


def _split(a):
    hi = a.astype(BF16)
    lo = (a - hi.astype(F32)).astype(BF16)
    return hi, lo


def _dot3(a, b):
    ah, al = _split(a)
    bh, bl = _split(b)
    return _dot(ah, bh) + (_dot(ah, bl) + _dot(al, bh))


def _sigmoid(x):
    return 1.0 / (1.0 + jnp.exp(-x))


def _silu(x):
    return x * _sigmoid(x)


def _softplus(x):
    return jnp.maximum(x, 0.0) + jnp.log1p(jnp.exp(-jnp.abs(x)))


def _rmsnorm_body(x_ref, w_ref, o_ref):
    x = x_ref[...]
    y = x * lax.rsqrt(jnp.mean(x * x, axis=-1, keepdims=True) + EPS) * w_ref[...]
    o_ref[...] = y.astype(o_ref.dtype)


def _rmsnorm(x, w, tm):
    n, d = x.shape
    return pl.pallas_call(
        _rmsnorm_body, out_shape=S((n, d), BF16), grid=(n // tm,),
        in_specs=[BS((tm, d), lambda i: (i, 0)), BS((1, d), lambda i: (0, 0))],
        out_specs=BS((tm, d), lambda i: (i, 0)),
        compiler_params=_cparams(("parallel",)), name="rmsnorm")(x, w.reshape(1, d))


def _fused_matmul(acts, pairs, combine, extras, out_dtype, m, tm, tn, name, out_tile=None):
    n = acts[0].shape[0]
    na, npair, nex = len(acts), len(pairs), len(extras)
    act_idx = [a for a, _, _ in pairs]

    def body(*refs):
        a_vals = [r[...] for r in refs[:na]]
        dots = [_dot(a_vals[act_idx[j]], refs[na + j][...].astype(BF16)) for j in range(npair)]
        ex = [r[...] for r in refs[na + npair:na + npair + nex]]
        o_ref = refs[na + npair + nex]
        o_ref[...] = combine(dots, ex).astype(o_ref.dtype)

    def w_spec(w, c0):
        off = c0 // tn
        if isinstance(w, tuple):
            arr, layer = w
            return BS((None, arr.shape[1], tn), lambda i, j: (layer, 0, j + off))
        return BS((w.shape[0], tn), lambda i, j: (0, j + off))

    in_specs = ([BS((tm, a.shape[1]), lambda i, j: (i, 0)) for a in acts]
                + [w_spec(w, c0) for _, w, c0 in pairs]
                + [BS((tm, tn), lambda i, j: (i, j)) for _ in extras])
    out_tile = out_tile or (lambda j: j)
    return pl.pallas_call(
        body, out_shape=S((n, m), out_dtype), grid=(n // tm, m // tn),
        in_specs=in_specs, out_specs=BS((tm, tn), lambda i, j: (i, out_tile(j))),
        compiler_params=_cparams(("parallel", "arbitrary")), name=name,
    )(*acts, *[w[0] if isinstance(w, tuple) else w for _, w, _ in pairs], *extras)


def _qkv_body(aq_ref, ak_ref, av_ref, g_ref, qw_ref, kw_ref, *rest):
    q_ref, kb_ref, vb_ref, kf_ref, vf_ref = rest[-5:]
    g = g_ref[...]

    def gnorm(x, w):
        hi, lo = _split(x * x)
        ss = _dot(hi, g) + _dot(lo, g)
        return x * lax.rsqrt(ss * (1.0 / ATT_DK) + EPS) * w

    q = gnorm(aq_ref[...], qw_ref[...]) * (ATT_DK ** -0.5 * LOG2E)
    k = gnorm(ak_ref[...], kw_ref[...])
    v = av_ref[...]
    q_ref[...] = q.astype(BF16)
    kb_ref[...] = k.astype(BF16)
    vb_ref[...] = v.astype(BF16)
    for h in range(ATT_HEADS):
        kf_ref[:, h, :] = k[:, h * ATT_DV:(h + 1) * ATT_DV]
        vf_ref[:, h, :] = v[:, h * ATT_DV:(h + 1) * ATT_DV]


def _qkv_prep(proj, q_norm, k_norm, tm, layer, depth, kv_rows=None):
    n = proj.shape[0]
    w = ATT_W
    gi = jnp.arange(w) // ATT_DK
    gmat = (gi[:, None] == gi[None, :]).astype(BF16)
    qw = jnp.tile(q_norm, w // ATT_DK).reshape(1, w)
    kw = jnp.tile(k_norm, w // ATT_DK).reshape(1, w)
    row = lambda c: BS((tm, w), lambda i: (i, c))
    const = lambda shp: BS(shp, lambda i: (0, 0))
    thd = BS((None, tm, ATT_HEADS, ATT_DV), lambda i: (layer, i, 0, 0))
    rows_shape = S((depth, n, ATT_HEADS, ATT_DV), F32)
    in_specs = [row(C_AQ // w), row(C_AK // w), row(C_AV // w), const((w, w)), const((1, w)), const((1, w))]
    args = [proj, proj, proj, gmat, qw, kw]
    aliases = {}
    if kv_rows is not None:
        aliases = {len(args): 3, len(args) + 1: 4}
        in_specs += [BS(memory_space=pl.ANY), BS(memory_space=pl.ANY)]
        args += list(kv_rows)
    return pl.pallas_call(
        _qkv_body,
        out_shape=(S((n, w), BF16), S((n, w), BF16), S((n, w), BF16), rows_shape, rows_shape),
        grid=(n // tm,), in_specs=in_specs, out_specs=(row(0), row(0), row(0), thd, thd),
        input_output_aliases=aliases,
        compiler_params=_cparams(("parallel",)), name="qkv_prep")(*args)


def _t5_bucket(n):
    max_exact = NUM_BUCKETS // 2
    scaled = jnp.log(jnp.maximum(n, 1).astype(F32) / max_exact) / math.log(MAX_DISTANCE / max_exact)
    large = jnp.minimum(max_exact + (scaled * (NUM_BUCKETS - max_exact)).astype(jnp.int32), NUM_BUCKETS - 1)
    return jnp.where(n < max_exact, n, large)


def _bias_by_distance(rel_bias, n):
    return jnp.moveaxis(rel_bias[_t5_bucket(n)], -1, 0).astype(F32)


def _bias_table_body(rb_ref, o_ref, *, t, rows):
    h = pl.program_id(0)
    max_exact = NUM_BUCKETS // 2

    def block(i, carry):
        r0 = pl.multiple_of(i * rows, rows)
        c = lax.broadcasted_iota(jnp.int32, (rows, t), 0) + r0
        r = lax.broadcasted_iota(jnp.int32, (rows, t), 1)
        for idx in range(2):
            n = r - c + idx * t
            nn = jnp.maximum(n, 0)
            scaled = jnp.log(jnp.maximum(nn, 1).astype(F32) / max_exact) / math.log(MAX_DISTANCE / max_exact)
            large = jnp.minimum(max_exact + (scaled * (NUM_BUCKETS - max_exact)).astype(jnp.int32), NUM_BUCKETS - 1)
            bucket = jnp.where(nn < max_exact, nn, large)
            val = jnp.zeros((rows, t), F32)
            for b in range(NUM_BUCKETS):
                val = jnp.where(bucket == b, rb_ref[b, h] * LOG2E, val)
            if idx == 0:
                val = jnp.where(n >= 0, val, NEG)
            o_ref[idx, pl.ds(r0, rows), :] = val
        return carry

    lax.fori_loop(0, t // rows, block, 0)


def _bias_tables(rel_bias, t):
    return pl.pallas_call(
        functools.partial(_bias_table_body, t=t, rows=8),
        out_shape=S((ATT_HEADS, 2, t, t), F32), grid=(ATT_HEADS,),
        in_specs=[BS(memory_space=pltpu.SMEM)],
        out_specs=BS((None, 2, t, t), lambda h: (h, 0, 0, 0)),
        compiler_params=_cparams(("parallel",)), name="bias_tables")(rel_bias.astype(F32))


def _attn_body(lam_ref, cfar_ref, q_ref, k_ref, vt_ref, tab_ref, sw_ref, o_ref,
               q2_ref, s_ref, sb_ref, p0_ref, p1_ref, m_ref, l_ref, acc0_ref, acc1_ref, *, t, rb, post_scale):
    h = pl.program_id(1)
    qi = pl.program_id(2)
    q = q_ref[...]
    lane = lax.broadcasted_iota(jnp.int32, q.shape, 1)
    zero = jnp.zeros_like(q)
    q2_ref[0:t, :] = jnp.where(lane < ATT_DK, q, zero)
    q2_ref[t:2 * t, :] = jnp.where(lane >= ATT_DK, q, zero)
    m_ref[...] = jnp.full(m_ref.shape, -jnp.inf, F32)
    l_ref[...] = jnp.zeros(l_ref.shape, F32)
    acc0_ref[...] = jnp.zeros(acc0_ref.shape, F32)
    acc1_ref[...] = jnp.zeros(acc1_ref.shape, F32)
    cfar = cfar_ref[h] * LOG2E
    w2 = 2 * t

    def tile(ki, tab_idx):
        s_ref[...] = _dot_nt(k_ref[pl.ds(pl.multiple_of(ki * t, t), t), :], q2_ref[...])
        src_ref = s_ref if tab_idx is None else sb_ref

        def key_max(g, mx):
            rows = pl.ds(pl.multiple_of(g * rb, rb), rb)
            s = s_ref[rows, :]
            if tab_idx is not None:
                tab = tab_ref[tab_idx, rows, :]
                s = s + jnp.concatenate([tab, tab], axis=1)
                sb_ref[rows, :] = s
            for a in range(0, rb, 8):
                mx = jnp.maximum(mx, s[a:a + 8, :])
            return mx

        mx8 = lax.fori_loop(0, t // rb, key_max, jnp.full((8, w2), -jnp.inf, F32), unroll=4)
        tile_max = jnp.max(mx8, axis=0, keepdims=True)
        if tab_idx is None:
            tile_max = tile_max + cfar
        m_old = m_ref[...]
        m_new = jnp.maximum(m_old, tile_max)
        alpha = jnp.exp2(m_old - m_new)
        m_ref[...] = m_new
        shift = jnp.broadcast_to(m_new if tab_idx is not None else m_new - cfar, (rb, w2))

        vt = vt_ref[ki]
        tots = []
        for half, (ph_ref, ah_ref) in enumerate(((p0_ref, acc0_ref), (p1_ref, acc1_ref))):
            cols = slice(half * t, (half + 1) * t)
            sh = shift[:, cols]
            tot8 = jnp.zeros((8, t), F32)
            for g in range(t // rb):
                p = jnp.exp2(src_ref[g * rb:(g + 1) * rb, cols] - sh)
                ph_ref[g * rb:(g + 1) * rb, :] = p.astype(BF16)
                for a in range(0, rb, 8):
                    tot8 = tot8 + p[a:a + 8, :]
            tots.append(jnp.sum(tot8, axis=0, keepdims=True))
            ah_ref[...] = alpha[:, cols] * ah_ref[...] + _dot(vt, ph_ref[...])
        l_ref[...] = alpha * l_ref[...] + jnp.concatenate(tots, axis=1)

    def far(ki, carry):
        tile(ki, None)
        return carry

    lax.fori_loop(0, jnp.maximum(qi - 1, 0), far, 0)

    @pl.when(qi >= 1)
    def _():
        tile(qi - 1, 1)

    tile(qi, 0)
    inv_l = 1.0 / l_ref[...]
    o = (acc0_ref[...] * inv_l[:, 0:t] - lam_ref[0] * (acc1_ref[...] * inv_l[:, t:w2])).T
    o = o * lax.rsqrt(jnp.mean(o * o, axis=-1, keepdims=True) + EPS) * sw_ref[...] * post_scale
    o_ref[...] = o.astype(o_ref.dtype)


def _attn_prompt(q, k, v, lam, rel_bias, tabs, subln, post_scale, bsz, seq, t):
    nq = seq // t
    cfar = rel_bias[NUM_BUCKETS - 1].astype(F32)
    vt = v.reshape(bsz, nq, t, ATT_HEADS, ATT_DV).transpose(0, 3, 1, 4, 2).reshape(bsz * ATT_HEADS, nq, ATT_DV, t)
    smem = BS(memory_space=pltpu.SMEM)
    return pl.pallas_call(
        functools.partial(_attn_body, t=t, rb=16, post_scale=post_scale),
        out_shape=S((bsz * seq, ATT_W), BF16), grid=(bsz, ATT_HEADS, nq),
        in_specs=[smem, smem,
                  BS((t, ATT_DV), lambda b, h, qi: (b * nq + qi, h)),
                  BS((seq, ATT_DV), lambda b, h, qi: (b, h)),
                  BS((None, nq, ATT_DV, t), lambda b, h, qi: (b * ATT_HEADS + h, 0, 0, 0)),
                  BS((None, 2, t, t), lambda b, h, qi: (h, 0, 0, 0)),
                  BS((1, ATT_DV), lambda b, h, qi: (0, 0))],
        out_specs=BS((t, ATT_DV), lambda b, h, qi: (b * nq + qi, h)),
        scratch_shapes=[pltpu.VMEM((2 * t, ATT_DV), BF16), pltpu.VMEM((t, 2 * t), F32), pltpu.VMEM((t, 2 * t), F32),
                        pltpu.VMEM((t, t), BF16), pltpu.VMEM((t, t), BF16),
                        pltpu.VMEM((1, 2 * t), F32), pltpu.VMEM((1, 2 * t), F32),
                        pltpu.VMEM((ATT_DV, t), F32), pltpu.VMEM((ATT_DV, t), F32)],
        compiler_params=_cparams(("parallel", "parallel", "arbitrary")), name="attn_prompt",
    )(lam.reshape(1), cfar, q, k, vt, tabs, subln.reshape(1, ATT_DV))


def _decode_body(pt_ref, lam_ref, q8_ref, kn_ref, vn_ref, btab_ref, sw_ref, *rest, pp, n_steps, post_scale):
    k_refs = rest[:pp]
    v_refs = rest[pp:2 * pp]
    o_ref = rest[2 * pp]
    m_ref, l_ref, acc_ref = rest[2 * pp + 1:]
    s_idx = pl.program_id(1)

    @pl.when(s_idx == 0)
    def _():
        m_ref[...] = jnp.full(m_ref.shape, -jnp.inf, F32)
        l_ref[...] = jnp.zeros(l_ref.shape, F32)
        acc_ref[...] = jnp.zeros(acc_ref.shape, F32)

    q8 = q8_ref[...]
    n_pages = n_steps * pp
    scores = []
    for j in range(pp):
        page = s_idx * pp + j
        bias = jnp.where(page == n_pages - 1, btab_ref[0], btab_ref[1])
        scores.append(_dot_nt(q8, k_refs[j][...].astype(BF16)) + bias)
    s_all = jnp.concatenate(scores, axis=1)
    m_old = m_ref[...]
    m_new = jnp.maximum(m_old, jnp.max(s_all, axis=1, keepdims=True))
    alpha = jnp.exp2(m_old - m_new)
    p_all = jnp.exp2(s_all - m_new)
    l_ref[...] = alpha * l_ref[...] + jnp.sum(p_all, axis=1, keepdims=True)
    ps = p_all.shape[1] // pp
    p16 = p_all.astype(BF16)
    pv = _dot(p16[:, :ps], v_refs[0][...].astype(BF16))
    for j in range(1, pp):
        pv = pv + _dot(p16[:, j * ps:(j + 1) * ps], v_refs[j][...].astype(BF16))
    acc_ref[...] = alpha * acc_ref[...] + pv
    m_ref[...] = m_new

    @pl.when(s_idx == n_steps - 1)
    def _():
        s_self = (jnp.sum(q8.astype(F32) * kn_ref[...].astype(F32), axis=1, keepdims=True)
                  + btab_ref[2][:, 0:1])
        m_o = m_ref[...]
        m_f = jnp.maximum(m_o, s_self)
        a_o = jnp.exp2(m_o - m_f)
        p_s = jnp.exp2(s_self - m_f)
        l_f = a_o * l_ref[...] + p_s
        acc = a_o * acc_ref[...] + p_s * vn_ref[...].astype(F32)
        o8 = acc * (1.0 / l_f)
        lam = lam_ref[0]
        outs = []
        for h in range(ATT_HEADS):
            o = o8[2 * h:2 * h + 1, :] - lam * o8[2 * h + 1:2 * h + 2, :]
            o = o * lax.rsqrt(jnp.mean(o * o, axis=-1, keepdims=True) + EPS) * sw_ref[...] * post_scale
            outs.append(o)
        o_ref[...] = jnp.concatenate(outs, axis=1).astype(o_ref.dtype)


def _attn_decode(layer, q, k_new, v_new, cache_k, cache_v, page_table, lam, rel_bias, subln, post_scale, pp):
    bsz = q.shape[0]
    depth, n_phys, page = cache_k.shape[0], cache_k.shape[1], cache_k.shape[2]
    n_pages = page_table.shape[1]
    n_steps = n_pages // pp
    prow = page * ATT_HEADS
    ck = cache_k.reshape(depth, n_phys, prow, ATT_DV)
    cv = cache_v.reshape(depth, n_phys, prow, ATT_DV)
    rows = jnp.arange(2 * ATT_HEADS)
    qh = jnp.repeat(q.reshape(bsz, ATT_HEADS, ATT_DV), 2, axis=1)
    q8 = jnp.where((jnp.arange(ATT_DV)[None, :] // ATT_DK == rows[:, None] % 2)[None], qh, jnp.zeros((), q.dtype))
    kn8 = jnp.repeat(k_new.reshape(bsz, ATT_HEADS, ATT_DV), 2, axis=1)
    vn8 = jnp.repeat(v_new.reshape(bsz, ATT_HEADS, ATT_DV), 2, axis=1)
    same = (jnp.arange(prow)[None, :] % ATT_HEADS) == (rows[:, None] // 2)
    d_last = page - jnp.arange(prow) // ATT_HEADS
    b_last = jnp.repeat(_bias_by_distance(rel_bias, d_last), 2, axis=0)
    b_far = jnp.broadcast_to(jnp.repeat(rel_bias[NUM_BUCKETS - 1].astype(F32), 2)[:, None], (2 * ATT_HEADS, prow))
    b_self = jnp.broadcast_to(jnp.repeat(rel_bias[0].astype(F32), 2)[:, None], (2 * ATT_HEADS, prow))
    btab = jnp.stack([jnp.where(same, b_last * LOG2E, NEG), jnp.where(same, b_far * LOG2E, NEG), b_self * LOG2E])

    def page_spec(j):
        return BS((None, None, prow, ATT_DV), lambda b, s, pt: (layer, pt[b, s * pp + j], 0, 0))

    const = lambda shp: BS(shp, lambda b, s, pt: tuple(0 for _ in shp))
    per_b = BS((None, 2 * ATT_HEADS, ATT_DV), lambda b, s, pt: (b, 0, 0))
    grid_spec = pltpu.PrefetchScalarGridSpec(
        num_scalar_prefetch=1, grid=(bsz, n_steps),
        in_specs=[BS(memory_space=pltpu.SMEM), per_b, per_b, per_b,
                  const((3, 2 * ATT_HEADS, prow)), const((1, ATT_DV))]
        + [page_spec(j) for j in range(pp)] + [page_spec(j) for j in range(pp)],
        out_specs=BS((None, 1, ATT_W), lambda b, s, pt: (b, 0, 0)),
        scratch_shapes=[pltpu.VMEM((2 * ATT_HEADS, 1), F32), pltpu.VMEM((2 * ATT_HEADS, 1), F32),
                        pltpu.VMEM((2 * ATT_HEADS, ATT_DV), F32)])
    out = pl.pallas_call(
        functools.partial(_decode_body, pp=pp, n_steps=n_steps, post_scale=post_scale),
        out_shape=S((bsz, 1, ATT_W), BF16), grid_spec=grid_spec,
        compiler_params=_cparams(("parallel", "arbitrary")), name="attn_decode",
    )(page_table, lam.reshape(1), q8, kn8, vn8, btab, subln.reshape(1, ATT_DV), *([ck] * pp), *([cv] * pp))
    return out.reshape(bsz, ATT_W)


def _causal_conv(x, prev8, cw_ref, c0, c1):
    acc = x * cw_ref[CONV_K - 1:CONV_K, c0:c1]
    row8 = lax.broadcasted_iota(jnp.int32, prev8.shape, 0)
    for s in range(1, CONV_K):
        r = pltpu.roll(x, s, axis=0)
        pr = pltpu.roll(prev8, s, axis=0)
        head = jnp.where(row8 < s, pr, r[:8])
        sh = jnp.concatenate([head, r[8:]], axis=0) if x.shape[0] > 8 else head
        acc = acc + sh * cw_ref[CONV_K - 1 - s:CONV_K - s, c0:c1]
    return acc


def _tri(cl):
    ii = lax.broadcasted_iota(jnp.int32, (cl, cl), 0)
    jj = lax.broadcasted_iota(jnp.int32, (cl, cl), 1)
    return ii, jj


def _ssd_body(z_ref, xbc_ref, ps_ref, cs0_ref, h0_ref, cw_ref, cb_ref, brow_ref, bcol_ref, arow_ref, acol_ref,
              dexp_ref, nw_ref, ex_ref, y_ref, hl_ref, st_ref, halo_ref, yacc_ref, *, cl, nc, valid_len):
    c = pl.program_id(1)

    @pl.when(c == 0)
    def _():
        st_ref[...] = h0_ref[...]
        halo_ref[...] = cs0_ref[...]

    n_pairs = SSM_HEADS // 2
    pw = 2 * SSM_HEAD_DIM
    states = [st_ref[j] for j in range(n_pairs)]
    x_raw = xbc_ref[...]
    conv = _causal_conv(x_raw, halo_ref[...], cw_ref, 0, SSM_CONV_DIM)
    xbc = _silu(conv + cb_ref[...])
    x = xbc[:, :SSM_D_INNER]
    ps = ps_ref[...]
    step_c = _softplus(ps + brow_ref[...])
    step_r = _softplus(ps.T + bcol_ref[...])
    if valid_len is not None:
        t_c = lax.broadcasted_iota(jnp.int32, step_c.shape, 0) + c * cl
        t_r = lax.broadcasted_iota(jnp.int32, step_r.shape, 1) + c * cl
        step_c = jnp.where(t_c < valid_len, step_c, 0.0)
        step_r = jnp.where(t_r < valid_len, step_r, 0.0)
    la_c = step_c * -jnp.exp(arow_ref[...])
    la_r = (step_r * -jnp.exp(acol_ref[...]))[PS_DT:PS_DT + SSM_HEADS, :]
    ii, jj = _tri(cl)
    incl = ii >= jj
    acs_c = _dot_hi(incl.astype(F32), la_c)
    acs_r = _dot_hi(la_r, (ii <= jj).astype(F32))
    last = acs_c[cl - 1:cl, :]
    e_last = jnp.exp(last)

    def expand(a):
        hi, lo = _split(a)
        return _dot(hi, ex_ref[...]) + _dot(lo, ex_ref[...])

    xs = x * expand(step_c)
    xe = (xs * expand(jnp.exp(last - acs_c))).astype(BF16)
    e_acs = expand(jnp.exp(acs_c))
    xs16 = xs.astype(BF16)
    lane = lax.broadcasted_iota(jnp.int32, (cl, pw), 1)
    srow = lax.broadcasted_iota(jnp.int32, (pw, 1), 0)
    ppg = n_pairs // SSM_GROUPS
    for g in range(SSM_GROUPS):
        b_g = xbc[:, SSM_D_INNER + g * SSM_STATE:SSM_D_INNER + (g + 1) * SSM_STATE].astype(BF16)
        c0 = SSM_D_INNER + SSM_GROUPS * SSM_STATE + g * SSM_STATE
        c_g = xbc[:, c0:c0 + SSM_STATE].astype(BF16)
        cb = _dot_nt(c_g, b_g)
        for r in range(ppg):
            j = g * ppg + r
            cols = slice(j * pw, (j + 1) * pw)
            y_heads = []
            for h in (2 * j, 2 * j + 1):
                dec = jnp.where(incl, jnp.exp(jnp.where(incl, acs_c[:, h:h + 1] - acs_r[h:h + 1, :], 0.0)), 0.0)
                y_heads.append(_dot((cb * dec).astype(BF16), xs16[:, cols]))
            y_diag = jnp.where(lane < SSM_HEAD_DIM, y_heads[0], y_heads[1])
            hst = states[j]
            y_off = _dot_nt(c_g, hst.astype(BF16)) * e_acs[:, cols]
            keep = jnp.where(srow < SSM_HEAD_DIM, e_last[:, 2 * j:2 * j + 1], e_last[:, 2 * j + 1:2 * j + 2])
            states[j] = hst * keep + _dot_tn(xe[:, cols], b_g)
            yacc_ref[:, cols] = y_diag + y_off
    for j in range(n_pairs):
        st_ref[j] = states[j]
    halo_ref[...] = x_raw[cl - 8:, :]
    y = (yacc_ref[...] + dexp_ref[...] * x) * _silu(z_ref[...])
    gw = SSM_D_INNER // SSM_GROUPS
    parts = []
    for g in range(SSM_GROUPS):
        yg = y[:, g * gw:(g + 1) * gw]
        parts.append(yg * lax.rsqrt(jnp.mean(yg * yg, axis=-1, keepdims=True) + EPS))
    y_ref[...] = (jnp.concatenate(parts, axis=1) * nw_ref[...]).astype(y_ref.dtype)

    @pl.when(c == nc - 1)
    def _():
        hl_ref[...] = st_ref[...]


def _ssd(proj, ps, conv0, h0, conv_w, conv_b, vec, ssm_d, ssm_norm, bsz, seq, cl, valid_len):
    nc = seq // cl
    n = bsz * seq
    brow, bcol, arow, acol = vec
    cs0 = jnp.pad(conv0, ((0, 0), (8 - (CONV_K - 1), 0), (0, 0)))
    dexp = jnp.repeat(ssm_d, SSM_HEAD_DIM).reshape(1, SSM_D_INNER)
    expand = (jnp.arange(SSM_D_INNER)[None, :] // SSM_HEAD_DIM == jnp.arange(PS_W)[:, None] - PS_DT).astype(BF16)
    n_pairs, pw = SSM_HEADS // 2, 2 * SSM_HEAD_DIM
    st_shape = (bsz, n_pairs, pw, SSM_STATE)
    if isinstance(h0, tuple):
        h0p = h0[0].reshape((h0[0].shape[0],) + st_shape)
        layer = h0[1]
        h0_spec = BS((None, None, n_pairs, pw, SSM_STATE), lambda b, c: (layer, b, 0, 0, 0))
    else:
        h0p = h0.reshape(st_shape)
        h0_spec = BS((None, n_pairs, pw, SSM_STATE), lambda b, c: (b, 0, 0, 0))
    const = lambda shp: BS(shp, lambda b, c: tuple(0 for _ in shp))
    y, h_last = pl.pallas_call(
        functools.partial(_ssd_body, cl=cl, nc=nc, valid_len=valid_len),
        out_shape=(S((n, SSM_D_INNER), BF16), S(st_shape, F32)), grid=(bsz, nc),
        in_specs=[BS((cl, SSM_D_INNER), lambda b, c: (b * nc + c, C_SZ // SSM_D_INNER)),
                  BS((cl, SSM_CONV_DIM), lambda b, c: (b * nc + c, C_XBC // SSM_CONV_DIM)),
                  BS((cl, PS_W), lambda b, c: (b * nc + c, 0)),
                  BS((None, 8, SSM_CONV_DIM), lambda b, c: (b, 0, 0)),
                  h0_spec,
                  const((CONV_K, SSM_CONV_DIM)), const((1, SSM_CONV_DIM)),
                  const((1, PS_W)), const((PS_W, 1)), const((1, PS_W)), const((PS_W, 1)),
                  const((1, SSM_D_INNER)), const((1, SSM_D_INNER)), const((PS_W, SSM_D_INNER))],
        out_specs=(BS((cl, SSM_D_INNER), lambda b, c: (b * nc + c, 0)),
                   BS((None, n_pairs, pw, SSM_STATE), lambda b, c: (b, 0, 0, 0))),
        scratch_shapes=[pltpu.VMEM((n_pairs, pw, SSM_STATE), F32), pltpu.VMEM((8, SSM_CONV_DIM), F32),
                        pltpu.VMEM((cl, SSM_D_INNER), F32)],
        compiler_params=_cparams(("parallel", "arbitrary")), name="ssd",
    )(proj, proj, ps, cs0, h0p, conv_w, conv_b.reshape(1, -1), brow, bcol, arow, acol, dexp,
      ssm_norm.reshape(1, SSM_D_INNER), expand)
    return y, h_last.reshape(bsz, SSM_HEADS, SSM_HEAD_DIM, SSM_STATE)


def _unit_lower_inverses(mats, ii, jj, cl):
    eye = (ii == jj).astype(F32)
    pair = ((ii >> 1) == (jj >> 1)) & (ii > jj)
    invs = [eye - jnp.where(pair, a, 0.0) for a in mats]
    s = 2
    while s < cl:
        blk = ((ii // (2 * s)) == (jj // (2 * s))) & ((ii % (2 * s)) >= s) & ((jj % (2 * s)) < s)
        xs = [_dot3(inv, jnp.where(blk, a, 0.0)) for inv, a in zip(invs, mats)]
        invs = [inv - _dot3(x, inv) for inv, x in zip(invs, xs)]
        s *= 2
    return invs


def _gdn_body(q_ref, k_ref, v_ref, z_ref, ps_ref, cs0_ref, s0_ref, cw_ref, brow_ref, bcol_ref, arow_ref, acol_ref,
              nw_ref, o_ref, sl_ref, st_ref, halo_ref, *, cl, cps, nc, valid_len):
    c = pl.program_id(1)

    @pl.when(c == 0)
    def _():
        st_ref[...] = s0_ref[...]
        halo_ref[...] = cs0_ref[...]

    w = DN_W
    rows = cl * cps
    halo = halo_ref[...]
    states = [st_ref[h] for h in range(DN_HEADS)]
    raws = (q_ref[...], k_ref[...], v_ref[...])
    q, k, v = (_silu(_causal_conv(raws[p], halo[:, p * w:(p + 1) * w], cw_ref, p * w, (p + 1) * w)) for p in range(3))
    z = z_ref[...]
    ps = ps_ref[...]
    g_c = -jnp.exp(arow_ref[...]) * _softplus(ps + brow_ref[...])
    g_r = -jnp.exp(acol_ref[...]) * _softplus(ps.T + bcol_ref[...])
    beta = _sigmoid(ps)
    if valid_len is not None:
        t_c = lax.broadcasted_iota(jnp.int32, g_c.shape, 0) + c * rows
        t_r = lax.broadcasted_iota(jnp.int32, g_r.shape, 1) + c * rows
        g_c = jnp.where(t_c < valid_len, g_c, 0.0)
        beta = jnp.where(t_c < valid_len, beta, 0.0)
        g_r = jnp.where(t_r < valid_len, g_r, 0.0)
    ii, jj = _tri(cl)
    incl = ii >= jj
    strict = ii > jj
    lower_ones = incl.astype(F32)
    upper_ones = (ii <= jj).astype(F32)
    off = PS_DECAY - 16
    nw = nw_ref[...]
    pre = []
    for ci in range(cps):
        rs = slice(ci * cl, (ci + 1) * cl)
        gcs_c = _dot_hi(lower_ones, g_c[rs, 16:24])
        gcs_r = _dot_hi(g_r[16:24, rs], upper_ones)
        for h in range(DN_HEADS):
            sl = slice(h * DN_DK, (h + 1) * DN_DK)
            qh = q[rs, sl]
            kh = k[rs, sl]
            qh = qh * lax.rsqrt(jnp.sum(qh * qh, axis=-1, keepdims=True) + EPS) * (DN_DK ** -0.5)
            kh = kh * lax.rsqrt(jnp.sum(kh * kh, axis=-1, keepdims=True) + EPS)
            bh = beta[rs, PS_BETA + h:PS_BETA + h + 1]
            col = gcs_c[:, off + h:off + h + 1]
            row = gcs_r[off + h:off + h + 1, :]
            last = gcs_c[cl - 1:cl, off + h:off + h + 1]
            e_col = jnp.exp(col)
            kb = kh * bh
            pre.append(dict(
                dec=jnp.where(incl, jnp.exp(jnp.where(incl, col - row, 0.0)), 0.0),
                q16=qh.astype(BF16), k16=kh.astype(BF16), kb16=kb.astype(BF16),
                vb=v[rs, sl] * bh, kbe=kb * e_col,
                qd16=(qh * e_col).astype(BF16),
                kd16=(kh * jnp.exp(last - col)).astype(BF16),
                g_end=jnp.exp(last),
                gate=nw * _silu(z[rs, sl])))
    lowers = [jnp.where(strict, _dot_nt(t["kb16"], t["k16"]) * t["dec"], 0.0) for t in pre]
    qks = [(_dot_nt(t["q16"], t["k16"]) * t["dec"]).astype(BF16) for t in pre]
    tinvs = _unit_lower_inverses(lowers, ii, jj, cl)
    us = [_dot3(ti, t["vb"]) for ti, t in zip(tinvs, pre)]
    ws = [_dot3(ti, t["kbe"]).astype(BF16) for ti, t in zip(tinvs, pre)]
    outs = []
    for ci in range(cps):
        items = range(ci * DN_HEADS, (ci + 1) * DN_HEADS)
        s16 = [st.astype(BF16) for st in states]
        v16 = [(us[i] - _dot(ws[i], s16[h])).astype(BF16) for h, i in enumerate(items)]
        os_ = [_dot(pre[i]["qd16"], s16[h]) + _dot(qks[i], v16[h]) for h, i in enumerate(items)]
        states = [states[h] * pre[i]["g_end"] + _dot_tn(pre[i]["kd16"], v16[h]) for h, i in enumerate(items)]
        outs += [o * lax.rsqrt(jnp.mean(o * o, axis=-1, keepdims=True) + EPS) * pre[i]["gate"]
                 for o, i in zip(os_, items)]
    for ci in range(cps):
        for h in range(DN_HEADS):
            o_ref[ci * cl:(ci + 1) * cl, h * DN_DV:(h + 1) * DN_DV] = outs[ci * DN_HEADS + h].astype(o_ref.dtype)
    for h in range(DN_HEADS):
        st_ref[h] = states[h]
    for p in range(3):
        halo_ref[:, p * w:(p + 1) * w] = raws[p][rows - 8:, :]

    @pl.when(c == nc - 1)
    def _():
        sl_ref[...] = st_ref[...]


def _gdn(proj, ps, conv0, s0, conv_w, vec, dn_norm, bsz, seq, cl, cps, valid_len):
    rows = cl * cps
    nc = seq // rows
    n = bsz * seq
    brow, bcol, arow, acol = vec
    cs0 = jnp.pad(conv0, ((0, 0), (8 - (CONV_K - 1), 0), (0, 0)))
    w = DN_W
    if isinstance(s0, tuple):
        s0, layer = s0
        s0_spec = BS((None, None, DN_HEADS, DN_DK, DN_DV), lambda b, c: (layer, b, 0, 0, 0))
    else:
        s0_spec = BS((None, DN_HEADS, DN_DK, DN_DV), lambda b, c: (b, 0, 0, 0))
    const = lambda shp: BS(shp, lambda b, c: tuple(0 for _ in shp))
    blk = lambda col: BS((rows, w), lambda b, c: (b * nc + c, col // w))
    o, s_last = pl.pallas_call(
        functools.partial(_gdn_body, cl=cl, cps=cps, nc=nc, valid_len=valid_len),
        out_shape=(S((n, w), BF16), S((bsz, DN_HEADS, DN_DK, DN_DV), F32)), grid=(bsz, nc),
        in_specs=[blk(C_DQ), blk(C_DK), blk(C_DV), blk(C_DZ),
                  BS((rows, PS_W), lambda b, c: (b * nc + c, 0)),
                  BS((None, 8, DN_CONV_DIM), lambda b, c: (b, 0, 0)),
                  s0_spec,
                  const((CONV_K, DN_CONV_DIM)),
                  const((1, PS_W)), const((PS_W, 1)), const((1, PS_W)), const((PS_W, 1)), const((1, DN_DV))],
        out_specs=(BS((rows, w), lambda b, c: (b * nc + c, 0)),
                   BS((None, DN_HEADS, DN_DK, DN_DV), lambda b, c: (b, 0, 0, 0))),
        scratch_shapes=[pltpu.VMEM((DN_HEADS, DN_DK, DN_DV), F32), pltpu.VMEM((8, DN_CONV_DIM), F32)],
        compiler_params=_cparams(("parallel", "arbitrary")), name="gdn",
    )(proj, proj, proj, proj, ps, cs0, s0, conv_w, brow, bcol, arow, acol, dn_norm.reshape(1, DN_DV))
    return o, s_last


def _router_body(h_ref, lw_ref, rw_ref, rb_ref, meta_ref):
    x = h_ref[...]
    xn = x * lax.rsqrt(jnp.mean(x * x, axis=-1, keepdims=True) + EPS) * lw_ref[...]
    logits = _dot3(xn, rw_ref[...]) + rb_ref[...]
    lane = lax.broadcasted_iota(jnp.int32, logits.shape, 1)
    big = jnp.int32(logits.shape[1])
    m1 = jnp.max(logits, axis=1, keepdims=True)
    i1 = jnp.min(jnp.where(logits == m1, lane, big), axis=1, keepdims=True)
    rest = jnp.where(lane == i1, NEG, logits)
    m2 = jnp.max(rest, axis=1, keepdims=True)
    i2 = jnp.min(jnp.where(rest == m2, lane, big), axis=1, keepdims=True)
    e2 = jnp.exp(m2 - m1)
    g1 = 1.0 / (1.0 + e2)
    g2 = e2 * g1
    meta = jnp.where(lane == 0, i1.astype(F32), jnp.where(lane == 1, i2.astype(F32),
                     jnp.where(lane == 2, g1, jnp.where(lane == 3, g2, 0.0))))
    meta_ref[...] = meta


def _router(h, ln_w, router_w, router_b, tm):
    n, d = h.shape
    rw = jnp.pad(router_w, ((0, 0), (0, 128 - N_EXPERTS)))
    rb = jnp.pad(router_b, (0, 128 - N_EXPERTS), constant_values=NEG).reshape(1, 128)
    const = lambda shp: BS(shp, lambda i: (0, 0))
    return pl.pallas_call(
        _router_body, out_shape=S((n, 128), F32), grid=(n // tm,),
        in_specs=[BS((tm, d), lambda i: (i, 0)), const((1, d)), const((d, 128)), const((1, 128))],
        out_specs=BS((tm, 128), lambda i: (i, 0)),
        compiler_params=_cparams(("parallel",)), name="router")(h, ln_w.reshape(1, d), rw, rb)


def _experts_body(be_ref, tok_ref, nused_ref, h_hbm, lw_ref, wg_ref, wu_ref, wd_ref, y_ref,
                  xg_ref, xb_ref, acc_ref, sem, *, tm, nf):
    i = pl.program_id(0)
    f = pl.program_id(1)
    used = i < nused_ref[0]
    slot = lax.rem(i, 2)

    def gather(blk, dst):
        def start(r, carry):
            pltpu.make_async_copy(h_hbm.at[pl.ds(tok_ref[blk * tm + r], 1), :],
                                  xg_ref.at[dst, pl.ds(r, 1), :], sem.at[dst]).start()
            return carry

        lax.fori_loop(0, tm, start, 0, unroll=8)

    @pl.when((i == 0) & (f == 0))
    def _():
        gather(0, 0)

    @pl.when(used & (f == 0))
    def _():
        pltpu.make_async_copy(h_hbm.at[pl.ds(0, tm), :], xg_ref.at[slot], sem.at[slot]).wait()
        x = xg_ref[slot]
        xn = x * lax.rsqrt(jnp.mean(x * x, axis=-1, keepdims=True) + EPS) * lw_ref[...]
        xb_ref[...] = xn.astype(BF16)
        acc_ref[...] = jnp.zeros(acc_ref.shape, F32)

    chunk = tm // nf

    @pl.when(used)
    def _():
        nxt = jnp.minimum(i + 1, nused_ref[0] - 1)
        for r in range(chunk):
            pltpu.make_async_copy(h_hbm.at[pl.ds(tok_ref[nxt * tm + f * chunk + r], 1), :],
                                  xg_ref.at[1 - slot, pl.ds(f * chunk + r, 1), :], sem.at[1 - slot]).start()
        xb = xb_ref[...]
        hid = _silu(_dot(xb, wg_ref[...].astype(BF16))) * _dot(xb, wu_ref[...].astype(BF16))
        acc_ref[...] += _dot(hid.astype(BF16), wd_ref[...].astype(BF16))

    @pl.when((i == nused_ref[0] - 1) & (f == nf - 1))
    def _():
        pltpu.make_async_copy(h_hbm.at[pl.ds(0, tm), :], xg_ref.at[1 - slot], sem.at[1 - slot]).wait()

    @pl.when(used & (f == nf - 1))
    def _():
        y_ref[...] = acc_ref[...]

    @pl.when(jnp.logical_not(used) & (f == nf - 1))
    def _():
        y_ref[...] = jnp.zeros(y_ref.shape, F32)


def _experts(h, ln_w, block_expert, row_token, n_used, wg, wu, wd, tm, tf):
    n, d = h.shape
    rows = row_token.shape[0]
    n_blocks = rows // tm
    (wg, layer), (wu, _), (wd, _) = wg, wu, wd
    ff = wg.shape[3]
    nf = ff // tf
    grid_spec = pltpu.PrefetchScalarGridSpec(
        num_scalar_prefetch=3, grid=(n_blocks, nf),
        in_specs=[BS(memory_space=pl.ANY),
                  BS((1, d), lambda i, f, be, tok, nu: (0, 0)),
                  BS((None, None, d, tf), lambda i, f, be, tok, nu: (layer, be[i], 0, f)),
                  BS((None, None, d, tf), lambda i, f, be, tok, nu: (layer, be[i], 0, f)),
                  BS((None, None, tf, d), lambda i, f, be, tok, nu: (layer, be[i], f, 0))],
        out_specs=BS((tm, d), lambda i, f, be, tok, nu: (i, 0)),
        scratch_shapes=[pltpu.VMEM((2, tm, d), F32), pltpu.VMEM((tm, d), BF16), pltpu.VMEM((tm, d), F32),
                        pltpu.SemaphoreType.DMA((2,))])
    return pl.pallas_call(
        functools.partial(_experts_body, tm=tm, nf=nf),
        out_shape=S((rows, d), F32), grid_spec=grid_spec,
        compiler_params=_cparams(("arbitrary", "arbitrary")), name="experts",
    )(block_expert, row_token, n_used, h, ln_w.reshape(1, d), wg, wu, wd)


def _combine_body(pos_ref, y_hbm, h_ref, meta_ref, o_ref, ya_ref, yb_ref, sem, *, tc, n_steps):
    i = pl.program_id(0)
    slot = lax.rem(i, 2)

    def gather(step, dst):
        def start(r, carry):
            t = step * tc + r
            pltpu.make_async_copy(y_hbm.at[pl.ds(pos_ref[2 * t], 1), :], ya_ref.at[dst, pl.ds(r, 1), :],
                                  sem.at[dst]).start(priority=0)
            pltpu.make_async_copy(y_hbm.at[pl.ds(pos_ref[2 * t + 1], 1), :], yb_ref.at[dst, pl.ds(r, 1), :],
                                  sem.at[dst]).start(priority=1)
            return carry

        lax.fori_loop(0, tc, start, 0, unroll=8)

    @pl.when(i == 0)
    def _():
        gather(0, 0)

    @pl.when(i + 1 < n_steps)
    def _():
        gather(i + 1, 1 - slot)

    pltpu.make_async_copy(y_hbm.at[pl.ds(0, tc), :], ya_ref.at[slot], sem.at[slot]).wait()
    pltpu.make_async_copy(y_hbm.at[pl.ds(0, tc), :], yb_ref.at[slot], sem.at[slot]).wait()
    meta = meta_ref[...]
    o_ref[...] = h_ref[...] + (meta[:, 2:3] * ya_ref[slot] + meta[:, 3:4] * yb_ref[slot])


def _combine(h, y_sorted, pos, meta, tc):
    n, d = h.shape
    n_steps = n // tc
    grid_spec = pltpu.PrefetchScalarGridSpec(
        num_scalar_prefetch=1, grid=(n_steps,),
        in_specs=[BS(memory_space=pl.ANY), BS((tc, d), lambda i, pos: (i, 0)), BS((tc, 128), lambda i, pos: (i, 0))],
        out_specs=BS((tc, d), lambda i, pos: (i, 0)),
        scratch_shapes=[pltpu.VMEM((2, tc, d), F32), pltpu.VMEM((2, tc, d), F32), pltpu.SemaphoreType.DMA((2,))])
    return pl.pallas_call(
        functools.partial(_combine_body, tc=tc, n_steps=n_steps), out_shape=S((n, d), F32), grid_spec=grid_spec,
        compiler_params=_cparams(("arbitrary",)), name="moe_combine")(pos, y_sorted, h, meta)


def _moe(h, ln_w, router_w, router_b, wg, wu, wd, tm_r, tm, tf, tc):
    n, d = h.shape
    meta = _router(h, ln_w, router_w, router_b, tm_r)
    idx = meta[:, 0:2].astype(jnp.int32)
    member = jnp.sum(jax.nn.one_hot(idx, N_EXPERTS, dtype=jnp.int32), axis=1)
    before = jnp.cumsum(member, axis=0) - member
    counts = jnp.sum(member, axis=0)
    padded = (counts + tm - 1) // tm * tm
    pad_ends = jnp.cumsum(padded)
    pad_starts = pad_ends - padded
    pos = (pad_starts[idx] + jnp.take_along_axis(before, idx, axis=1)).astype(jnp.int32)
    n_blocks = -(-(2 * n + N_EXPERTS * (tm - 1)) // tm)
    rows = n_blocks * tm
    tok = jnp.broadcast_to(jnp.arange(n, dtype=jnp.int32)[:, None], (n, 2))
    row_token = jnp.zeros((rows,), jnp.int32).at[pos.reshape(-1)].set(tok.reshape(-1))
    n_used = (pad_ends[-1] // tm).astype(jnp.int32).reshape(1)
    blk_start = jnp.minimum(jnp.arange(n_blocks, dtype=jnp.int32), n_used[0] - 1) * tm
    block_expert = jnp.minimum(jnp.searchsorted(pad_ends, blk_start, side="right"), N_EXPERTS - 1).astype(jnp.int32)
    y_sorted = _experts(h, ln_w, block_expert, row_token, n_used, wg, wu, wd, tm, tf)
    return _combine(h, y_sorted, pos.reshape(-1), meta, tc)


def _layer(i, h, p, lw, cfg, att_fn, ssm_conv0, ssm_h0, dn_conv0, dn_s0):
    bsz, seq, tm, cl_ssd, cl_gdn, pad_to, valid_len = (cfg[k] for k in
                                                      ("bsz", "seq", "tm", "cl_ssd", "cl_gdn", "pad_to", "valid_len"))
    n = bsz * seq
    xn = _rmsnorm(h, lw["ln_mix"], tm)
    ident = lambda d, e: d[0]
    tn = 512
    rot = PROJ_A_ROT // tn
    n_a = PROJ_A_W // tn
    tm_in = _pick(n, 2 * tm)
    proj_a = _fused_matmul([xn], [(0, lw["w_in"], 0)], ident, [], F32, PROJ_A_W, tm_in, tn, "in_proj_a",
                           out_tile=lambda j: lax.rem(j + (n_a - rot), n_a))
    proj_b = _fused_matmul([xn], [(0, lw["w_tail"], 0)], ident, [], F32, PROJ_B_W, tm_in, tn, "in_proj_b")
    ps = _fused_matmul([xn], [(0, lw["w_small"], 0)], ident, [], F32, PS_W, tm, PS_W, "in_proj_small")

    q16, k16, v16, k32, v32 = _qkv_prep(proj_a, lw["q_norm"], lw["k_norm"], tm, i, cfg["depth"], cfg.get("kv_rows"))
    cfg["kv_rows"] = (k32, v32)
    att = att_fn(q16, k16, v16)

    if pad_to is None:
        pa_r, pb_r, ps_r, seq_r = proj_a, proj_b, ps, seq
    else:
        seq_r = pad_to
        padded = lambda a: jnp.pad(a.reshape(bsz, seq, -1), ((0, 0), (0, pad_to - seq), (0, 0))).reshape(bsz * pad_to, -1)
        pa_r, pb_r, ps_r = padded(proj_a), padded(proj_b), padded(ps)
    y, ssm_h1 = _ssd(pa_r, ps_r, ssm_conv0, ssm_h0, lw["ssm_conv_w"], lw["ssm_conv_b"], lw["vec"], lw["ssm_d"],
                     lw["ssm_norm"], bsz, seq_r, cl_ssd, valid_len)
    o, dn_s1 = _gdn(pb_r, ps_r, dn_conv0, dn_s0, lw["dn_conv_w"], lw["vec"], lw["dn_norm"], bsz, seq_r, cl_gdn,
                    cfg["cps_gdn"], valid_len)
    if pad_to is not None:
        y = y.reshape(bsz, pad_to, -1)[:, :seq].reshape(n, -1)
        o = o.reshape(bsz, pad_to, -1)[:, :seq].reshape(n, -1)

    def merge(d, e):
        return _sigmoid(d[0]) * d[3] + _sigmoid(d[1]) * d[4] + _sigmoid(d[2]) * d[5]

    d_m = D_MODEL
    merged = _fused_matmul([xn, att, y, o],
                           [(0, lw["w_gate"], 0), (0, lw["w_gate"], d_m), (0, lw["w_gate"], 2 * d_m),
                            (1, lw["w_up_att"], 0), (2, lw["w_up_ssm"], 0), (3, lw["w_up_dn"], 0)],
                           merge, [], BF16, d_m, tm, 512, "merge")
    resid = lambda d, e: e[0] + d[0]
    h = _fused_matmul([merged], [(0, lw["w_out"], 0)], resid, [h], F32, d_m, tm_in, 512, "out_proj")

    if i % 2 == 0:
        hn = _rmsnorm(h, lw["ln_ffn"], tm)
        d_ff = lw["d_ff"]
        ff = _fused_matmul([hn], [(0, lw["ffn_w_gate"], 0), (0, lw["ffn_w_up"], 0)],
                           lambda d, e: _silu(d[0]) * d[1], [], BF16, d_ff, tm, 1408, "ffn_up")
        h = _fused_matmul([ff], [(0, lw["ffn_w_down"], 0)], resid, [h], F32, d_m, tm, 512, "ffn_down")
    else:
        h = _moe(h, lw["ln_ffn"], lw["router_w"], lw["router_b"], lw["exp_w_gate"], lw["exp_w_up"], lw["exp_w_down"],
                 min(tm, 512), cfg["tm_moe"], TF_MOE, cfg["tc"])

    hn = _rmsnorm(h, lw["ln_ple"], tm)
    h = _fused_matmul([hn, p.astype(BF16)], [(0, lw["ple_w_gate"], 0), (1, lw["ple_w_proj"], 0)],
                      lambda d, e: e[0] + _sigmoid(d[0]) * d[1], [h], F32, d_m, tm_in, 512, "ple")

    keep = CONV_K - 1
    tail = max(seq - keep, 0)
    xbc_raw = proj_a.reshape(bsz, seq, -1)[:, tail:, C_XBC:C_XBC + SSM_CONV_DIM]
    dn_raw = proj_b.reshape(bsz, seq, -1)[:, tail:, C_DQ:C_DQ + DN_CONV_DIM]
    if seq >= keep:
        ssm_conv1, dn_conv1 = xbc_raw, dn_raw
    else:
        ssm_conv1 = jnp.concatenate([ssm_conv0, xbc_raw], axis=1)[:, -keep:]
        dn_conv1 = jnp.concatenate([dn_conv0, dn_raw], axis=1)[:, -keep:]
    return h, (None, None, ssm_conv1, ssm_h1, dn_conv1, dn_s1)


def _prep_layer_weights(i, W):
    w_in = W["w_in"][i]
    o_dt = PROJ_A_W
    o_tail = o_dt + SSM_HEADS
    o_beta = o_tail + PROJ_B_W
    w_tail = w_in[:, o_tail:o_beta].astype(BF16)
    w_small = jnp.concatenate([w_in[:, o_dt:o_tail], w_in[:, o_beta:o_beta + 2 * DN_HEADS],
                               jnp.zeros((D_MODEL, PS_W - SSM_HEADS - 2 * DN_HEADS), w_in.dtype)], axis=1)
    bias_row = jnp.zeros((PS_W,), F32).at[PS_DT:PS_DT + SSM_HEADS].set(W["ssm_dt_bias"][i])
    bias_row = bias_row.at[PS_DECAY:PS_DECAY + DN_HEADS].set(W["dn_dt_bias"][i])
    alog_row = jnp.zeros((PS_W,), F32).at[PS_DT:PS_DT + SSM_HEADS].set(W["ssm_a_log"][i])
    alog_row = alog_row.at[PS_DECAY:PS_DECAY + DN_HEADS].set(W["dn_a_log"][i])
    vec = (bias_row.reshape(1, PS_W), bias_row.reshape(PS_W, 1), alog_row.reshape(1, PS_W), alog_row.reshape(PS_W, 1))
    lw = {
        "ln_mix": W["ln_mix"][i], "w_in": (W["w_in"], i), "w_tail": w_tail, "w_small": w_small.astype(BF16),
        "q_norm": W["q_norm"][i], "k_norm": W["k_norm"][i], "att_subln": W["att_subln"][i],
        "ssm_conv_w": W["ssm_conv_w"][i], "ssm_conv_b": W["ssm_conv_b"][i], "vec": vec,
        "ssm_d": W["ssm_d"][i], "ssm_norm": W["ssm_norm"][i],
        "dn_conv_w": W["dn_conv_w"][i], "dn_norm": W["dn_norm"][i],
        "w_gate": (W["w_gate"], i), "w_up_att": (W["w_up_att"], i), "w_up_ssm": (W["w_up_ssm"], i),
        "w_up_dn": (W["w_up_dn"], i), "w_out": (W["w_out"], i),
        "ln_ffn": W["ln_ffn"][i], "ln_ple": W["ln_ple"][i],
        "ple_w_gate": (W["ple_w_gate"], i), "ple_w_proj": (W["ple_w_proj"], i),
    }
    if i % 2 == 0:
        lw.update(ffn_w_gate=(W["ffn_w_gate"], i // 2), ffn_w_up=(W["ffn_w_up"], i // 2),
                  ffn_w_down=(W["ffn_w_down"], i // 2), d_ff=W["ffn_w_gate"].shape[2])
    else:
        lw.update(router_w=W["router_w"][i // 2], router_b=W["router_b"][i // 2],
                  exp_w_gate=(W["exp_w_gate"], i // 2), exp_w_up=(W["exp_w_up"], i // 2),
                  exp_w_down=(W["exp_w_down"], i // 2))
    lam_init = 0.8 - 0.6 * math.exp(-0.3 * i)
    lam = (jnp.exp(jnp.sum(W["lam_q1"][i] * W["lam_k1"][i])) - jnp.exp(jnp.sum(W["lam_q2"][i] * W["lam_k2"][i]))
           + lam_init).astype(F32)
    return lw, lam, lam_init


def _pick(n, pref):
    t = min(n, pref)
    while n % t:
        t //= 2
    return t


def kernel(x_prompt, x_sample, cache_k, cache_v, state_ssm_conv, state_ssm, state_dn_conv, state_dn, page_table, p_prompt, p_sample, ln_mix, w_in, w_gate, q_norm, k_norm, lam_q1, lam_k1, lam_q2, lam_k2, att_subln, rel_bias, ssm_conv_w, ssm_conv_b, ssm_dt_bias, ssm_a_log, ssm_d, ssm_norm, dn_conv_w, dn_dt_bias, dn_a_log, dn_norm, w_up_att, w_up_ssm, w_up_dn, w_out, ln_ffn, ffn_w_gate, ffn_w_up, ffn_w_down, router_w, router_b, exp_w_gate, exp_w_up, exp_w_down, ln_ple, ple_w_gate, ple_w_proj):
    W = dict(ln_mix=ln_mix, w_in=w_in, w_gate=w_gate, q_norm=q_norm, k_norm=k_norm, lam_q1=lam_q1, lam_k1=lam_k1,
             lam_q2=lam_q2, lam_k2=lam_k2, att_subln=att_subln, ssm_conv_w=ssm_conv_w, ssm_conv_b=ssm_conv_b,
             ssm_dt_bias=ssm_dt_bias, ssm_a_log=ssm_a_log, ssm_d=ssm_d, ssm_norm=ssm_norm, dn_conv_w=dn_conv_w,
             dn_dt_bias=dn_dt_bias, dn_a_log=dn_a_log, dn_norm=dn_norm, w_up_att=w_up_att, w_up_ssm=w_up_ssm,
             w_up_dn=w_up_dn, w_out=w_out, ln_ffn=ln_ffn, ffn_w_gate=ffn_w_gate, ffn_w_up=ffn_w_up,
             ffn_w_down=ffn_w_down, router_w=router_w, router_b=router_b, exp_w_gate=exp_w_gate, exp_w_up=exp_w_up,
             exp_w_down=exp_w_down, ln_ple=ln_ple, ple_w_gate=ple_w_gate, ple_w_proj=ple_w_proj)
    depth = ln_mix.shape[0]
    bp, lp, d = x_prompt.shape
    bs, ls, _ = x_sample.shape
    n_pages = page_table.shape[1]
    t_att = _pick(lp, T_ATT)
    cfg_p = dict(bsz=bp, seq=lp, tm=_pick(bp * lp, TM), cl_ssd=_pick(lp, CL_SSD), cl_gdn=_pick(lp, CL_GDN), pad_to=None,
                 valid_len=None, tm_moe=_pick(bp * lp, TM_MOE), tc=_pick(bp * lp, TC_MOE))
    cfg_p["cps_gdn"] = _pick(lp // cfg_p["cl_gdn"], CPS_GDN)
    cfg_s = dict(bsz=bs, seq=ls, tm=bs * ls, cl_ssd=CL_SAMPLE, cl_gdn=CL_SAMPLE, cps_gdn=1, pad_to=CL_SAMPLE,
                 valid_len=ls,
                 tm_moe=64, tc=bs * ls)
    cfg_p["depth"] = cfg_s["depth"] = depth
    hp = x_prompt.reshape(bp * lp, d)
    hs = x_sample.reshape(bs * ls, d)
    st_p, st_s = [], []
    tabs = _bias_tables(rel_bias, t_att)
    for i in range(depth):
        lw, lam, lam_init = _prep_layer_weights(i, W)
        post = 1.0 - lam_init
        att_p = lambda q, k, v: _attn_prompt(q, k, v, lam, rel_bias, tabs, lw["att_subln"], post, bp, lp, t_att)
        hp, st = _layer(i, hp, p_prompt[i].reshape(bp * lp, -1), lw, cfg_p, att_p,
                        jnp.zeros((bp, CONV_K - 1, SSM_CONV_DIM), F32),
                        jnp.zeros((bp, SSM_HEADS, SSM_HEAD_DIM, SSM_STATE), F32),
                        jnp.zeros((bp, CONV_K - 1, DN_CONV_DIM), F32),
                        jnp.zeros((bp, DN_HEADS, DN_DK, DN_DV), F32))
        st_p.append(st)
        att_s = lambda q, k, v: _attn_decode(i, q, k, v, cache_k, cache_v, page_table, lam, rel_bias,
                                             lw["att_subln"], post, _pick(n_pages, PAGES_PER_STEP))
        hs, st = _layer(i, hs, p_sample[i].reshape(bs * ls, -1), lw, cfg_s, att_s,
                        state_ssm_conv[i], (state_ssm, i), state_dn_conv[i], (state_dn, i))
        st_s.append(st)
    stk = lambda sts, j: jnp.stack([s[j] for s in sts])
    kv_p = [a.reshape(depth, bp, lp, ATT_HEADS, ATT_DV) for a in cfg_p["kv_rows"]]
    kv_s = [a.reshape(depth, bs, ls, ATT_HEADS, ATT_DV) for a in cfg_s["kv_rows"]]
    return (hp.reshape(bp, lp, d), hs.reshape(bs, ls, d),
            kv_p[0], kv_p[1], stk(st_p, 2), stk(st_p, 3), stk(st_p, 4), stk(st_p, 5),
            kv_s[0], kv_s[1], stk(st_s, 2), stk(st_s, 3), stk(st_s, 4), stk(st_s, 5))
```
